```python
import math
import jax, jax.numpy as jnp
from jax import lax
import numpy as np

D_MODEL = 1024
BATCH = 8
SEQ = 4096
DEPTH = 1

POOL_WINDOWS = (2, 4, 8, 16)
POOL_GROUPS = 4
POOL_WIDTH = D_MODEL // 2
POOL_GROUP_DIM = POOL_WIDTH // POOL_GROUPS
N_HEADS = 8
HEAD_DIM = 64
ATTN_WIDTH = N_HEADS * HEAD_DIM
KV_DIM = HEAD_DIM
IDX_HEADS = 8
IDX_DIM = 32
TOPK_MAX = 256
Q_BLOCK = 128
NEG_INF = -1e30
N_BRANCHES = 2
N_GROUPS = 4
EXPERTS_PER_GROUP = 8
N_EXPERTS = N_GROUPS * EXPERTS_PER_GROUP
EXPERT_FF = 256
ROUTE_TOPK = 2
LN_EPS = 1e-5
DEEPNORM_ALPHA = (2.0 * DEPTH) ** 0.25
DEEPNORM_BETA = (8.0 * DEPTH) ** -0.25
SPLITS = (POOL_WIDTH, ATTN_WIDTH, KV_DIM, KV_DIM, IDX_HEADS * IDX_DIM, IDX_DIM, IDX_HEADS, N_BRANCHES * D_MODEL)
IN_WIDTH = 3496

kernel_name = 'hybrid_pool_dsa_hmoe'


def layer_norm(x, g, b):
    x32 = x.astype(jnp.float32)
    mu = jnp.mean(x32, axis=-1, keepdims=True)
    xc = x32 - mu
    var = jnp.mean(xc * xc, axis=-1, keepdims=True)
    y = xc * lax.rsqrt(var + LN_EPS) * g.astype(jnp.float32) + b.astype(jnp.float32)
    return y.astype(x.dtype)


def causal_multiscale_pool(u, w_group, scale):
    B, L, _ = u.shape
    u32 = u.astype(jnp.float32).reshape(B, L, POOL_GROUPS, POOL_GROUP_DIM)
    csum = jnp.concatenate([jnp.zeros_like(u32[:, :1]), jnp.cumsum(u32, axis=1)], axis=1)
    pos = jnp.arange(L)
    deltas = []
    for g, w in enumerate(POOL_WINDOWS):
        start = jnp.maximum(pos + 1 - w, 0)
        win_sum = csum[:, 1:, g] - csum[:, start, g]
        count = (pos + 1 - start).astype(jnp.float32)
        deltas.append(win_sum / count[None, :, None] - u32[:, :, g])
    delta = jnp.stack(deltas, axis=2)
    mixed = jnp.einsum('blgd,gde->blge', delta, w_group.astype(jnp.float32))
    out = mixed.reshape(B, L, POOL_WIDTH) * scale.astype(jnp.float32)
    return out.astype(u.dtype)


def dsa_attention(q, k, v, iq, ik, iw):
    B, L, H, dh = q.shape
    n_sel = min(TOPK_MAX, L // 4)
    nb = L // Q_BLOCK
    sm_scale = 1.0 / math.sqrt(dh)
    key_pos = jnp.arange(L)
    ik32 = ik.astype(jnp.float32)

    def to_blocks(a):
        return a.reshape((B, nb, Q_BLOCK) + a.shape[2:]).swapaxes(0, 1)

    q_b, iq_b, iw_b = to_blocks(q), to_blocks(iq), to_blocks(iw)
    qpos_b = key_pos.reshape(nb, Q_BLOCK)

    def block(args):
        qb, iqb, iwb, qpos = args
        idx_logits = jnp.einsum('bqhd,bsd->bqhs', iqb.astype(jnp.float32), ik32)
        score = jnp.einsum('bqh,bqhs->bqs', iwb.astype(jnp.float32), jax.nn.relu(idx_logits))
        admissible = key_pos[None, :] <= qpos[:, None]
        score = jnp.where(admissible[None], score, NEG_INF)
        _, sel = lax.top_k(score, n_sel)
        valid = sel <= qpos[None, :, None]
        k_sel = jax.vmap(lambda kk, ii: kk[ii])(k, sel)
        v_sel = jax.vmap(lambda vv, ii: vv[ii])(v, sel)
        logits = jnp.einsum('bqhd,bqkd->bqhk', qb.astype(jnp.float32), k_sel.astype(jnp.float32)) * sm_scale
        logits = jnp.where(valid[:, :, None, :], logits, NEG_INF)
        p = jax.nn.softmax(logits, axis=-1)
        o = jnp.einsum('bqhk,bqkd->bqhd', p, v_sel.astype(jnp.float32))
        return o.astype(q.dtype)

    out = lax.map(block, (q_b, iq_b, iw_b, qpos_b))
    return out.swapaxes(0, 1).reshape(B, L, H * dh)


def hierarchical_moe(h, w_group, b_group, w_router, b_router, w_gate, w_up, w_down):
    B, L, D = h.shape
    t = h.reshape(B * L, D)
    t32 = t.astype(jnp.float32)
    group_logits = t32 @ w_group.astype(jnp.float32) + b_group.astype(jnp.float32)
    group_prob = jax.nn.softmax(group_logits, axis=-1)
    g_sel = jnp.argmax(group_logits, axis=-1)
    p_group = jnp.take_along_axis(group_prob, g_sel[:, None], axis=-1)
    exp_logits = (t32 @ w_router.astype(jnp.float32) + b_router.astype(jnp.float32)).reshape(-1, N_GROUPS, EXPERTS_PER_GROUP)
    exp_logits = jnp.take_along_axis(exp_logits, g_sel[:, None, None], axis=1)[:, 0]
    top_vals, top_idx = lax.top_k(exp_logits, ROUTE_TOPK)
    top_w = jax.nn.softmax(top_vals, axis=-1) * p_group
    expert_id = g_sel[:, None] * EXPERTS_PER_GROUP + top_idx
    gates = jnp.sum(jax.nn.one_hot(expert_id, N_EXPERTS, dtype=jnp.float32) * top_w[..., None], axis=1)
    out = jnp.zeros((B * L, D), jnp.float32)
    for e in range(N_EXPERTS):
        hid = jax.nn.silu(t @ w_gate[e]) * (t @ w_up[e])
        out = out + gates[:, e:e + 1] * (hid @ w_down[e]).astype(jnp.float32)
    return out.reshape(B, L, D).astype(h.dtype)


def setup_inputs(seed: int = 0) -> dict:
    key = jax.random.key(seed)
    ks = jax.random.split(key, 24)
    f32 = jnp.float32
    D = D_MODEL

    def nrm(k, shape, scale):
        return jax.random.normal(k, shape, f32) * scale

    col_scale = jnp.concatenate([
        jnp.ones((POOL_WIDTH + ATTN_WIDTH + KV_DIM,), f32),
        jnp.full((KV_DIM,), DEEPNORM_BETA, f32),
        jnp.ones((IDX_HEADS * IDX_DIM + IDX_DIM + IDX_HEADS + N_BRANCHES * D,), f32)])
    return {
        'x': jax.random.normal(ks[0], (BATCH, SEQ, D), f32),
        'ln_in_g': 1.0 + nrm(ks[1], (D,), 0.01),
        'ln_in_b': nrm(ks[2], (D,), 0.01),
        'w_in': nrm(ks[3], (DEPTH, D, IN_WIDTH), D ** -0.5) * col_scale,
        'b_gate': nrm(ks[4], (DEPTH, N_BRANCHES * D), 0.1),
        'pool_w': nrm(ks[5], (DEPTH, POOL_GROUPS, POOL_GROUP_DIM, POOL_GROUP_DIM), POOL_GROUP_DIM ** -0.5),
        'pool_scale': 1.0 + nrm(ks[6], (DEPTH, POOL_WIDTH), 0.01),
        'w_proj_pool': nrm(ks[7], (DEPTH, POOL_WIDTH, D), POOL_WIDTH ** -0.5),
        'w_proj_attn': nrm(ks[8], (DEPTH, ATTN_WIDTH, D), ATTN_WIDTH ** -0.5),
        'w_out': nrm(ks[9], (DEPTH, D, D), D ** -0.5 * DEEPNORM_BETA),
        'ln1_g': 1.0 + nrm(ks[10], (DEPTH, D), 0.01),
        'ln1_b': nrm(ks[11], (DEPTH, D), 0.01),
        'w_group': nrm(ks[12], (DEPTH, D, N_GROUPS), D ** -0.5),
        'b_group': nrm(ks[13], (DEPTH, N_GROUPS), 0.01),
        'w_router': nrm(ks[14], (DEPTH, D, N_EXPERTS), D ** -0.5),
        'b_router': nrm(ks[15], (DEPTH, N_EXPERTS), 0.01),
        'w_gate': nrm(ks[16], (DEPTH, N_EXPERTS, D, EXPERT_FF), D ** -0.5),
        'w_up': nrm(ks[17], (DEPTH, N_EXPERTS, D, EXPERT_FF), D ** -0.5),
        'w_down': nrm(ks[18], (DEPTH, N_EXPERTS, EXPERT_FF, D), EXPERT_FF ** -0.5 * DEEPNORM_BETA),
        'ln2_g': 1.0 + nrm(ks[19], (DEPTH, D), 0.01),
        'ln2_b': nrm(ks[20], (DEPTH, D), 0.01),
    }


def reference(x, ln_in_g, ln_in_b, w_in, b_gate, pool_w, pool_scale, w_proj_pool, w_proj_attn, w_out,
              ln1_g, ln1_b, w_group, b_group, w_router, b_router, w_gate, w_up, w_down, ln2_g, ln2_b):
    B, L, D = x.shape
    offsets = [int(o) for o in np.cumsum(SPLITS)[:-1]]
    h = layer_norm(x, ln_in_g, ln_in_b)
    for l in range(DEPTH):
        proj = h @ w_in[l]
        u_pool, q, k, v, iq, ik, iw, gate_pre = jnp.split(proj, offsets, axis=-1)
        pool_out = causal_multiscale_pool(u_pool, pool_w[l], pool_scale[l])
        attn_out = dsa_attention(q.reshape(B, L, N_HEADS, HEAD_DIM), k, v,
                                 iq.reshape(B, L, IDX_HEADS, IDX_DIM), ik, iw)
        gates = jax.nn.sigmoid(gate_pre.astype(jnp.float32) + b_gate[l].astype(jnp.float32)).reshape(B, L, N_BRANCHES, D)
        branch_pool = (pool_out @ w_proj_pool[l]).astype(jnp.float32)
        branch_attn = (attn_out @ w_proj_attn[l]).astype(jnp.float32)
        merged = gates[:, :, 0] * branch_pool + gates[:, :, 1] * branch_attn
        mix = merged.astype(h.dtype) @ w_out[l]
        h = layer_norm(DEEPNORM_ALPHA * h + mix, ln1_g[l], ln1_b[l])
        ffn = hierarchical_moe(h, w_group[l], b_group[l], w_router[l], b_router[l], w_gate[l], w_up[l], w_down[l])
        h = layer_norm(DEEPNORM_ALPHA * h + ffn, ln2_g[l], ln2_b[l])
    return h.astype(x.dtype)
```

```python
import functools
import math

import jax
import jax.numpy as jnp
import numpy as np
from jax import lax
from jax.experimental import pallas as pl
from jax.experimental.pallas import tpu as pltpu

D_MODEL = 1024
POOL_WINDOWS = (2, 4, 8, 16)
POOL_GROUPS = 4
POOL_WIDTH = 512
POOL_GROUP_DIM = 128
N_HEADS = 8
HEAD_DIM = 64
ATTN_WIDTH = 512
KV_DIM = 64
IDX_HEADS = 8
IDX_DIM = 32
TOPK_MAX = 256
NEG_INF = -1e30
N_BRANCHES = 2
N_GROUPS = 4
EXPERTS_PER_GROUP = 8
N_EXPERTS = 32
EXPERT_FF = 256
LN_EPS = 1e-5
DEPTH = 1
DEEPNORM_ALPHA = (2.0 * DEPTH) ** 0.25

V7X_VMEM_LIMIT_BYTES = 56 * 1024 * 1024
LANES = 128

F32 = jnp.float32
BF16 = jnp.bfloat16
I32 = jnp.int32
INT_MIN = -2 ** 31
INT_MAX = 2 ** 31 - 1

TM_PROJ = 512
TQ = 256
TK = 256
TM_MERGE = 512
POOL_HALO = 16
TM_MOE = 512


def _layer_norm(x, g, b):
    mu = jnp.mean(x, axis=-1, keepdims=True)
    xc = x - mu
    var = jnp.mean(xc * xc, axis=-1, keepdims=True)
    return xc * lax.rsqrt(var + LN_EPS) * g + b


def _dot(a, b):
    return jnp.dot(a, b, preferred_element_type=F32)


def _dot_nt(a, b):
    return lax.dot_general(a, b, (((1,), (1,)), ((), ())), preferred_element_type=F32)


def _in_proj_kernel(x_ref, g_ref, b_ref, wu_ref, wk_ref, wik_ref, wt_ref,
                    u_ref, k_ref, ik_ref, qt_ref, vt_ref, iqt_ref, iwt_ref):
    h = _layer_norm(x_ref[...], g_ref[...], b_ref[...])
    hb = h.astype(BF16)
    u_ref[...] = _dot(hb, wu_ref[...])
    k_ref[...] = _dot(hb, wk_ref[...]).astype(BF16)
    ik_ref[...] = _dot(hb, wik_ref[...]).astype(BF16)
    pt = _dot_nt(wt_ref[...], hb)
    r0 = 0
    qt_ref[...] = pt[r0:r0 + ATTN_WIDTH].astype(BF16)
    r0 += ATTN_WIDTH
    vt_ref[...] = pt[r0:r0 + KV_DIM].astype(BF16)
    r0 += KV_DIM
    iqt_ref[...] = pt[r0:r0 + IDX_HEADS * IDX_DIM].astype(BF16)
    r0 += IDX_HEADS * IDX_DIM
    iwt_ref[...] = pt[r0:r0 + IDX_HEADS]


def _in_proj(x, ln_g, ln_b, wu, wk, wik, wt):
    B, L, D = x.shape
    tm = TM_PROJ
    grid = (B, L // tm)
    tok = lambda b, i: (b, i, 0)
    tokt = lambda b, i: (b, 0, i)
    const2 = lambda b, i: (0, 0)
    n_t = wt.shape[0]
    return pl.pallas_call(
        _in_proj_kernel,
        grid=grid,
        in_specs=[
            pl.BlockSpec((None, tm, D), tok),
            pl.BlockSpec((1, D), const2),
            pl.BlockSpec((1, D), const2),
            pl.BlockSpec((D, POOL_WIDTH), const2),
            pl.BlockSpec((D, KV_DIM), const2),
            pl.BlockSpec((D, IDX_DIM), const2),
            pl.BlockSpec((n_t, D), const2),
        ],
        out_specs=[
            pl.BlockSpec((None, tm, POOL_WIDTH), tok),
            pl.BlockSpec((None, tm, KV_DIM), tok),
            pl.BlockSpec((None, tm, IDX_DIM), tok),
            pl.BlockSpec((None, ATTN_WIDTH, tm), tokt),
            pl.BlockSpec((None, KV_DIM, tm), tokt),
            pl.BlockSpec((None, IDX_HEADS * IDX_DIM, tm), tokt),
            pl.BlockSpec((None, IDX_HEADS, tm), tokt),
        ],
        out_shape=[
            jax.ShapeDtypeStruct((B, L, POOL_WIDTH), F32),
            jax.ShapeDtypeStruct((B, L, KV_DIM), BF16),
            jax.ShapeDtypeStruct((B, L, IDX_DIM), BF16),
            jax.ShapeDtypeStruct((B, ATTN_WIDTH, L), BF16),
            jax.ShapeDtypeStruct((B, KV_DIM, L), BF16),
            jax.ShapeDtypeStruct((B, IDX_HEADS * IDX_DIM, L), BF16),
            jax.ShapeDtypeStruct((B, IDX_HEADS, L), F32),
        ],
        compiler_params=pltpu.CompilerParams(
            dimension_semantics=("arbitrary", "arbitrary"),
            vmem_limit_bytes=V7X_VMEM_LIMIT_BYTES),
        name="in_proj",
    )(x, ln_g, ln_b, wu, wk, wik, wt)


def _float_order_key(x):
    bits = pltpu.bitcast(x, I32)
    return bits ^ (lax.shift_right_arithmetic(bits, 31) & INT_MAX)


def _neg_key():
    bits = int(np.float32(NEG_INF).view(np.int32))
    return bits ^ ((bits >> 31) & INT_MAX)


def _dsa_kernel(iqt_ref, iwt_ref, ik_ref, qt_ref, k_ref, va_ref, o_ref,
                keys_ref, thr_ref, m_ref, acc_ref, *, seq_len, n_sel):
    qi = pl.program_id(1)
    q0 = qi * TQ
    nkc = (qi + 1) * (TQ // TK)
    n_beyond = seq_len - (qi + 1) * TQ
    neg_key = _neg_key()

    row_iota = lax.broadcasted_iota(I32, (TK, TQ), 0)
    t_idx = q0 + lax.broadcasted_iota(I32, (TK, TQ), 1)

    def score_chunk(c, carry):
        ikc = ik_ref[c]
        score = jnp.zeros((TK, TQ), F32)
        for h in range(IDX_HEADS):
            lg = _dot(ikc, iqt_ref[h * IDX_DIM:(h + 1) * IDX_DIM, :])
            score = score + iwt_ref[h:h + 1, :] * jnp.maximum(lg, 0.0)
        s_idx = c * TK + row_iota
        score = jnp.where(s_idx <= t_idx, score, NEG_INF)
        keys_ref[c] = _float_order_key(score)
        return carry

    lax.fori_loop(0, nkc, score_chunk, 0)

    def count(pred):
        def body(c, acc):
            hit = jnp.where(pred(keys_ref[c], c), 1, 0).astype(I32)
            return acc + jnp.sum(hit, axis=0, keepdims=True)
        return lax.fori_loop(0, nkc, body, jnp.zeros((1, TQ), I32))

    def radix_step(i, u):
        trial = u | lax.shift_left(jnp.int32(1), 31 - i)
        cand = trial ^ INT_MIN
        cnt = count(lambda kc, c: kc >= cand)
        cnt = cnt + jnp.where(cand <= neg_key, n_beyond, 0)
        return jnp.where(cnt >= n_sel, trial, u)

    u = lax.fori_loop(0, 32, radix_step, jnp.zeros((1, TQ), I32))
    kstar = u ^ INT_MIN

    c_gt = count(lambda kc, c: kc > kstar)
    c_ge = count(lambda kc, c: kc >= kstar)
    need = n_sel - (c_gt + jnp.where(kstar < neg_key, n_beyond, 0))
    general = jnp.logical_or(c_ge - c_gt != need, kstar <= neg_key)
    any_general = jnp.max(jnp.where(general, 1, 0)) > 0

    thr_ref[...] = kstar

    @pl.when(any_general)
    def _():
        def idx_step(i, p):
            trial = p | lax.shift_left(jnp.int32(1), 11 - i)
            cnt = count(lambda kc, c: jnp.logical_and(kc == kstar, c * TK + row_iota < trial))
            return jnp.where(cnt < need, trial, p)

        p = lax.fori_loop(0, 12, idx_step, jnp.zeros((1, TQ), I32))

        def rewrite(c, carry):
            kc = keys_ref[c]
            s_idx = c * TK + row_iota
            tie_ok = jnp.logical_and(kc == kstar, s_idx <= p)
            sel = jnp.logical_and(jnp.logical_or(kc > kstar, tie_ok), s_idx <= t_idx)
            keys_ref[c] = jnp.where(sel, INT_MAX, INT_MIN).astype(I32)
            return carry

        lax.fori_loop(0, nkc, rewrite, 0)
        thr_ref[...] = jnp.full((1, TQ), INT_MAX, I32)

    m_ref[...] = jnp.full(m_ref.shape, NEG_INF, F32)
    acc_ref[...] = jnp.zeros(acc_ref.shape, F32)
    thr = thr_ref[...]

    def attend_chunk(c, carry):
        kc = k_ref[c]
        vc = va_ref[c]
        sel = keys_ref[c] >= thr
        for h in range(N_HEADS):
            lg = _dot(kc, qt_ref[h * HEAD_DIM:(h + 1) * HEAD_DIM, :])
            lg = jnp.where(sel, lg, NEG_INF)
            m_old = m_ref[h:h + 1, :]
            m_new = jnp.maximum(m_old, jnp.max(lg, axis=0, keepdims=True))
            alpha = jnp.exp(m_old - m_new)
            p = jnp.exp(lg - m_new).astype(BF16)
            acc_ref[h] = alpha * acc_ref[h] + _dot(vc, p)
            m_ref[h:h + 1, :] = m_new
        return carry

    lax.fori_loop(0, nkc, attend_chunk, 0)

    outs = []
    for h in range(N_HEADS):
        a = acc_ref[h]
        outs.append(a[:HEAD_DIM] / a[HEAD_DIM:HEAD_DIM + 1])
    o_ref[...] = jnp.concatenate(outs, axis=0).T


def _dsa_attention(iqt, iwt, ik, qt, k, vaug):
    B, _, L = qt.shape
    n_sel = min(TOPK_MAX, L // 4)
    nch = L // TK
    va_rows = vaug.shape[2]
    grid = (B, L // TQ)
    kern = functools.partial(_dsa_kernel, seq_len=L, n_sel=n_sel)
    return pl.pallas_call(
        kern,
        grid=grid,
        in_specs=[
            pl.BlockSpec((None, IDX_HEADS * IDX_DIM, TQ), lambda b, i: (b, 0, i)),
            pl.BlockSpec((None, IDX_HEADS, TQ), lambda b, i: (b, 0, i)),
            pl.BlockSpec((None, nch, TK, IDX_DIM), lambda b, i: (b, 0, 0, 0)),
            pl.BlockSpec((None, ATTN_WIDTH, TQ), lambda b, i: (b, 0, i)),
            pl.BlockSpec((None, nch, TK, HEAD_DIM), lambda b, i: (b, 0, 0, 0)),
            pl.BlockSpec((None, nch, va_rows, TK), lambda b, i: (b, 0, 0, 0)),
        ],
        out_specs=pl.BlockSpec((None, TQ, ATTN_WIDTH), lambda b, i: (b, i, 0)),
        out_shape=jax.ShapeDtypeStruct((B, L, ATTN_WIDTH), F32),
        scratch_shapes=[
            pltpu.VMEM((nch, TK, TQ), I32),
            pltpu.VMEM((1, TQ), I32),
            pltpu.VMEM((N_HEADS, TQ), F32),
            pltpu.VMEM((N_HEADS, va_rows, TQ), F32),
        ],
        compiler_params=pltpu.CompilerParams(
            dimension_semantics=("arbitrary", "arbitrary"),
            vmem_limit_bytes=V7X_VMEM_LIMIT_BYTES),
        name="dsa_attn",
    )(iqt, iwt, ik, qt, k, vaug)


def _merge_kernel(x_ref, u_ref, uh_ref, a_ref, lng_ref, lnb_ref, wgate_ref, bgate_ref,
                  poolw_ref, pscale_ref, wpp_ref, wpa_ref, wout_ref, ln1g_ref, ln1b_ref,
                  wrh_ref, wrl_ref, br_ref, h1_ref, gates_ref, *, tiles_per_seq):
    tm = TM_MERGE
    i = pl.program_id(0)
    seq_start = (i % tiles_per_seq) == 0
    h = _layer_norm(x_ref[...], lng_ref[...], lnb_ref[...])
    hb = h.astype(BF16)

    halo = jnp.where(seq_start, 0.0, uh_ref[...])
    u = u_ref[...]
    ext = jnp.concatenate([halo, u], axis=0)
    pos = (i % tiles_per_seq) * tm + lax.broadcasted_iota(I32, (tm, 1), 0)
    mixed = []
    for g, w in enumerate(POOL_WINDOWS):
        s = ext[:, g * POOL_GROUP_DIM:(g + 1) * POOL_GROUP_DIM]
        span = 1
        while span < w:
            s = s + pltpu.roll(s, span, 0)
            span *= 2
        win = s[POOL_HALO:]
        cnt = jnp.minimum(pos + 1, w).astype(F32)
        ug = u[:, g * POOL_GROUP_DIM:(g + 1) * POOL_GROUP_DIM]
        delta = win / cnt - ug
        mixed.append(_dot(delta.astype(BF16), poolw_ref[g]))
    pool_out = jnp.concatenate(mixed, axis=1) * pscale_ref[...]

    gate_pre = _dot(hb, wgate_ref[...]) + bgate_ref[...]
    gates = jax.nn.sigmoid(gate_pre)
    bp = _dot(pool_out.astype(BF16), wpp_ref[...])
    ba = _dot(a_ref[...].astype(BF16), wpa_ref[...])
    merged = gates[:, :D_MODEL] * bp + gates[:, D_MODEL:] * ba
    mix = _dot(merged.astype(BF16), wout_ref[...])
    h1 = _layer_norm(DEEPNORM_ALPHA * h + mix, ln1g_ref[...], ln1b_ref[...])
    h1_ref[...] = h1

    hi = h1.astype(BF16)
    lo = (h1 - hi.astype(F32)).astype(BF16)
    lg = _dot(hi, wrh_ref[...]) + (_dot(lo, wrh_ref[...]) + _dot(hi, wrl_ref[...])) + br_ref[...]
    lane = lax.broadcasted_iota(I32, (tm, LANES), 1).astype(F32)
    big = jnp.float32(1 << 20)
    ninf = jnp.float32(-jnp.inf)
    is_g = jnp.logical_and(lane >= N_EXPERTS, lane < N_EXPERTS + N_GROUPS)
    gl = jnp.where(is_g, lg, ninf)
    gmax = jnp.max(gl, axis=1, keepdims=True)
    gsel = jnp.min(jnp.where(gl == gmax, lane, big), axis=1, keepdims=True) - N_EXPERTS
    sumexp = jnp.sum(jnp.where(is_g, jnp.exp(gl - gmax), 0.0), axis=1, keepdims=True)
    p_group = 1.0 / sumexp
    e_lo = gsel * EXPERTS_PER_GROUP
    in_grp = jnp.logical_and(lane >= e_lo, lane < e_lo + EXPERTS_PER_GROUP)
    el = jnp.where(in_grp, lg, ninf)
    m1 = jnp.max(el, axis=1, keepdims=True)
    i1 = jnp.min(jnp.where(el == m1, lane, big), axis=1, keepdims=True)
    el2 = jnp.where(lane == i1, ninf, el)
    m2 = jnp.max(el2, axis=1, keepdims=True)
    i2 = jnp.min(jnp.where(el2 == m2, lane, big), axis=1, keepdims=True)
    e2 = jnp.exp(m2 - m1)
    den = 1.0 + e2
    w1 = (1.0 / den) * p_group
    w2 = (e2 / den) * p_group
    dense = jnp.where(lane == i1, w1, 0.0) + jnp.where(lane == i2, w2, 0.0)
    gates_ref[...] = dense[:, :N_EXPERTS]


def _merge(x, u, attn, ln_g, ln_b, wgate, bgate, poolw, pscale, wpp, wpa, wout, ln1g, ln1b,
           wrh, wrl, br, seq_len):
    T, D = x.shape
    tm = TM_MERGE
    tiles_per_seq = seq_len // tm
    grid = (T // tm,)
    tok = lambda i: (i, 0)
    c2 = lambda i: (0, 0)
    c3 = lambda i: (0, 0, 0)
    halo_blocks = tm // POOL_HALO
    kern = functools.partial(_merge_kernel, tiles_per_seq=tiles_per_seq)
    return pl.pallas_call(
        kern,
        grid=grid,
        in_specs=[
            pl.BlockSpec((tm, D), tok),
            pl.BlockSpec((tm, POOL_WIDTH), tok),
            pl.BlockSpec((POOL_HALO, POOL_WIDTH),
                         lambda i: (jnp.maximum(i * halo_blocks - 1, 0), 0)),
            pl.BlockSpec((tm, ATTN_WIDTH), tok),
            pl.BlockSpec((1, D), c2),
            pl.BlockSpec((1, D), c2),
            pl.BlockSpec((D, N_BRANCHES * D), c2),
            pl.BlockSpec((1, N_BRANCHES * D), c2),
            pl.BlockSpec((POOL_GROUPS, POOL_GROUP_DIM, POOL_GROUP_DIM), c3),
            pl.BlockSpec((1, POOL_WIDTH), c2),
            pl.BlockSpec((POOL_WIDTH, D), c2),
            pl.BlockSpec((ATTN_WIDTH, D), c2),
            pl.BlockSpec((D, D), c2),
            pl.BlockSpec((1, D), c2),
            pl.BlockSpec((1, D), c2),
            pl.BlockSpec((D, LANES), c2),
            pl.BlockSpec((D, LANES), c2),
            pl.BlockSpec((1, LANES), c2),
        ],
        out_specs=[
            pl.BlockSpec((tm, D), tok),
            pl.BlockSpec((tm, N_EXPERTS), tok),
        ],
        out_shape=[
            jax.ShapeDtypeStruct((T, D), F32),
            jax.ShapeDtypeStruct((T, N_EXPERTS), F32),
        ],
        compiler_params=pltpu.CompilerParams(
            dimension_semantics=("arbitrary",),
            vmem_limit_bytes=V7X_VMEM_LIMIT_BYTES),
        name="merge",
    )(x, u, u, attn, ln_g, ln_b, wgate, bgate, poolw, pscale, wpp, wpa, wout, ln1g, ln1b,
      wrh, wrl, br)


def _moe_kernel(h1_ref, g_ref, wgu_ref, wd_ref, ln2g_ref, ln2b_ref, o_ref, acc_ref):
    g = pl.program_id(1)
    x = h1_ref[...]
    xb = x.astype(BF16)
    ff = EXPERTS_PER_GROUP * EXPERT_FF
    hc = _dot(xb, wgu_ref[...])
    hid = jax.nn.silu(hc[:, :ff]) * hc[:, ff:]
    gates = g_ref[...]
    parts = []
    for j in range(EXPERTS_PER_GROUP):
        parts.append(hid[:, j * EXPERT_FF:(j + 1) * EXPERT_FF] * gates[:, j:j + 1])
    y = _dot(jnp.concatenate(parts, axis=1).astype(BF16), wd_ref[...])

    @pl.when(g == 0)
    def _():
        acc_ref[...] = y

    @pl.when(g > 0)
    def _():
        acc_ref[...] = acc_ref[...] + y

    @pl.when(g == N_GROUPS - 1)
    def _():
        o_ref[...] = _layer_norm(DEEPNORM_ALPHA * x + acc_ref[...], ln2g_ref[...], ln2b_ref[...])


def _moe(h1, gates_g, wgu, wd, ln2g, ln2b):
    T, D = h1.shape
    tm = TM_MOE
    ff = EXPERTS_PER_GROUP * EXPERT_FF
    grid = (T // tm, N_GROUPS)
    return pl.pallas_call(
        _moe_kernel,
        grid=grid,
        in_specs=[
            pl.BlockSpec((tm, D), lambda i, g: (i, 0)),
            pl.BlockSpec((None, tm, EXPERTS_PER_GROUP), lambda i, g: (g, i, 0)),
            pl.BlockSpec((None, D, 2 * ff), lambda i, g: (g, 0, 0)),
            pl.BlockSpec((None, ff, D), lambda i, g: (g, 0, 0)),
            pl.BlockSpec((1, D), lambda i, g: (0, 0)),
            pl.BlockSpec((1, D), lambda i, g: (0, 0)),
        ],
        out_specs=pl.BlockSpec((tm, D), lambda i, g: (i, 0)),
        out_shape=jax.ShapeDtypeStruct((T, D), F32),
        scratch_shapes=[pltpu.VMEM((tm, D), F32)],
        compiler_params=pltpu.CompilerParams(
            dimension_semantics=("arbitrary", "arbitrary"),
            vmem_limit_bytes=V7X_VMEM_LIMIT_BYTES),
        name="moe",
    )(h1, gates_g, wgu, wd, ln2g, ln2b)


def kernel(x, ln_in_g, ln_in_b, w_in, b_gate, pool_w, pool_scale, w_proj_pool, w_proj_attn, w_out,
           ln1_g, ln1_b, w_group, b_group, w_router, b_router, w_gate, w_up, w_down, ln2_g, ln2_b):
    B, L, D = x.shape
    assert D == D_MODEL and w_in.shape[0] == DEPTH == 1
    assert L % TQ == 0 and L % TM_MERGE == 0 and L % TM_PROJ == 0
    assert L <= 4096
    T = B * L
    row = lambda v: v.reshape(1, -1).astype(F32)

    w = w_in[0]
    o = 0
    w_u = w[:, o:o + POOL_WIDTH]; o += POOL_WIDTH
    w_q = w[:, o:o + ATTN_WIDTH]; o += ATTN_WIDTH
    w_k = w[:, o:o + KV_DIM]; o += KV_DIM
    w_v = w[:, o:o + KV_DIM]; o += KV_DIM
    w_iq = w[:, o:o + IDX_HEADS * IDX_DIM]; o += IDX_HEADS * IDX_DIM
    w_ik = w[:, o:o + IDX_DIM]; o += IDX_DIM
    w_iw = w[:, o:o + IDX_HEADS]; o += IDX_HEADS
    w_g = w[:, o:]
    sm_scale = 1.0 / math.sqrt(HEAD_DIM)
    bf16_rows = 16
    w_t = jnp.concatenate([w_q * sm_scale, w_v, w_iq, w_iw,
                           jnp.zeros((D, bf16_rows - IDX_HEADS), F32)], axis=1).T.astype(BF16)

    u, k, ik, qt, vt, iqt, iwt = _in_proj(
        x, row(ln_in_g), row(ln_in_b), w_u.astype(BF16), w_k.astype(BF16), w_ik.astype(BF16), w_t)

    nch = L // TK
    ones_rows = 16
    vaug = jnp.concatenate([vt, jnp.ones((B, ones_rows, L), BF16)], axis=1)
    vaug = vaug.reshape(B, KV_DIM + ones_rows, nch, TK).transpose(0, 2, 1, 3)
    attn = _dsa_attention(iqt, iwt, ik.reshape(B, nch, TK, IDX_DIM), qt,
                          k.reshape(B, nch, TK, HEAD_DIM), vaug)

    w_r = jnp.zeros((D, LANES), F32)
    w_r = w_r.at[:, :N_EXPERTS].set(w_router[0]).at[:, N_EXPERTS:N_EXPERTS + N_GROUPS].set(w_group[0])
    b_r = jnp.zeros((1, LANES), F32)
    b_r = b_r.at[0, :N_EXPERTS].set(b_router[0]).at[0, N_EXPERTS:N_EXPERTS + N_GROUPS].set(b_group[0])
    w_rh = w_r.astype(BF16)
    w_rl = (w_r - w_rh.astype(F32)).astype(BF16)

    h1, gates = _merge(
        x.reshape(T, D), u.reshape(T, POOL_WIDTH), attn.reshape(T, ATTN_WIDTH),
        row(ln_in_g), row(ln_in_b), w_g.astype(BF16), row(b_gate[0]),
        pool_w[0].astype(BF16), row(pool_scale[0]), w_proj_pool[0].astype(BF16),
        w_proj_attn[0].astype(BF16), w_out[0].astype(BF16), row(ln1_g[0]), row(ln1_b[0]),
        w_rh, w_rl, b_r, L)

    ff = EXPERTS_PER_GROUP * EXPERT_FF
    wg = w_gate[0].reshape(N_GROUPS, EXPERTS_PER_GROUP, D, EXPERT_FF).transpose(0, 2, 1, 3)
    wu = w_up[0].reshape(N_GROUPS, EXPERTS_PER_GROUP, D, EXPERT_FF).transpose(0, 2, 1, 3)
    wgu = jnp.concatenate([wg.reshape(N_GROUPS, D, ff), wu.reshape(N_GROUPS, D, ff)], axis=2).astype(BF16)
    wd = w_down[0].reshape(N_GROUPS, ff, D).astype(BF16)
    gates_g = gates.reshape(T, N_GROUPS, EXPERTS_PER_GROUP).transpose(1, 0, 2)

    out = _moe(h1, gates_g, wgu, wd, row(ln2_g[0]), row(ln2_b[0]))
    return out.reshape(B, L, D)
```

```python
import functools
import math

import jax
import jax.numpy as jnp
import numpy as np
from jax import lax
from jax.experimental import pallas as pl
from jax.experimental.pallas import tpu as pltpu

D_MODEL = 1024
POOL_WINDOWS = (2, 4, 8, 16)
POOL_GROUPS = 4
POOL_WIDTH = 512
POOL_GROUP_DIM = 128
N_HEADS = 8
HEAD_DIM = 64
ATTN_WIDTH = 512
KV_DIM = 64
IDX_HEADS = 8
IDX_DIM = 32
TOPK_MAX = 256
NEG_INF = -1e30
N_BRANCHES = 2
N_GROUPS = 4
EXPERTS_PER_GROUP = 8
N_EXPERTS = 32
EXPERT_FF = 256
LN_EPS = 1e-5
DEPTH = 1
DEEPNORM_ALPHA = (2.0 * DEPTH) ** 0.25

V7X_VMEM_LIMIT_BYTES = 56 * 1024 * 1024
LANES = 128

F32 = jnp.float32
BF16 = jnp.bfloat16
I32 = jnp.int32
INT_MIN = -2 ** 31
INT_MAX = 2 ** 31 - 1

TM_PROJ = 512
TQ = 256
TK = 256
TM_MERGE = 512
POOL_HALO = 16
TM_MOE = 512


def _layer_norm(x, g, b):
    mu = jnp.mean(x, axis=-1, keepdims=True)
    xc = x - mu
    var = jnp.mean(xc * xc, axis=-1, keepdims=True)
    return xc * lax.rsqrt(var + LN_EPS) * g + b


def _dot(a, b):
    return jnp.dot(a, b, preferred_element_type=F32)


def _dot_nt(a, b):
    return lax.dot_general(a, b, (((1,), (1,)), ((), ())), preferred_element_type=F32)


def _in_proj_kernel(x_ref, g_ref, b_ref, wu_ref, wk_ref, wik_ref, wt_ref,
                    u_ref, k_ref, ik_ref, qt_ref, vt_ref, iqt_ref, iwt_ref):
    h = _layer_norm(x_ref[...], g_ref[...], b_ref[...])
    hb = h.astype(BF16)
    u_ref[...] = _dot(hb, wu_ref[...])
    k_ref[...] = _dot(hb, wk_ref[...]).astype(BF16)
    ik_ref[...] = _dot(hb, wik_ref[...]).astype(BF16)
    pt = _dot_nt(wt_ref[...], hb)
    r0 = 0
    for j in range(TM_PROJ // TQ):
        qt_ref[j] = pt[r0:r0 + ATTN_WIDTH, j * TQ:(j + 1) * TQ].astype(BF16)
    r0 += ATTN_WIDTH
    vt_ref[...] = pt[r0:r0 + KV_DIM].astype(BF16)
    r0 += KV_DIM
    iqt_ref[...] = pt[r0:r0 + IDX_HEADS * IDX_DIM].astype(BF16)
    r0 += IDX_HEADS * IDX_DIM
    iwt_ref[...] = pt[r0:r0 + IDX_HEADS]


def _in_proj(x, ln_g, ln_b, wu, wk, wik, wt):
    B, L, D = x.shape
    tm = TM_PROJ
    grid = (B, L // tm)
    tok = lambda b, i: (b, i, 0)
    tokt = lambda b, i: (b, 0, i)
    const2 = lambda b, i: (0, 0)
    n_t = wt.shape[0]
    return pl.pallas_call(
        _in_proj_kernel,
        grid=grid,
        in_specs=[
            pl.BlockSpec((None, tm, D), tok),
            pl.BlockSpec((1, D), const2),
            pl.BlockSpec((1, D), const2),
            pl.BlockSpec((D, POOL_WIDTH), const2),
            pl.BlockSpec((D, KV_DIM), const2),
            pl.BlockSpec((D, IDX_DIM), const2),
            pl.BlockSpec((n_t, D), const2),
        ],
        out_specs=[
            pl.BlockSpec((None, tm, POOL_WIDTH), tok),
            pl.BlockSpec((None, tm, KV_DIM), tok),
            pl.BlockSpec((None, tm, IDX_DIM), tok),
            pl.BlockSpec((None, tm // TQ, ATTN_WIDTH, TQ), lambda b, i: (b, i, 0, 0)),
            pl.BlockSpec((None, KV_DIM, tm), tokt),
            pl.BlockSpec((None, IDX_HEADS * IDX_DIM, tm), tokt),
            pl.BlockSpec((None, IDX_HEADS, tm), tokt),
        ],
        out_shape=[
            jax.ShapeDtypeStruct((B, L, POOL_WIDTH), F32),
            jax.ShapeDtypeStruct((B, L, KV_DIM), BF16),
            jax.ShapeDtypeStruct((B, L, IDX_DIM), BF16),
            jax.ShapeDtypeStruct((B, L // TQ, ATTN_WIDTH, TQ), BF16),
            jax.ShapeDtypeStruct((B, KV_DIM, L), BF16),
            jax.ShapeDtypeStruct((B, IDX_HEADS * IDX_DIM, L), BF16),
            jax.ShapeDtypeStruct((B, IDX_HEADS, L), F32),
        ],
        compiler_params=pltpu.CompilerParams(
            dimension_semantics=("arbitrary", "arbitrary"),
            vmem_limit_bytes=V7X_VMEM_LIMIT_BYTES),
        name="in_proj",
    )(x, ln_g, ln_b, wu, wk, wik, wt)


def _ordered_bits_to_f32(u):
    key = u ^ INT_MIN
    bits = key ^ (lax.shift_right_arithmetic(key, 31) & INT_MAX)
    return pltpu.bitcast(bits, F32)


MAX_SEARCH_PASSES = 24


def _dsa_kernel(iqt_ref, iwt_ref, ik_ref, qw_ref, k_ref, va_ref, o_ref,
                sc_ref, thr_ref, m_ref, acc_ref, *, seq_len, n_sel):
    qi = pl.program_id(1)
    q0 = qi * TQ
    nkc = (qi + 1) * (TQ // TK)
    n_beyond = seq_len - (qi + 1) * TQ
    k_sel = jnp.float32(n_sel)

    row_iota = lax.broadcasted_iota(I32, (TK, TQ), 0)
    t_idx = q0 + lax.broadcasted_iota(I32, (TK, TQ), 1)

    def score_chunk(c, carry):
        smin, smax = carry
        ikc = ik_ref[c]
        score = jnp.zeros((TK, TQ), F32)
        for h in range(IDX_HEADS):
            lg = _dot(ikc, iqt_ref[h * IDX_DIM:(h + 1) * IDX_DIM, :])
            score = score + iwt_ref[h:h + 1, :] * jnp.maximum(lg, 0.0)
        smin = jnp.minimum(smin, jnp.min(score, axis=0, keepdims=True))
        smax = jnp.maximum(smax, jnp.max(score, axis=0, keepdims=True))
        s_idx = c * TK + row_iota
        sc_ref[c] = jnp.where(s_idx <= t_idx, score, NEG_INF)
        return smin, smax

    big = jnp.float32(3e38)
    smin, smax = lax.fori_loop(0, nkc, score_chunk,
                               (jnp.full((1, TQ), big, F32), jnp.full((1, TQ), -big, F32)))

    def count(pred):
        def body(c, acc):
            hit = jnp.where(pred(sc_ref[c], c), 1.0, 0.0)
            return acc + jnp.sum(hit, axis=0, keepdims=True)
        return lax.fori_loop(0, nkc, body, jnp.zeros((1, TQ), F32))

    def search_cond(st):
        return jnp.logical_and(st[0] < MAX_SEARCH_PASSES, st[1] > 0.5)

    def search_body(st):
        it, _, lo, hi, clo, chi = st
        interp = (clo - (k_sel - 0.5)) / (clo - chi)
        frac = jnp.where((it & 1) == 0, 0.5, jnp.clip(interp, 1.0 / 64, 63.0 / 64))
        cand = jnp.where(clo == k_sel, lo, lo + (hi - lo) * frac)
        cnt = count(lambda s, c: s >= cand)
        ge = cnt >= k_sel
        lo = jnp.where(ge, cand, lo)
        clo = jnp.where(ge, cnt, clo)
        hi = jnp.where(ge, hi, cand)
        chi = jnp.where(ge, chi, cnt)
        pending = jnp.sum(jnp.where(clo == k_sel, 0.0, 1.0))
        return it + 1, pending, lo, hi, clo, chi

    n_adm = (q0 + 1 + lax.broadcasted_iota(I32, (1, TQ), 1)).astype(F32)
    st = lax.while_loop(search_cond, search_body,
                        (jnp.int32(0), jnp.float32(TQ), smin, smax, n_adm, jnp.ones((1, TQ), F32)))
    lo, clo = st[2], st[4]
    settled = jnp.logical_and(clo == k_sel, lo > NEG_INF)
    thr_ref[...] = lo

    @pl.when(jnp.sum(jnp.where(settled, 0.0, 1.0)) > 0.5)
    def _():
        def radix_step(i, u):
            trial = u | lax.shift_left(jnp.int32(1), 31 - i)
            cand = _ordered_bits_to_f32(trial)
            cnt = count(lambda s, c: s >= cand)
            cnt = cnt + jnp.where(cand <= NEG_INF, n_beyond, 0).astype(F32)
            return jnp.where(cnt >= k_sel, trial, u)

        u = lax.fori_loop(0, 32, radix_step, jnp.zeros((1, TQ), I32))
        kstar = _ordered_bits_to_f32(u)
        c_gt = count(lambda s, c: s > kstar)
        need = k_sel - (c_gt + jnp.where(kstar < NEG_INF, n_beyond, 0).astype(F32))

        def idx_step(i, p):
            trial = p | lax.shift_left(jnp.int32(1), 11 - i)
            cnt = count(lambda s, c: jnp.logical_and(s == kstar, c * TK + row_iota < trial))
            return jnp.where(cnt < need, trial, p)

        p = lax.fori_loop(0, 12, idx_step, jnp.zeros((1, TQ), I32))

        def rewrite(c, carry):
            s = sc_ref[c]
            s_idx = c * TK + row_iota
            tie_ok = jnp.logical_and(s == kstar, s_idx <= p)
            sel = jnp.logical_and(jnp.logical_or(s > kstar, tie_ok), s_idx <= t_idx)
            sc_ref[c] = jnp.where(sel, 1.0, -1.0)
            return carry

        lax.fori_loop(0, nkc, rewrite, 0)
        thr_ref[...] = jnp.zeros((1, TQ), F32)

    m_ref[...] = jnp.full(m_ref.shape, NEG_INF, F32)
    acc_ref[...] = jnp.zeros(acc_ref.shape, F32)
    thr = thr_ref[...]

    def attend_chunk(c, carry):
        vc = va_ref[c]
        bias = jnp.where(sc_ref[c] >= thr, 0.0, NEG_INF)
        lg = _dot(k_ref[c], qw_ref[...])
        lg = jnp.concatenate(
            [lg[:, h * TQ:(h + 1) * TQ] + bias for h in range(N_HEADS)], axis=1)
        m_old = m_ref[...]
        m_new = jnp.maximum(m_old, jnp.max(lg, axis=0, keepdims=True))
        alpha = jnp.exp2(m_old - m_new)
        p = jnp.exp2(lg - m_new).astype(BF16)
        acc_ref[...] = alpha * acc_ref[...] + _dot(vc, p)
        m_ref[...] = m_new
        return carry

    lax.fori_loop(0, nkc, attend_chunk, 0)

    a = acc_ref[...]
    o = a[:HEAD_DIM] / a[HEAD_DIM:HEAD_DIM + 1]
    o = jnp.concatenate([o[:, h * TQ:(h + 1) * TQ] for h in range(N_HEADS)], axis=0)
    o_ref[...] = o.T


def _dsa_attention(iqt, iwt, ik, qw, k, vaug):
    B, _, L = iqt.shape
    n_sel = min(TOPK_MAX, L // 4)
    nch = L // TK
    va_rows = vaug.shape[2]
    grid = (B, L // TQ)
    kern = functools.partial(_dsa_kernel, seq_len=L, n_sel=n_sel)
    return pl.pallas_call(
        kern,
        grid=grid,
        in_specs=[
            pl.BlockSpec((None, IDX_HEADS * IDX_DIM, TQ), lambda b, i: (b, 0, i)),
            pl.BlockSpec((None, IDX_HEADS, TQ), lambda b, i: (b, 0, i)),
            pl.BlockSpec((None, nch, TK, IDX_DIM), lambda b, i: (b, 0, 0, 0)),
            pl.BlockSpec((None, None, HEAD_DIM, N_HEADS * TQ), lambda b, i: (b, i, 0, 0)),
            pl.BlockSpec((None, nch, TK, HEAD_DIM), lambda b, i: (b, 0, 0, 0)),
            pl.BlockSpec((None, nch, va_rows, TK), lambda b, i: (b, 0, 0, 0)),
        ],
        out_specs=pl.BlockSpec((None, TQ, ATTN_WIDTH), lambda b, i: (b, i, 0)),
        out_shape=jax.ShapeDtypeStruct((B, L, ATTN_WIDTH), F32),
        scratch_shapes=[
            pltpu.VMEM((nch, TK, TQ), F32),
            pltpu.VMEM((1, TQ), F32),
            pltpu.VMEM((1, N_HEADS * TQ), F32),
            pltpu.VMEM((va_rows, N_HEADS * TQ), F32),
        ],
        compiler_params=pltpu.CompilerParams(
            dimension_semantics=("arbitrary", "arbitrary"),
            vmem_limit_bytes=V7X_VMEM_LIMIT_BYTES),
        name="dsa_attn",
    )(iqt, iwt, ik, qw, k, vaug)


def _merge_kernel(x_ref, u_ref, uh_ref, a_ref, lng_ref, lnb_ref, wgate_ref, bgate_ref,
                  poolw_ref, pscale_ref, wpp_ref, wpa_ref, wout_ref, ln1g_ref, ln1b_ref,
                  wrh_ref, wrl_ref, br_ref, h1_ref, gates_ref, *, tiles_per_seq):
    tm = TM_MERGE
    i = pl.program_id(0)
    seq_start = (i % tiles_per_seq) == 0
    h = _layer_norm(x_ref[...], lng_ref[...], lnb_ref[...])
    hb = h.astype(BF16)

    halo = jnp.where(seq_start, 0.0, uh_ref[...])
    u = u_ref[...]
    ext = jnp.concatenate([halo, u], axis=0)
    pos = (i % tiles_per_seq) * tm + lax.broadcasted_iota(I32, (tm, 1), 0)
    mixed = []
    for g, w in enumerate(POOL_WINDOWS):
        s = ext[:, g * POOL_GROUP_DIM:(g + 1) * POOL_GROUP_DIM]
        span = 1
        while span < w:
            s = s + pltpu.roll(s, span, 0)
            span *= 2
        win = s[POOL_HALO:]
        cnt = jnp.minimum(pos + 1, w).astype(F32)
        ug = u[:, g * POOL_GROUP_DIM:(g + 1) * POOL_GROUP_DIM]
        delta = win / cnt - ug
        mixed.append(_dot(delta.astype(BF16), poolw_ref[g]))
    pool_out = jnp.concatenate(mixed, axis=1) * pscale_ref[...]

    gate_pre = _dot(hb, wgate_ref[...]) + bgate_ref[...]
    gates = jax.nn.sigmoid(gate_pre)
    bp = _dot(pool_out.astype(BF16), wpp_ref[...])
    ba = _dot(a_ref[...].astype(BF16), wpa_ref[...])
    merged = gates[:, :D_MODEL] * bp + gates[:, D_MODEL:] * ba
    mix = _dot(merged.astype(BF16), wout_ref[...])
    h1 = _layer_norm(DEEPNORM_ALPHA * h + mix, ln1g_ref[...], ln1b_ref[...])
    h1_ref[...] = h1

    hi = h1.astype(BF16)
    lo = (h1 - hi.astype(F32)).astype(BF16)
    lg = _dot(hi, wrh_ref[...]) + (_dot(lo, wrh_ref[...]) + _dot(hi, wrl_ref[...])) + br_ref[...]
    lane = lax.broadcasted_iota(I32, (tm, LANES), 1).astype(F32)
    big = jnp.float32(1 << 20)
    ninf = jnp.float32(-jnp.inf)
    is_g = jnp.logical_and(lane >= N_EXPERTS, lane < N_EXPERTS + N_GROUPS)
    gl = jnp.where(is_g, lg, ninf)
    gmax = jnp.max(gl, axis=1, keepdims=True)
    gsel = jnp.min(jnp.where(gl == gmax, lane, big), axis=1, keepdims=True) - N_EXPERTS
    sumexp = jnp.sum(jnp.where(is_g, jnp.exp(gl - gmax), 0.0), axis=1, keepdims=True)
    p_group = 1.0 / sumexp
    e_lo = gsel * EXPERTS_PER_GROUP
    in_grp = jnp.logical_and(lane >= e_lo, lane < e_lo + EXPERTS_PER_GROUP)
    el = jnp.where(in_grp, lg, ninf)
    m1 = jnp.max(el, axis=1, keepdims=True)
    i1 = jnp.min(jnp.where(el == m1, lane, big), axis=1, keepdims=True)
    el2 = jnp.where(lane == i1, ninf, el)
    m2 = jnp.max(el2, axis=1, keepdims=True)
    i2 = jnp.min(jnp.where(el2 == m2, lane, big), axis=1, keepdims=True)
    e2 = jnp.exp(m2 - m1)
    den = 1.0 + e2
    w1 = (1.0 / den) * p_group
    w2 = (e2 / den) * p_group
    dense = jnp.where(lane == i1, w1, 0.0) + jnp.where(lane == i2, w2, 0.0)
    gates_ref[...] = dense[:, :N_EXPERTS]


def _merge(x, u, attn, ln_g, ln_b, wgate, bgate, poolw, pscale, wpp, wpa, wout, ln1g, ln1b,
           wrh, wrl, br, seq_len):
    T, D = x.shape
    tm = TM_MERGE
    tiles_per_seq = seq_len // tm
    grid = (T // tm,)
    tok = lambda i: (i, 0)
    c2 = lambda i: (0, 0)
    c3 = lambda i: (0, 0, 0)
    halo_blocks = tm // POOL_HALO
    kern = functools.partial(_merge_kernel, tiles_per_seq=tiles_per_seq)
    return pl.pallas_call(
        kern,
        grid=grid,
        in_specs=[
            pl.BlockSpec((tm, D), tok),
            pl.BlockSpec((tm, POOL_WIDTH), tok),
            pl.BlockSpec((POOL_HALO, POOL_WIDTH),
                         lambda i: (jnp.maximum(i * halo_blocks - 1, 0), 0)),
            pl.BlockSpec((tm, ATTN_WIDTH), tok),
            pl.BlockSpec((1, D), c2),
            pl.BlockSpec((1, D), c2),
            pl.BlockSpec((D, N_BRANCHES * D), c2),
            pl.BlockSpec((1, N_BRANCHES * D), c2),
            pl.BlockSpec((POOL_GROUPS, POOL_GROUP_DIM, POOL_GROUP_DIM), c3),
            pl.BlockSpec((1, POOL_WIDTH), c2),
            pl.BlockSpec((POOL_WIDTH, D), c2),
            pl.BlockSpec((ATTN_WIDTH, D), c2),
            pl.BlockSpec((D, D), c2),
            pl.BlockSpec((1, D), c2),
            pl.BlockSpec((1, D), c2),
            pl.BlockSpec((D, LANES), c2),
            pl.BlockSpec((D, LANES), c2),
            pl.BlockSpec((1, LANES), c2),
        ],
        out_specs=[
            pl.BlockSpec((tm, D), tok),
            pl.BlockSpec((tm, N_EXPERTS), tok),
        ],
        out_shape=[
            jax.ShapeDtypeStruct((T, D), F32),
            jax.ShapeDtypeStruct((T, N_EXPERTS), F32),
        ],
        compiler_params=pltpu.CompilerParams(
            dimension_semantics=("arbitrary",),
            vmem_limit_bytes=V7X_VMEM_LIMIT_BYTES),
        name="merge",
    )(x, u, u, attn, ln_g, ln_b, wgate, bgate, poolw, pscale, wpp, wpa, wout, ln1g, ln1b,
      wrh, wrl, br)


def _moe_kernel(h1_ref, g_ref, wgu_ref, wd_ref, ln2g_ref, ln2b_ref, o_ref, acc_ref):
    g = pl.program_id(1)
    x = h1_ref[...]
    xb = x.astype(BF16)
    ff = EXPERTS_PER_GROUP * EXPERT_FF
    hc = _dot(xb, wgu_ref[...])
    hid = jax.nn.silu(hc[:, :ff]) * hc[:, ff:]
    gates = g_ref[...]
    parts = []
    for j in range(EXPERTS_PER_GROUP):
        parts.append(hid[:, j * EXPERT_FF:(j + 1) * EXPERT_FF] * gates[:, j:j + 1])
    y = _dot(jnp.concatenate(parts, axis=1).astype(BF16), wd_ref[...])

    @pl.when(g == 0)
    def _():
        acc_ref[...] = y

    @pl.when(g > 0)
    def _():
        acc_ref[...] = acc_ref[...] + y

    @pl.when(g == N_GROUPS - 1)
    def _():
        o_ref[...] = _layer_norm(DEEPNORM_ALPHA * x + acc_ref[...], ln2g_ref[...], ln2b_ref[...])


def _moe(h1, gates_g, wgu, wd, ln2g, ln2b):
    T, D = h1.shape
    tm = TM_MOE
    ff = EXPERTS_PER_GROUP * EXPERT_FF
    grid = (T // tm, N_GROUPS)
    return pl.pallas_call(
        _moe_kernel,
        grid=grid,
        in_specs=[
            pl.BlockSpec((tm, D), lambda i, g: (i, 0)),
            pl.BlockSpec((None, tm, EXPERTS_PER_GROUP), lambda i, g: (g, i, 0)),
            pl.BlockSpec((None, D, 2 * ff), lambda i, g: (g, 0, 0)),
            pl.BlockSpec((None, ff, D), lambda i, g: (g, 0, 0)),
            pl.BlockSpec((1, D), lambda i, g: (0, 0)),
            pl.BlockSpec((1, D), lambda i, g: (0, 0)),
        ],
        out_specs=pl.BlockSpec((tm, D), lambda i, g: (i, 0)),
        out_shape=jax.ShapeDtypeStruct((T, D), F32),
        scratch_shapes=[pltpu.VMEM((tm, D), F32)],
        compiler_params=pltpu.CompilerParams(
            dimension_semantics=("arbitrary", "arbitrary"),
            vmem_limit_bytes=V7X_VMEM_LIMIT_BYTES),
        name="moe",
    )(h1, gates_g, wgu, wd, ln2g, ln2b)


def kernel(x, ln_in_g, ln_in_b, w_in, b_gate, pool_w, pool_scale, w_proj_pool, w_proj_attn, w_out,
           ln1_g, ln1_b, w_group, b_group, w_router, b_router, w_gate, w_up, w_down, ln2_g, ln2_b):
    B, L, D = x.shape
    assert D == D_MODEL and w_in.shape[0] == DEPTH == 1
    assert L % TQ == 0 and L % TM_MERGE == 0 and L % TM_PROJ == 0
    assert L <= 4096
    T = B * L
    row = lambda v: v.reshape(1, -1).astype(F32)

    w = w_in[0]
    o = 0
    w_u = w[:, o:o + POOL_WIDTH]; o += POOL_WIDTH
    w_q = w[:, o:o + ATTN_WIDTH]; o += ATTN_WIDTH
    w_k = w[:, o:o + KV_DIM]; o += KV_DIM
    w_v = w[:, o:o + KV_DIM]; o += KV_DIM
    w_iq = w[:, o:o + IDX_HEADS * IDX_DIM]; o += IDX_HEADS * IDX_DIM
    w_ik = w[:, o:o + IDX_DIM]; o += IDX_DIM
    w_iw = w[:, o:o + IDX_HEADS]; o += IDX_HEADS
    w_g = w[:, o:]
    sm_scale = math.log2(math.e) / math.sqrt(HEAD_DIM)
    bf16_rows = 16
    w_qd = (w_q * sm_scale).reshape(D, N_HEADS, HEAD_DIM).transpose(0, 2, 1).reshape(D, ATTN_WIDTH)
    w_t = jnp.concatenate([w_qd, w_v, w_iq, w_iw,
                           jnp.zeros((D, bf16_rows - IDX_HEADS), F32)], axis=1).T.astype(BF16)

    u, k, ik, qt, vt, iqt, iwt = _in_proj(
        x, row(ln_in_g), row(ln_in_b), w_u.astype(BF16), w_k.astype(BF16), w_ik.astype(BF16), w_t)

    nch = L // TK
    ones_rows = 16
    vaug = jnp.concatenate([vt, jnp.ones((B, ones_rows, L), BF16)], axis=1)
    vaug = vaug.reshape(B, KV_DIM + ones_rows, nch, TK).transpose(0, 2, 1, 3)
    qw = qt.reshape(B, L // TQ, HEAD_DIM, N_HEADS * TQ)
    attn = _dsa_attention(iqt, iwt, ik.reshape(B, nch, TK, IDX_DIM), qw,
                          k.reshape(B, nch, TK, HEAD_DIM), vaug)

    w_r = jnp.zeros((D, LANES), F32)
    w_r = w_r.at[:, :N_EXPERTS].set(w_router[0]).at[:, N_EXPERTS:N_EXPERTS + N_GROUPS].set(w_group[0])
    b_r = jnp.zeros((1, LANES), F32)
    b_r = b_r.at[0, :N_EXPERTS].set(b_router[0]).at[0, N_EXPERTS:N_EXPERTS + N_GROUPS].set(b_group[0])
    w_rh = w_r.astype(BF16)
    w_rl = (w_r - w_rh.astype(F32)).astype(BF16)

    h1, gates = _merge(
        x.reshape(T, D), u.reshape(T, POOL_WIDTH), attn.reshape(T, ATTN_WIDTH),
        row(ln_in_g), row(ln_in_b), w_g.astype(BF16), row(b_gate[0]),
        pool_w[0].astype(BF16), row(pool_scale[0]), w_proj_pool[0].astype(BF16),
        w_proj_attn[0].astype(BF16), w_out[0].astype(BF16), row(ln1_g[0]), row(ln1_b[0]),
        w_rh, w_rl, b_r, L)

    ff = EXPERTS_PER_GROUP * EXPERT_FF
    wg = w_gate[0].reshape(N_GROUPS, EXPERTS_PER_GROUP, D, EXPERT_FF).transpose(0, 2, 1, 3)
    wu = w_up[0].reshape(N_GROUPS, EXPERTS_PER_GROUP, D, EXPERT_FF).transpose(0, 2, 1, 3)
    wgu = jnp.concatenate([wg.reshape(N_GROUPS, D, ff), wu.reshape(N_GROUPS, D, ff)], axis=2).astype(BF16)
    wd = w_down[0].reshape(N_GROUPS, ff, D).astype(BF16)
    gates_g = gates.reshape(T, N_GROUPS, EXPERTS_PER_GROUP).transpose(1, 0, 2)

    out = _moe(h1, gates_g, wgu, wd, row(ln2_g[0]), row(ln2_b[0]))
    return out.reshape(B, L, D)
```

```python
import functools
import math

import jax
import jax.numpy as jnp
import numpy as np
from jax import lax
from jax.experimental import pallas as pl
from jax.experimental.pallas import tpu as pltpu

D_MODEL = 1024
POOL_WINDOWS = (2, 4, 8, 16)
POOL_GROUPS = 4
POOL_WIDTH = 512
POOL_GROUP_DIM = 128
N_HEADS = 8
HEAD_DIM = 64
ATTN_WIDTH = 512
KV_DIM = 64
IDX_HEADS = 8
IDX_DIM = 32
TOPK_MAX = 256
NEG_INF = float(np.float32(-1e30))
N_BRANCHES = 2
N_GROUPS = 4
EXPERTS_PER_GROUP = 8
N_EXPERTS = 32
EXPERT_FF = 256
LN_EPS = 1e-5
DEPTH = 1
DEEPNORM_ALPHA = (2.0 * DEPTH) ** 0.25

V7X_VMEM_LIMIT_BYTES = 56 * 1024 * 1024
LANES = 128

F32 = jnp.float32
BF16 = jnp.bfloat16
I32 = jnp.int32
INT_MIN = -2 ** 31
INT_MAX = 2 ** 31 - 1

TM_PROJ = 512
TQ = 256
TK = 128
TM_MERGE = 512
POOL_HALO = 16
TM_MOE = 512


def _layer_norm(x, g, b):
    mu = jnp.mean(x, axis=-1, keepdims=True)
    xc = x - mu
    var = jnp.mean(xc * xc, axis=-1, keepdims=True)
    return xc * lax.rsqrt(var + LN_EPS) * g + b


def _dot(a, b):
    return jnp.dot(a, b, preferred_element_type=F32)


def _dot_nt(a, b):
    return lax.dot_general(a, b, (((1,), (1,)), ((), ())), preferred_element_type=F32)


def _in_proj_kernel(x_ref, g_ref, b_ref, wu_ref, wk_ref, wik_ref, wt_ref,
                    u_ref, k_ref, ik_ref, qt_ref, vt_ref, iqt_ref, iwt_ref):
    h = _layer_norm(x_ref[...], g_ref[...], b_ref[...])
    hb = h.astype(BF16)
    u_ref[...] = _dot(hb, wu_ref[...])
    k_ref[...] = _dot(hb, wk_ref[...]).astype(BF16)
    ik_ref[...] = _dot(hb, wik_ref[...]).astype(BF16)
    pt = _dot_nt(wt_ref[...], hb)
    r0 = 0
    for j in range(TM_PROJ // TQ):
        qt_ref[j] = pt[r0:r0 + ATTN_WIDTH, j * TQ:(j + 1) * TQ].astype(BF16)
    r0 += ATTN_WIDTH
    vt_ref[...] = pt[r0:r0 + KV_DIM].astype(BF16)
    r0 += KV_DIM
    iqt_ref[...] = pt[r0:r0 + IDX_HEADS * IDX_DIM].astype(BF16)
    r0 += IDX_HEADS * IDX_DIM
    iwt_ref[...] = pt[r0:r0 + IDX_HEADS]


def _in_proj(x, ln_g, ln_b, wu, wk, wik, wt):
    B, L, D = x.shape
    tm = TM_PROJ
    grid = (B, L // tm)
    tok = lambda b, i: (b, i, 0)
    tokt = lambda b, i: (b, 0, i)
    const2 = lambda b, i: (0, 0)
    n_t = wt.shape[0]
    return pl.pallas_call(
        _in_proj_kernel,
        grid=grid,
        in_specs=[
            pl.BlockSpec((None, tm, D), tok),
            pl.BlockSpec((1, D), const2),
            pl.BlockSpec((1, D), const2),
            pl.BlockSpec((D, POOL_WIDTH), const2),
            pl.BlockSpec((D, KV_DIM), const2),
            pl.BlockSpec((D, IDX_DIM), const2),
            pl.BlockSpec((n_t, D), const2),
        ],
        out_specs=[
            pl.BlockSpec((None, tm, POOL_WIDTH), tok),
            pl.BlockSpec((None, tm, KV_DIM), tok),
            pl.BlockSpec((None, tm, IDX_DIM), tok),
            pl.BlockSpec((None, tm // TQ, ATTN_WIDTH, TQ), lambda b, i: (b, i, 0, 0)),
            pl.BlockSpec((None, KV_DIM, tm), tokt),
            pl.BlockSpec((None, IDX_HEADS * IDX_DIM, tm), tokt),
            pl.BlockSpec((None, IDX_HEADS, tm), tokt),
        ],
        out_shape=[
            jax.ShapeDtypeStruct((B, L, POOL_WIDTH), F32),
            jax.ShapeDtypeStruct((B, L, KV_DIM), BF16),
            jax.ShapeDtypeStruct((B, L, IDX_DIM), BF16),
            jax.ShapeDtypeStruct((B, L // TQ, ATTN_WIDTH, TQ), BF16),
            jax.ShapeDtypeStruct((B, KV_DIM, L), BF16),
            jax.ShapeDtypeStruct((B, IDX_HEADS * IDX_DIM, L), BF16),
            jax.ShapeDtypeStruct((B, IDX_HEADS, L), F32),
        ],
        compiler_params=pltpu.CompilerParams(
            dimension_semantics=("arbitrary", "arbitrary"),
            vmem_limit_bytes=V7X_VMEM_LIMIT_BYTES),
        name="in_proj",
    )(x, ln_g, ln_b, wu, wk, wik, wt)


BRACKET_PASSES = 14
NO_TIE = 1e9


def _rows_to_sublanes(x, op):
    return op(x.reshape(x.shape[0] // 8, 8, TQ), axis=0)


SCAN_CHUNKS = 2
SCAN_ROWS = SCAN_CHUNKS * TK


def _dsa_kernel(iqt_ref, iwt_ref, ik_ref, qw_ref, k_ref, va_ref, o_ref,
                sc_ref, thr_ref, tie_ref, m_ref, acc_ref, lga_ref, lgb_ref, *, seq_len, n_sel):
    assert (TQ // TK) % 2 == 0
    qi = pl.program_id(1)
    q0 = qi * TQ
    nkc = (qi + 1) * (TQ // TK)
    n_beyond = seq_len - (qi + 1) * TQ
    k_sel = jnp.float32(n_sel)

    row_iota = lax.broadcasted_iota(I32, (TK, TQ), 0)
    t_idx = q0 + lax.broadcasted_iota(I32, (TK, TQ), 1)
    scan_rows = lax.broadcasted_iota(I32, (SCAN_ROWS, TQ), 0)
    scan_t = q0 + lax.broadcasted_iota(I32, (SCAN_ROWS, TQ), 1)
    n_scan = nkc // SCAN_CHUNKS

    def scores_at(b):
        return sc_ref[pl.ds(b * SCAN_CHUNKS, SCAN_CHUNKS)].reshape(SCAN_ROWS, TQ)

    def score_step(b, carry):
        smin, smax = carry
        ikc = ik_ref[pl.ds(b * SCAN_CHUNKS, SCAN_CHUNKS)].reshape(SCAN_ROWS, IDX_DIM)
        score = jnp.zeros((SCAN_ROWS, TQ), F32)
        for h in range(IDX_HEADS):
            lg = _dot(ikc, iqt_ref[h * IDX_DIM:(h + 1) * IDX_DIM, :])
            score = score + iwt_ref[h:h + 1, :] * jnp.maximum(lg, 0.0)
        smin = jnp.minimum(smin, jnp.min(score, axis=0, keepdims=True))
        smax = jnp.maximum(smax, jnp.max(score, axis=0, keepdims=True))
        score = jnp.where(b * SCAN_ROWS + scan_rows <= scan_t, score, NEG_INF)
        sc_ref[pl.ds(b * SCAN_CHUNKS, SCAN_CHUNKS)] = score.reshape(SCAN_CHUNKS, TK, TQ)
        return smin, smax

    big = jnp.float32(3e38)
    smin, smax = lax.fori_loop(0, n_scan, score_step,
                               (jnp.full((1, TQ), big, F32), jnp.full((1, TQ), -big, F32)))

    nb_f = n_beyond.astype(F32)
    ninf = jnp.float32(-jnp.inf)

    def count_ge(cand):
        def body(b, acc):
            return acc + _rows_to_sublanes(jnp.where(scores_at(b) >= cand, 1.0, 0.0), jnp.sum)
        acc = lax.fori_loop(0, n_scan, body, jnp.zeros((8, TQ), F32))
        return jnp.sum(acc, axis=0, keepdims=True) + jnp.where(cand <= NEG_INF, nb_f, 0.0)

    def max_below(h):
        def body(b, acc):
            s = scores_at(b)
            return jnp.maximum(acc, _rows_to_sublanes(jnp.where(s < h, s, ninf), jnp.max))
        acc = lax.fori_loop(0, n_scan, body, jnp.full((8, TQ), ninf, F32))
        return jnp.max(acc, axis=0, keepdims=True)

    def count_ge_and_max_below(v):
        def body(b, carry):
            acc, mx = carry
            s = scores_at(b)
            acc = acc + _rows_to_sublanes(jnp.where(s >= v, 1.0, 0.0), jnp.sum)
            mx = jnp.maximum(mx, _rows_to_sublanes(jnp.where(s < v, s, ninf), jnp.max))
            return acc, mx
        acc, mx = lax.fori_loop(0, n_scan, body,
                                (jnp.zeros((8, TQ), F32), jnp.full((8, TQ), ninf, F32)))
        cnt = jnp.sum(acc, axis=0, keepdims=True) + jnp.where(v <= NEG_INF, nb_f, 0.0)
        return cnt, jnp.max(mx, axis=0, keepdims=True)

    n_adm = (q0 + 1 + lax.broadcasted_iota(I32, (1, TQ), 1)).astype(F32)
    few = n_adm < k_sel
    lo0 = jnp.where(few, NEG_INF, smin)
    clo0 = jnp.where(few, k_sel, jnp.where(smin <= NEG_INF, jnp.float32(seq_len), n_adm))
    hi0 = smax + (jnp.abs(smax) * 1e-6 + 1e-30)
    chi0 = jnp.zeros((1, TQ), F32)

    def bracket_body(it, st):
        lo, hi, clo, chi, flo, fhi, side = st
        done = clo == k_sel
        frac = jnp.clip(flo / (flo - fhi), 1.0 / 512, 511.0 / 512)
        frac = jnp.where(clo - chi <= 2.0, 0.5, frac)
        cand = lo + (hi - lo) * frac
        zero_inside = jnp.logical_and(jnp.logical_and(lo < 0.0, hi > 0.0), it == 0)
        cand = jnp.where(done, lo, jnp.where(zero_inside, 0.0, cand))
        cnt = count_ge(cand)
        ge = cnt >= k_sel
        f = cnt - (k_sel - 0.5)
        new_side = jnp.where(ge, 1.0, -1.0)
        same = new_side == side
        flo_n = jnp.where(ge, f, jnp.where(same, flo * 0.5, flo))
        fhi_n = jnp.where(ge, jnp.where(same, fhi * 0.5, fhi), f)
        up_lo = jnp.logical_and(jnp.logical_not(done), ge)
        up_hi = jnp.logical_and(jnp.logical_not(done), jnp.logical_not(ge))
        lo = jnp.where(up_lo, cand, lo)
        clo = jnp.where(up_lo, cnt, clo)
        hi = jnp.where(up_hi, cand, hi)
        chi = jnp.where(up_hi, cnt, chi)
        flo = jnp.where(done, flo, flo_n)
        fhi = jnp.where(done, fhi, fhi_n)
        side = jnp.where(done, side, new_side)
        return lo, hi, clo, chi, flo, fhi, side

    lo, hi, clo, chi = lax.fori_loop(
        0, BRACKET_PASSES, bracket_body,
        (lo0, hi0, clo0, chi0, clo0 - (k_sel - 0.5), chi0 - (k_sel - 0.5),
         jnp.zeros((1, TQ), F32)))[:4]
    pending = jnp.sum(jnp.where(clo == k_sel, 0.0, 1.0))
    thr_ref[...] = lo
    tie_ref[...] = jnp.full((1, TQ), NO_TIE, F32)

    @pl.when(pending > 0.5)
    def _():
        fin0 = jnp.where(clo == k_sel, 1.0, 0.0)
        v0 = max_below(hi)

        def fin_cond(st):
            return jnp.logical_and(st[0] < seq_len + 2, st[1] > 0.5)

        def fin_body(st):
            j, _, fin, h, ch, v, kst, need = st
            cnt, v2 = count_ge_and_max_below(v)
            hit = jnp.logical_and(fin < 0.5, cnt >= k_sel)
            kst = jnp.where(hit, v, kst)
            need = jnp.where(jnp.logical_and(hit, cnt > k_sel), k_sel - ch, need)
            fin = jnp.where(hit, 1.0, fin)
            open_ = fin < 0.5
            h = jnp.where(open_, v, h)
            ch = jnp.where(open_, cnt, ch)
            v = jnp.where(open_, v2, v)
            return j + 1, jnp.sum(1.0 - fin), fin, h, ch, v, kst, need

        st2 = lax.while_loop(
            fin_cond, fin_body,
            (jnp.int32(0), pending, fin0, hi, chi, v0, lo, jnp.full((1, TQ), NO_TIE, F32)))
        thr_ref[...] = st2[6]
        tie_ref[...] = st2[7]

    kstar = thr_ref[...]
    need = tie_ref[...]
    fix = jnp.logical_or(need < NO_TIE, kstar <= NEG_INF)

    @pl.when(jnp.sum(jnp.where(fix, 1.0, 0.0)) > 0.5)
    def _():
        tri = jnp.where(lax.broadcasted_iota(I32, (TK, TK), 0) >= lax.broadcasted_iota(I32, (TK, TK), 1),
                        1.0, 0.0).astype(BF16)

        def rewrite(c, seen):
            s = sc_ref[c]
            tied = s == kstar
            rank = seen + _dot(tri, jnp.where(tied, 1.0, 0.0).astype(BF16))
            take = jnp.logical_or(s > kstar, jnp.logical_and(tied, rank <= need))
            take = jnp.logical_and(take, c * TK + row_iota <= t_idx)
            sc_ref[c] = jnp.where(take, 1.0, -1.0)
            return rank[TK - 1:TK, :]

        lax.fori_loop(0, nkc, rewrite, jnp.zeros((1, TQ), F32))
        thr_ref[...] = jnp.zeros((1, TQ), F32)

    m_ref[...] = jnp.full(m_ref.shape, NEG_INF, F32)
    acc_ref[...] = jnp.zeros(acc_ref.shape, F32)
    thr = thr_ref[...]

    def masked_logits(c):
        bias = jnp.where(sc_ref[c] >= thr, 0.0, NEG_INF)
        lg = _dot(k_ref[c], qw_ref[...])
        lg = jnp.concatenate(
            [lg[:, h * TQ:(h + 1) * TQ] + bias for h in range(N_HEADS)], axis=1)
        return lg, jnp.max(lg, axis=0, keepdims=True)

    def attend(c, lg, lg_max):
        m_old = m_ref[...]
        m_new = jnp.maximum(m_old, lg_max)
        alpha = jnp.exp2(m_old - m_new)
        p = jnp.exp2(lg - m_new).astype(BF16)
        acc_ref[...] = alpha * acc_ref[...] + _dot(va_ref[c], p)
        m_ref[...] = m_new

    def attend_pair(i, max_a):
        c = 2 * i
        lg_b, max_b = masked_logits(c + 1)
        lgb_ref[...] = lg_b
        attend(c, lga_ref[...], max_a)
        lg_a, max_a = masked_logits(jnp.minimum(c + 2, nkc - 1))
        attend(c + 1, lgb_ref[...], max_b)
        lga_ref[...] = lg_a
        return max_a

    lg_a, max_a = masked_logits(0)
    lga_ref[...] = lg_a
    lax.fori_loop(0, nkc // 2, attend_pair, max_a)

    a = acc_ref[...]
    o = a[:HEAD_DIM] / a[HEAD_DIM:HEAD_DIM + 1]
    o = jnp.concatenate([o[:, h * TQ:(h + 1) * TQ] for h in range(N_HEADS)], axis=0)
    o_ref[...] = o.T


def _dsa_attention(iqt, iwt, ik, qw, k, vaug):
    B, _, L = iqt.shape
    n_sel = min(TOPK_MAX, L // 4)
    nch = L // TK
    va_rows = vaug.shape[2]
    grid = (B, L // TQ)
    kern = functools.partial(_dsa_kernel, seq_len=L, n_sel=n_sel)
    return pl.pallas_call(
        kern,
        grid=grid,
        in_specs=[
            pl.BlockSpec((None, IDX_HEADS * IDX_DIM, TQ), lambda b, i: (b, 0, i)),
            pl.BlockSpec((None, IDX_HEADS, TQ), lambda b, i: (b, 0, i)),
            pl.BlockSpec((None, nch, TK, IDX_DIM), lambda b, i: (b, 0, 0, 0)),
            pl.BlockSpec((None, None, HEAD_DIM, N_HEADS * TQ), lambda b, i: (b, i, 0, 0)),
            pl.BlockSpec((None, nch, TK, HEAD_DIM), lambda b, i: (b, 0, 0, 0)),
            pl.BlockSpec((None, nch, va_rows, TK), lambda b, i: (b, 0, 0, 0)),
        ],
        out_specs=pl.BlockSpec((None, TQ, ATTN_WIDTH), lambda b, i: (b, i, 0)),
        out_shape=jax.ShapeDtypeStruct((B, L, ATTN_WIDTH), F32),
        scratch_shapes=[
            pltpu.VMEM((nch, TK, TQ), F32),
            pltpu.VMEM((1, TQ), F32),
            pltpu.VMEM((1, TQ), F32),
            pltpu.VMEM((1, N_HEADS * TQ), F32),
            pltpu.VMEM((va_rows, N_HEADS * TQ), F32),
            pltpu.VMEM((TK, N_HEADS * TQ), F32),
            pltpu.VMEM((TK, N_HEADS * TQ), F32),
        ],
        compiler_params=pltpu.CompilerParams(
            dimension_semantics=("arbitrary", "arbitrary"),
            vmem_limit_bytes=V7X_VMEM_LIMIT_BYTES),
        name="dsa_attn",
    )(iqt, iwt, ik, qw, k, vaug)


def _merge_kernel(x_ref, u_ref, uh_ref, a_ref, lng_ref, lnb_ref, wgate_ref, bgate_ref,
                  poolw_ref, pscale_ref, wpp_ref, wpa_ref, wout_ref, ln1g_ref, ln1b_ref,
                  wrh_ref, wrl_ref, br_ref, h1_ref, gates_ref, *, tiles_per_seq):
    tm = TM_MERGE
    i = pl.program_id(0)
    seq_start = (i % tiles_per_seq) == 0
    h = _layer_norm(x_ref[...], lng_ref[...], lnb_ref[...])
    hb = h.astype(BF16)

    halo = jnp.where(seq_start, 0.0, uh_ref[...])
    u = u_ref[...]
    ext = jnp.concatenate([halo, u], axis=0)
    pos = (i % tiles_per_seq) * tm + lax.broadcasted_iota(I32, (tm, 1), 0)
    mixed = []
    for g, w in enumerate(POOL_WINDOWS):
        s = ext[:, g * POOL_GROUP_DIM:(g + 1) * POOL_GROUP_DIM]
        span = 1
        while span < w:
            s = s + pltpu.roll(s, span, 0)
            span *= 2
        win = s[POOL_HALO:]
        cnt = jnp.minimum(pos + 1, w).astype(F32)
        ug = u[:, g * POOL_GROUP_DIM:(g + 1) * POOL_GROUP_DIM]
        delta = win / cnt - ug
        mixed.append(_dot(delta.astype(BF16), poolw_ref[g]))
    pool_out = jnp.concatenate(mixed, axis=1) * pscale_ref[...]

    gate_pre = _dot(hb, wgate_ref[...]) + bgate_ref[...]
    gates = jax.nn.sigmoid(gate_pre)
    bp = _dot(pool_out.astype(BF16), wpp_ref[...])
    ba = _dot(a_ref[...].astype(BF16), wpa_ref[...])
    merged = gates[:, :D_MODEL] * bp + gates[:, D_MODEL:] * ba
    mix = _dot(merged.astype(BF16), wout_ref[...])
    h1 = _layer_norm(DEEPNORM_ALPHA * h + mix, ln1g_ref[...], ln1b_ref[...])
    h1_ref[...] = h1

    hi = h1.astype(BF16)
    lo = (h1 - hi.astype(F32)).astype(BF16)
    lg = _dot(hi, wrh_ref[...]) + (_dot(lo, wrh_ref[...]) + _dot(hi, wrl_ref[...])) + br_ref[...]
    lane = lax.broadcasted_iota(I32, (tm, LANES), 1).astype(F32)
    big = jnp.float32(1 << 20)
    ninf = jnp.float32(-jnp.inf)
    is_g = jnp.logical_and(lane >= N_EXPERTS, lane < N_EXPERTS + N_GROUPS)
    gl = jnp.where(is_g, lg, ninf)
    gmax = jnp.max(gl, axis=1, keepdims=True)
    gsel = jnp.min(jnp.where(gl == gmax, lane, big), axis=1, keepdims=True) - N_EXPERTS
    sumexp = jnp.sum(jnp.where(is_g, jnp.exp(gl - gmax), 0.0), axis=1, keepdims=True)
    p_group = 1.0 / sumexp
    e_lo = gsel * EXPERTS_PER_GROUP
    in_grp = jnp.logical_and(lane >= e_lo, lane < e_lo + EXPERTS_PER_GROUP)
    el = jnp.where(in_grp, lg, ninf)
    m1 = jnp.max(el, axis=1, keepdims=True)
    i1 = jnp.min(jnp.where(el == m1, lane, big), axis=1, keepdims=True)
    el2 = jnp.where(lane == i1, ninf, el)
    m2 = jnp.max(el2, axis=1, keepdims=True)
    i2 = jnp.min(jnp.where(el2 == m2, lane, big), axis=1, keepdims=True)
    e2 = jnp.exp(m2 - m1)
    den = 1.0 + e2
    w1 = (1.0 / den) * p_group
    w2 = (e2 / den) * p_group
    dense = jnp.where(lane == i1, w1, 0.0) + jnp.where(lane == i2, w2, 0.0)
    gates_ref[...] = dense[:, :N_EXPERTS]


def _merge(x, u, attn, ln_g, ln_b, wgate, bgate, poolw, pscale, wpp, wpa, wout, ln1g, ln1b,
           wrh, wrl, br, seq_len):
    T, D = x.shape
    tm = TM_MERGE
    tiles_per_seq = seq_len // tm
    grid = (T // tm,)
    tok = lambda i: (i, 0)
    c2 = lambda i: (0, 0)
    c3 = lambda i: (0, 0, 0)
    halo_blocks = tm // POOL_HALO
    kern = functools.partial(_merge_kernel, tiles_per_seq=tiles_per_seq)
    return pl.pallas_call(
        kern,
        grid=grid,
        in_specs=[
            pl.BlockSpec((tm, D), tok),
            pl.BlockSpec((tm, POOL_WIDTH), tok),
            pl.BlockSpec((POOL_HALO, POOL_WIDTH),
                         lambda i: (jnp.maximum(i * halo_blocks - 1, 0), 0)),
            pl.BlockSpec((tm, ATTN_WIDTH), tok),
            pl.BlockSpec((1, D), c2),
            pl.BlockSpec((1, D), c2),
            pl.BlockSpec((D, N_BRANCHES * D), c2),
            pl.BlockSpec((1, N_BRANCHES * D), c2),
            pl.BlockSpec((POOL_GROUPS, POOL_GROUP_DIM, POOL_GROUP_DIM), c3),
            pl.BlockSpec((1, POOL_WIDTH), c2),
            pl.BlockSpec((POOL_WIDTH, D), c2),
            pl.BlockSpec((ATTN_WIDTH, D), c2),
            pl.BlockSpec((D, D), c2),
            pl.BlockSpec((1, D), c2),
            pl.BlockSpec((1, D), c2),
            pl.BlockSpec((D, LANES), c2),
            pl.BlockSpec((D, LANES), c2),
            pl.BlockSpec((1, LANES), c2),
        ],
        out_specs=[
            pl.BlockSpec((tm, D), tok),
            pl.BlockSpec((tm, N_EXPERTS), tok),
        ],
        out_shape=[
            jax.ShapeDtypeStruct((T, D), F32),
            jax.ShapeDtypeStruct((T, N_EXPERTS), F32),
        ],
        compiler_params=pltpu.CompilerParams(
            dimension_semantics=("arbitrary",),
            vmem_limit_bytes=V7X_VMEM_LIMIT_BYTES),
        name="merge",
    )(x, u, u, attn, ln_g, ln_b, wgate, bgate, poolw, pscale, wpp, wpa, wout, ln1g, ln1b,
      wrh, wrl, br)


def _moe_kernel(h1_ref, g_ref, wgu_ref, wd_ref, ln2g_ref, ln2b_ref, o_ref, acc_ref):
    g = pl.program_id(1)
    x = h1_ref[...]
    xb = x.astype(BF16)
    ff = EXPERTS_PER_GROUP * EXPERT_FF
    hc = _dot(xb, wgu_ref[...])
    hid = jax.nn.silu(hc[:, :ff]) * hc[:, ff:]
    gates = g_ref[...]
    parts = []
    for j in range(EXPERTS_PER_GROUP):
        parts.append(hid[:, j * EXPERT_FF:(j + 1) * EXPERT_FF] * gates[:, j:j + 1])
    y = _dot(jnp.concatenate(parts, axis=1).astype(BF16), wd_ref[...])

    @pl.when(g == 0)
    def _():
        acc_ref[...] = y

    @pl.when(g > 0)
    def _():
        acc_ref[...] = acc_ref[...] + y

    @pl.when(g == N_GROUPS - 1)
    def _():
        o_ref[...] = _layer_norm(DEEPNORM_ALPHA * x + acc_ref[...], ln2g_ref[...], ln2b_ref[...])


def _moe(h1, gates_g, wgu, wd, ln2g, ln2b):
    T, D = h1.shape
    tm = TM_MOE
    ff = EXPERTS_PER_GROUP * EXPERT_FF
    grid = (T // tm, N_GROUPS)
    return pl.pallas_call(
        _moe_kernel,
        grid=grid,
        in_specs=[
            pl.BlockSpec((tm, D), lambda i, g: (i, 0)),
            pl.BlockSpec((None, tm, EXPERTS_PER_GROUP), lambda i, g: (g, i, 0)),
            pl.BlockSpec((None, D, 2 * ff), lambda i, g: (g, 0, 0)),
            pl.BlockSpec((None, ff, D), lambda i, g: (g, 0, 0)),
            pl.BlockSpec((1, D), lambda i, g: (0, 0)),
            pl.BlockSpec((1, D), lambda i, g: (0, 0)),
        ],
        out_specs=pl.BlockSpec((tm, D), lambda i, g: (i, 0)),
        out_shape=jax.ShapeDtypeStruct((T, D), F32),
        scratch_shapes=[pltpu.VMEM((tm, D), F32)],
        compiler_params=pltpu.CompilerParams(
            dimension_semantics=("arbitrary", "arbitrary"),
            vmem_limit_bytes=V7X_VMEM_LIMIT_BYTES),
        name="moe",
    )(h1, gates_g, wgu, wd, ln2g, ln2b)


def kernel(x, ln_in_g, ln_in_b, w_in, b_gate, pool_w, pool_scale, w_proj_pool, w_proj_attn, w_out,
           ln1_g, ln1_b, w_group, b_group, w_router, b_router, w_gate, w_up, w_down, ln2_g, ln2_b):
    B, L, D = x.shape
    assert D == D_MODEL and w_in.shape[0] == DEPTH == 1
    assert L % TQ == 0 and L % TM_MERGE == 0 and L % TM_PROJ == 0
    assert L <= 4096
    T = B * L
    row = lambda v: v.reshape(1, -1).astype(F32)

    w = w_in[0]
    o = 0
    w_u = w[:, o:o + POOL_WIDTH]; o += POOL_WIDTH
    w_q = w[:, o:o + ATTN_WIDTH]; o += ATTN_WIDTH
    w_k = w[:, o:o + KV_DIM]; o += KV_DIM
    w_v = w[:, o:o + KV_DIM]; o += KV_DIM
    w_iq = w[:, o:o + IDX_HEADS * IDX_DIM]; o += IDX_HEADS * IDX_DIM
    w_ik = w[:, o:o + IDX_DIM]; o += IDX_DIM
    w_iw = w[:, o:o + IDX_HEADS]; o += IDX_HEADS
    w_g = w[:, o:]
    sm_scale = math.log2(math.e) / math.sqrt(HEAD_DIM)
    bf16_rows = 16
    w_qd = (w_q * sm_scale).reshape(D, N_HEADS, HEAD_DIM).transpose(0, 2, 1).reshape(D, ATTN_WIDTH)
    w_t = jnp.concatenate([w_qd, w_v, w_iq, w_iw,
                           jnp.zeros((D, bf16_rows - IDX_HEADS), F32)], axis=1).T.astype(BF16)

    u, k, ik, qt, vt, iqt, iwt = _in_proj(
        x, row(ln_in_g), row(ln_in_b), w_u.astype(BF16), w_k.astype(BF16), w_ik.astype(BF16), w_t)

    nch = L // TK
    ones_rows = 16
    vaug = jnp.concatenate([vt, jnp.ones((B, ones_rows, L), BF16)], axis=1)
    vaug = vaug.reshape(B, KV_DIM + ones_rows, nch, TK).transpose(0, 2, 1, 3)
    qw = qt.reshape(B, L // TQ, HEAD_DIM, N_HEADS * TQ)
    attn = _dsa_attention(iqt, iwt, ik.reshape(B, nch, TK, IDX_DIM), qw,
                          k.reshape(B, nch, TK, HEAD_DIM), vaug)

    w_r = jnp.zeros((D, LANES), F32)
    w_r = w_r.at[:, :N_EXPERTS].set(w_router[0]).at[:, N_EXPERTS:N_EXPERTS + N_GROUPS].set(w_group[0])
    b_r = jnp.zeros((1, LANES), F32)
    b_r = b_r.at[0, :N_EXPERTS].set(b_router[0]).at[0, N_EXPERTS:N_EXPERTS + N_GROUPS].set(b_group[0])
    w_rh = w_r.astype(BF16)
    w_rl = (w_r - w_rh.astype(F32)).astype(BF16)

    h1, gates = _merge(
        x.reshape(T, D), u.reshape(T, POOL_WIDTH), attn.reshape(T, ATTN_WIDTH),
        row(ln_in_g), row(ln_in_b), w_g.astype(BF16), row(b_gate[0]),
        pool_w[0].astype(BF16), row(pool_scale[0]), w_proj_pool[0].astype(BF16),
        w_proj_attn[0].astype(BF16), w_out[0].astype(BF16), row(ln1_g[0]), row(ln1_b[0]),
        w_rh, w_rl, b_r, L)

    ff = EXPERTS_PER_GROUP * EXPERT_FF
    wg = w_gate[0].reshape(N_GROUPS, EXPERTS_PER_GROUP, D, EXPERT_FF).transpose(0, 2, 1, 3)
    wu = w_up[0].reshape(N_GROUPS, EXPERTS_PER_GROUP, D, EXPERT_FF).transpose(0, 2, 1, 3)
    wgu = jnp.concatenate([wg.reshape(N_GROUPS, D, ff), wu.reshape(N_GROUPS, D, ff)], axis=2).astype(BF16)
    wd = w_down[0].reshape(N_GROUPS, ff, D).astype(BF16)
    gates_g = gates.reshape(T, N_GROUPS, EXPERTS_PER_GROUP).transpose(1, 0, 2)

    out = _moe(h1, gates_g, wgu, wd, row(ln2_g[0]), row(ln2_b[0]))
    return out.reshape(B, L, D)
```

```python
import functools
import math

import jax
import jax.numpy as jnp
import numpy as np
from jax import lax
from jax.experimental import pallas as pl
from jax.experimental.pallas import tpu as pltpu

D_MODEL = 1024
POOL_WINDOWS = (2, 4, 8, 16)
POOL_GROUPS = 4
POOL_WIDTH = 512
POOL_GROUP_DIM = 128
N_HEADS = 8
HEAD_DIM = 64
ATTN_WIDTH = 512
KV_DIM = 64
IDX_HEADS = 8
IDX_DIM = 32
TOPK_MAX = 256
NEG_INF = float(np.float32(-1e30))
N_BRANCHES = 2
N_GROUPS = 4
EXPERTS_PER_GROUP = 8
N_EXPERTS = 32
EXPERT_FF = 256
LN_EPS = 1e-5
DEPTH = 1
DEEPNORM_ALPHA = (2.0 * DEPTH) ** 0.25

V7X_VMEM_LIMIT_BYTES = 56 * 1024 * 1024
LANES = 128

F32 = jnp.float32
BF16 = jnp.bfloat16
I32 = jnp.int32
INT_MIN = -2 ** 31
INT_MAX = 2 ** 31 - 1

TM_PROJ = 512
TQ = 256
TK = 128
TM_MERGE = 512
POOL_HALO = 16
TM_MOE = 512
ROW_WIDTH = D_MODEL + LANES
ROUTE_WIDTH = 8
PERMUTE_ROWS = 512


def _layer_norm(x, g, b):
    mu = jnp.mean(x, axis=-1, keepdims=True)
    xc = x - mu
    var = jnp.mean(xc * xc, axis=-1, keepdims=True)
    return xc * lax.rsqrt(var + LN_EPS) * g + b


def _dot(a, b):
    return jnp.dot(a, b, preferred_element_type=F32)


def _dot_nt(a, b):
    return lax.dot_general(a, b, (((1,), (1,)), ((), ())), preferred_element_type=F32)


def _in_proj_kernel(x_ref, g_ref, b_ref, wu_ref, wk_ref, wik_ref, wt_ref,
                    u_ref, k_ref, ik_ref, qt_ref, vt_ref, iqt_ref, iwt_ref):
    h = _layer_norm(x_ref[...], g_ref[...], b_ref[...])
    hb = h.astype(BF16)
    u_ref[...] = _dot(hb, wu_ref[...])
    k_ref[...] = _dot(hb, wk_ref[...]).astype(BF16)
    ik_ref[...] = _dot(hb, wik_ref[...]).astype(BF16)
    pt = _dot_nt(wt_ref[...], hb)
    r0 = 0
    for j in range(TM_PROJ // TQ):
        qt_ref[j] = pt[r0:r0 + ATTN_WIDTH, j * TQ:(j + 1) * TQ].astype(BF16)
    r0 += ATTN_WIDTH
    vt_ref[...] = pt[r0:r0 + KV_DIM].astype(BF16)
    r0 += KV_DIM
    iqt_ref[...] = pt[r0:r0 + IDX_HEADS * IDX_DIM].astype(BF16)
    r0 += IDX_HEADS * IDX_DIM
    iwt_ref[...] = pt[r0:r0 + IDX_HEADS]


def _in_proj(x, ln_g, ln_b, wu, wk, wik, wt):
    B, L, D = x.shape
    tm = TM_PROJ
    grid = (B, L // tm)
    tok = lambda b, i: (b, i, 0)
    tokt = lambda b, i: (b, 0, i)
    const2 = lambda b, i: (0, 0)
    n_t = wt.shape[0]
    return pl.pallas_call(
        _in_proj_kernel,
        grid=grid,
        in_specs=[
            pl.BlockSpec((None, tm, D), tok),
            pl.BlockSpec((1, D), const2),
            pl.BlockSpec((1, D), const2),
            pl.BlockSpec((D, POOL_WIDTH), const2),
            pl.BlockSpec((D, KV_DIM), const2),
            pl.BlockSpec((D, IDX_DIM), const2),
            pl.BlockSpec((n_t, D), const2),
        ],
        out_specs=[
            pl.BlockSpec((None, tm, POOL_WIDTH), tok),
            pl.BlockSpec((None, tm, KV_DIM), tok),
            pl.BlockSpec((None, tm, IDX_DIM), tok),
            pl.BlockSpec((None, tm // TQ, ATTN_WIDTH, TQ), lambda b, i: (b, i, 0, 0)),
            pl.BlockSpec((None, KV_DIM, tm), tokt),
            pl.BlockSpec((None, IDX_HEADS * IDX_DIM, tm), tokt),
            pl.BlockSpec((None, IDX_HEADS, tm), tokt),
        ],
        out_shape=[
            jax.ShapeDtypeStruct((B, L, POOL_WIDTH), F32),
            jax.ShapeDtypeStruct((B, L, KV_DIM), BF16),
            jax.ShapeDtypeStruct((B, L, IDX_DIM), BF16),
            jax.ShapeDtypeStruct((B, L // TQ, ATTN_WIDTH, TQ), BF16),
            jax.ShapeDtypeStruct((B, KV_DIM, L), BF16),
            jax.ShapeDtypeStruct((B, IDX_HEADS * IDX_DIM, L), BF16),
            jax.ShapeDtypeStruct((B, IDX_HEADS, L), F32),
        ],
        compiler_params=pltpu.CompilerParams(
            dimension_semantics=("arbitrary", "arbitrary"),
            vmem_limit_bytes=V7X_VMEM_LIMIT_BYTES),
        name="in_proj",
    )(x, ln_g, ln_b, wu, wk, wik, wt)


BRACKET_PASSES = 14
NO_TIE = 1e9


def _rows_to_sublanes(x, op):
    return op(x.reshape(x.shape[0] // 8, 8, TQ), axis=0)


SCAN_CHUNKS = 2
SCAN_ROWS = SCAN_CHUNKS * TK


def _dsa_kernel(iqt_ref, iwt_ref, ik_ref, qw_ref, k_ref, va_ref, o_ref,
                sc_ref, thr_ref, tie_ref, m_ref, acc_ref, lga_ref, lgb_ref, *, seq_len, n_sel):
    assert (TQ // TK) % 2 == 0
    qi = pl.program_id(1)
    q0 = qi * TQ
    nkc = (qi + 1) * (TQ // TK)
    n_beyond = seq_len - (qi + 1) * TQ
    k_sel = jnp.float32(n_sel)

    row_iota = lax.broadcasted_iota(I32, (TK, TQ), 0)
    t_idx = q0 + lax.broadcasted_iota(I32, (TK, TQ), 1)
    scan_rows = lax.broadcasted_iota(I32, (SCAN_ROWS, TQ), 0)
    scan_t = q0 + lax.broadcasted_iota(I32, (SCAN_ROWS, TQ), 1)
    n_scan = nkc // SCAN_CHUNKS

    def scores_at(b):
        return sc_ref[pl.ds(b * SCAN_CHUNKS, SCAN_CHUNKS)].reshape(SCAN_ROWS, TQ)

    def score_step(b, carry):
        smin, smax = carry
        ikc = ik_ref[pl.ds(b * SCAN_CHUNKS, SCAN_CHUNKS)].reshape(SCAN_ROWS, IDX_DIM)
        score = jnp.zeros((SCAN_ROWS, TQ), F32)
        for h in range(IDX_HEADS):
            lg = _dot(ikc, iqt_ref[h * IDX_DIM:(h + 1) * IDX_DIM, :])
            score = score + iwt_ref[h:h + 1, :] * jnp.maximum(lg, 0.0)
        smin = jnp.minimum(smin, jnp.min(score, axis=0, keepdims=True))
        smax = jnp.maximum(smax, jnp.max(score, axis=0, keepdims=True))
        score = jnp.where(b * SCAN_ROWS + scan_rows <= scan_t, score, NEG_INF)
        sc_ref[pl.ds(b * SCAN_CHUNKS, SCAN_CHUNKS)] = score.reshape(SCAN_CHUNKS, TK, TQ)
        return smin, smax

    big = jnp.float32(3e38)
    smin, smax = lax.fori_loop(0, n_scan, score_step,
                               (jnp.full((1, TQ), big, F32), jnp.full((1, TQ), -big, F32)))

    nb_f = n_beyond.astype(F32)
    ninf = jnp.float32(-jnp.inf)

    def count_ge(cand):
        def body(b, acc):
            return acc + _rows_to_sublanes(jnp.where(scores_at(b) >= cand, 1.0, 0.0), jnp.sum)
        acc = lax.fori_loop(0, n_scan, body, jnp.zeros((8, TQ), F32))
        return jnp.sum(acc, axis=0, keepdims=True) + jnp.where(cand <= NEG_INF, nb_f, 0.0)

    def max_below(h):
        def body(b, acc):
            s = scores_at(b)
            return jnp.maximum(acc, _rows_to_sublanes(jnp.where(s < h, s, ninf), jnp.max))
        acc = lax.fori_loop(0, n_scan, body, jnp.full((8, TQ), ninf, F32))
        return jnp.max(acc, axis=0, keepdims=True)

    def count_ge_and_max_below(v):
        def body(b, carry):
            acc, mx = carry
            s = scores_at(b)
            acc = acc + _rows_to_sublanes(jnp.where(s >= v, 1.0, 0.0), jnp.sum)
            mx = jnp.maximum(mx, _rows_to_sublanes(jnp.where(s < v, s, ninf), jnp.max))
            return acc, mx
        acc, mx = lax.fori_loop(0, n_scan, body,
                                (jnp.zeros((8, TQ), F32), jnp.full((8, TQ), ninf, F32)))
        cnt = jnp.sum(acc, axis=0, keepdims=True) + jnp.where(v <= NEG_INF, nb_f, 0.0)
        return cnt, jnp.max(mx, axis=0, keepdims=True)

    n_adm = (q0 + 1 + lax.broadcasted_iota(I32, (1, TQ), 1)).astype(F32)
    few = n_adm < k_sel
    lo0 = jnp.where(few, NEG_INF, smin)
    clo0 = jnp.where(few, k_sel, jnp.where(smin <= NEG_INF, jnp.float32(seq_len), n_adm))
    hi0 = smax + (jnp.abs(smax) * 1e-6 + 1e-30)
    chi0 = jnp.zeros((1, TQ), F32)

    def bracket_body(it, st):
        lo, hi, clo, chi, flo, fhi, side = st
        done = clo == k_sel
        frac = jnp.clip(flo / (flo - fhi), 1.0 / 512, 511.0 / 512)
        frac = jnp.where(clo - chi <= 2.0, 0.5, frac)
        cand = lo + (hi - lo) * frac
        zero_inside = jnp.logical_and(jnp.logical_and(lo < 0.0, hi > 0.0), it == 0)
        cand = jnp.where(done, lo, jnp.where(zero_inside, 0.0, cand))
        cnt = count_ge(cand)
        ge = cnt >= k_sel
        f = cnt - (k_sel - 0.5)
        new_side = jnp.where(ge, 1.0, -1.0)
        same = new_side == side
        flo_n = jnp.where(ge, f, jnp.where(same, flo * 0.5, flo))
        fhi_n = jnp.where(ge, jnp.where(same, fhi * 0.5, fhi), f)
        up_lo = jnp.logical_and(jnp.logical_not(done), ge)
        up_hi = jnp.logical_and(jnp.logical_not(done), jnp.logical_not(ge))
        lo = jnp.where(up_lo, cand, lo)
        clo = jnp.where(up_lo, cnt, clo)
        hi = jnp.where(up_hi, cand, hi)
        chi = jnp.where(up_hi, cnt, chi)
        flo = jnp.where(done, flo, flo_n)
        fhi = jnp.where(done, fhi, fhi_n)
        side = jnp.where(done, side, new_side)
        return lo, hi, clo, chi, flo, fhi, side

    lo, hi, clo, chi = lax.fori_loop(
        0, BRACKET_PASSES, bracket_body,
        (lo0, hi0, clo0, chi0, clo0 - (k_sel - 0.5), chi0 - (k_sel - 0.5),
         jnp.zeros((1, TQ), F32)))[:4]
    pending = jnp.sum(jnp.where(clo == k_sel, 0.0, 1.0))
    thr_ref[...] = lo
    tie_ref[...] = jnp.full((1, TQ), NO_TIE, F32)

    @pl.when(pending > 0.5)
    def _():
        fin0 = jnp.where(clo == k_sel, 1.0, 0.0)
        v0 = max_below(hi)

        def fin_cond(st):
            return jnp.logical_and(st[0] < seq_len + 2, st[1] > 0.5)

        def fin_body(st):
            j, _, fin, h, ch, v, kst, need = st
            cnt, v2 = count_ge_and_max_below(v)
            hit = jnp.logical_and(fin < 0.5, cnt >= k_sel)
            kst = jnp.where(hit, v, kst)
            need = jnp.where(jnp.logical_and(hit, cnt > k_sel), k_sel - ch, need)
            fin = jnp.where(hit, 1.0, fin)
            open_ = fin < 0.5
            h = jnp.where(open_, v, h)
            ch = jnp.where(open_, cnt, ch)
            v = jnp.where(open_, v2, v)
            return j + 1, jnp.sum(1.0 - fin), fin, h, ch, v, kst, need

        st2 = lax.while_loop(
            fin_cond, fin_body,
            (jnp.int32(0), pending, fin0, hi, chi, v0, lo, jnp.full((1, TQ), NO_TIE, F32)))
        thr_ref[...] = st2[6]
        tie_ref[...] = st2[7]

    kstar = thr_ref[...]
    need = tie_ref[...]
    fix = jnp.logical_or(need < NO_TIE, kstar <= NEG_INF)

    @pl.when(jnp.sum(jnp.where(fix, 1.0, 0.0)) > 0.5)
    def _():
        tri = jnp.where(lax.broadcasted_iota(I32, (TK, TK), 0) >= lax.broadcasted_iota(I32, (TK, TK), 1),
                        1.0, 0.0).astype(BF16)

        def rewrite(c, seen):
            s = sc_ref[c]
            tied = s == kstar
            rank = seen + _dot(tri, jnp.where(tied, 1.0, 0.0).astype(BF16))
            take = jnp.logical_or(s > kstar, jnp.logical_and(tied, rank <= need))
            take = jnp.logical_and(take, c * TK + row_iota <= t_idx)
            sc_ref[c] = jnp.where(take, 1.0, -1.0)
            return rank[TK - 1:TK, :]

        lax.fori_loop(0, nkc, rewrite, jnp.zeros((1, TQ), F32))
        thr_ref[...] = jnp.zeros((1, TQ), F32)

    m_ref[...] = jnp.full(m_ref.shape, NEG_INF, F32)
    acc_ref[...] = jnp.zeros(acc_ref.shape, F32)
    thr = thr_ref[...]

    def masked_logits(c):
        bias = jnp.where(sc_ref[c] >= thr, 0.0, NEG_INF)
        lg = _dot(k_ref[c], qw_ref[...])
        lg = jnp.concatenate(
            [lg[:, h * TQ:(h + 1) * TQ] + bias for h in range(N_HEADS)], axis=1)
        return lg, jnp.max(lg, axis=0, keepdims=True)

    def attend(c, lg, lg_max):
        m_old = m_ref[...]
        m_new = jnp.maximum(m_old, lg_max)
        alpha = jnp.exp2(m_old - m_new)
        p = jnp.exp2(lg - m_new).astype(BF16)
        acc_ref[...] = alpha * acc_ref[...] + _dot(va_ref[c], p)
        m_ref[...] = m_new

    def attend_pair(i, max_a):
        c = 2 * i
        lg_b, max_b = masked_logits(c + 1)
        lgb_ref[...] = lg_b
        attend(c, lga_ref[...], max_a)
        lg_a, max_a = masked_logits(jnp.minimum(c + 2, nkc - 1))
        attend(c + 1, lgb_ref[...], max_b)
        lga_ref[...] = lg_a
        return max_a

    lg_a, max_a = masked_logits(0)
    lga_ref[...] = lg_a
    lax.fori_loop(0, nkc // 2, attend_pair, max_a)

    a = acc_ref[...]
    o = a[:HEAD_DIM] / a[HEAD_DIM:HEAD_DIM + 1]
    o = jnp.concatenate([o[:, h * TQ:(h + 1) * TQ] for h in range(N_HEADS)], axis=0)
    o_ref[...] = o.T


def _dsa_attention(iqt, iwt, ik, qw, k, vaug):
    B, _, L = iqt.shape
    n_sel = min(TOPK_MAX, L // 4)
    nch = L // TK
    va_rows = vaug.shape[2]
    grid = (B, L // TQ)
    kern = functools.partial(_dsa_kernel, seq_len=L, n_sel=n_sel)
    return pl.pallas_call(
        kern,
        grid=grid,
        in_specs=[
            pl.BlockSpec((None, IDX_HEADS * IDX_DIM, TQ), lambda b, i: (b, 0, i)),
            pl.BlockSpec((None, IDX_HEADS, TQ), lambda b, i: (b, 0, i)),
            pl.BlockSpec((None, nch, TK, IDX_DIM), lambda b, i: (b, 0, 0, 0)),
            pl.BlockSpec((None, None, HEAD_DIM, N_HEADS * TQ), lambda b, i: (b, i, 0, 0)),
            pl.BlockSpec((None, nch, TK, HEAD_DIM), lambda b, i: (b, 0, 0, 0)),
            pl.BlockSpec((None, nch, va_rows, TK), lambda b, i: (b, 0, 0, 0)),
        ],
        out_specs=pl.BlockSpec((None, TQ, ATTN_WIDTH), lambda b, i: (b, i, 0)),
        out_shape=jax.ShapeDtypeStruct((B, L, ATTN_WIDTH), F32),
        scratch_shapes=[
            pltpu.VMEM((nch, TK, TQ), F32),
            pltpu.VMEM((1, TQ), F32),
            pltpu.VMEM((1, TQ), F32),
            pltpu.VMEM((1, N_HEADS * TQ), F32),
            pltpu.VMEM((va_rows, N_HEADS * TQ), F32),
            pltpu.VMEM((TK, N_HEADS * TQ), F32),
            pltpu.VMEM((TK, N_HEADS * TQ), F32),
        ],
        compiler_params=pltpu.CompilerParams(
            dimension_semantics=("arbitrary", "arbitrary"),
            vmem_limit_bytes=V7X_VMEM_LIMIT_BYTES),
        name="dsa_attn",
    )(iqt, iwt, ik, qw, k, vaug)


def _merge_kernel(x_ref, u_ref, uh_ref, a_ref, lng_ref, lnb_ref, wgate_ref, bgate_ref,
                  poolw_ref, pscale_ref, wpp_ref, wpa_ref, wout_ref, ln1g_ref, ln1b_ref,
                  wrh_ref, wrl_ref, br_ref, tri_ref, hrow_ref, route_ref, counts_ref, run_ref,
                  *, tiles_per_seq):
    tm = TM_MERGE
    i = pl.program_id(0)
    seq_start = (i % tiles_per_seq) == 0
    h = _layer_norm(x_ref[...], lng_ref[...], lnb_ref[...])
    hb = h.astype(BF16)

    halo = jnp.where(seq_start, 0.0, uh_ref[...])
    u = u_ref[...]
    ext = jnp.concatenate([halo, u], axis=0)
    pos = (i % tiles_per_seq) * tm + lax.broadcasted_iota(I32, (tm, 1), 0)
    mixed = []
    for g, w in enumerate(POOL_WINDOWS):
        s = ext[:, g * POOL_GROUP_DIM:(g + 1) * POOL_GROUP_DIM]
        span = 1
        while span < w:
            s = s + pltpu.roll(s, span, 0)
            span *= 2
        win = s[POOL_HALO:]
        cnt = jnp.minimum(pos + 1, w).astype(F32)
        ug = u[:, g * POOL_GROUP_DIM:(g + 1) * POOL_GROUP_DIM]
        delta = win / cnt - ug
        mixed.append(_dot(delta.astype(BF16), poolw_ref[g]))
    pool_out = jnp.concatenate(mixed, axis=1) * pscale_ref[...]

    gate_pre = _dot(hb, wgate_ref[...]) + bgate_ref[...]
    gates = jax.nn.sigmoid(gate_pre)
    bp = _dot(pool_out.astype(BF16), wpp_ref[...])
    ba = _dot(a_ref[...].astype(BF16), wpa_ref[...])
    merged = gates[:, :D_MODEL] * bp + gates[:, D_MODEL:] * ba
    mix = _dot(merged.astype(BF16), wout_ref[...])
    h1 = _layer_norm(DEEPNORM_ALPHA * h + mix, ln1g_ref[...], ln1b_ref[...])
    hrow_ref[:, :D_MODEL] = h1

    hi = h1.astype(BF16)
    lo = (h1 - hi.astype(F32)).astype(BF16)
    lg = _dot(hi, wrh_ref[...]) + (_dot(lo, wrh_ref[...]) + _dot(hi, wrl_ref[...])) + br_ref[...]
    lane = lax.broadcasted_iota(I32, (tm, LANES), 1).astype(F32)
    big = jnp.float32(1 << 20)
    ninf = jnp.float32(-jnp.inf)
    is_g = jnp.logical_and(lane >= N_EXPERTS, lane < N_EXPERTS + N_GROUPS)
    gl = jnp.where(is_g, lg, ninf)
    gmax = jnp.max(gl, axis=1, keepdims=True)
    gsel = jnp.min(jnp.where(gl == gmax, lane, big), axis=1, keepdims=True) - N_EXPERTS
    sumexp = jnp.sum(jnp.where(is_g, jnp.exp(gl - gmax), 0.0), axis=1, keepdims=True)
    p_group = 1.0 / sumexp
    e_lo = gsel * EXPERTS_PER_GROUP
    in_grp = jnp.logical_and(lane >= e_lo, lane < e_lo + EXPERTS_PER_GROUP)
    el = jnp.where(in_grp, lg, ninf)
    m1 = jnp.max(el, axis=1, keepdims=True)
    i1 = jnp.min(jnp.where(el == m1, lane, big), axis=1, keepdims=True)
    el2 = jnp.where(lane == i1, ninf, el)
    m2 = jnp.max(el2, axis=1, keepdims=True)
    i2 = jnp.min(jnp.where(el2 == m2, lane, big), axis=1, keepdims=True)
    e2 = jnp.exp(m2 - m1)
    den = 1.0 + e2
    w1 = (1.0 / den) * p_group
    w2 = (e2 / den) * p_group
    dense = jnp.where(lane == i1, w1, 0.0) + jnp.where(lane == i2, w2, 0.0)
    side = jnp.where(gsel == 0.0, dense, 0.0)
    for g in range(1, N_GROUPS):
        side = side + jnp.where(gsel == float(g),
                                pltpu.roll(dense, LANES - g * EXPERTS_PER_GROUP, 1), 0.0)
    hrow_ref[:, D_MODEL:] = jnp.where(lane < EXPERTS_PER_GROUP, side, 0.0)

    @pl.when(i == 0)
    def _():
        run_ref[...] = jnp.zeros(run_ref.shape, F32)

    onehot = jnp.where(lane == gsel, 1.0, 0.0)
    cum = _dot(tri_ref[...], onehot.astype(BF16))
    run = run_ref[...]
    rank = jnp.sum(onehot * (cum - 1.0 + run), axis=1, keepdims=True)
    lane8 = lax.broadcasted_iota(I32, (tm, ROUTE_WIDTH), 1)
    route_ref[...] = jnp.where(lane8 == 0, gsel, jnp.where(lane8 == 1, rank, 0.0))
    run = run + cum[tm - 1:tm, :]
    run_ref[...] = run
    counts_ref[...] = run


def _merge(x, u, attn, ln_g, ln_b, wgate, bgate, poolw, pscale, wpp, wpa, wout, ln1g, ln1b,
           wrh, wrl, br, seq_len):
    T, D = x.shape
    tm = TM_MERGE
    tiles_per_seq = seq_len // tm
    grid = (T // tm,)
    tok = lambda i: (i, 0)
    c2 = lambda i: (0, 0)
    c3 = lambda i: (0, 0, 0)
    halo_blocks = tm // POOL_HALO
    kern = functools.partial(_merge_kernel, tiles_per_seq=tiles_per_seq)
    return pl.pallas_call(
        kern,
        grid=grid,
        in_specs=[
            pl.BlockSpec((tm, D), tok),
            pl.BlockSpec((tm, POOL_WIDTH), tok),
            pl.BlockSpec((POOL_HALO, POOL_WIDTH),
                         lambda i: (jnp.maximum(i * halo_blocks - 1, 0), 0)),
            pl.BlockSpec((tm, ATTN_WIDTH), tok),
            pl.BlockSpec((1, D), c2),
            pl.BlockSpec((1, D), c2),
            pl.BlockSpec((D, N_BRANCHES * D), c2),
            pl.BlockSpec((1, N_BRANCHES * D), c2),
            pl.BlockSpec((POOL_GROUPS, POOL_GROUP_DIM, POOL_GROUP_DIM), c3),
            pl.BlockSpec((1, POOL_WIDTH), c2),
            pl.BlockSpec((POOL_WIDTH, D), c2),
            pl.BlockSpec((ATTN_WIDTH, D), c2),
            pl.BlockSpec((D, D), c2),
            pl.BlockSpec((1, D), c2),
            pl.BlockSpec((1, D), c2),
            pl.BlockSpec((D, LANES), c2),
            pl.BlockSpec((D, LANES), c2),
            pl.BlockSpec((1, LANES), c2),
            pl.BlockSpec((tm, tm), c2),
        ],
        out_specs=[
            pl.BlockSpec((tm, ROW_WIDTH), tok),
            pl.BlockSpec((tm, ROUTE_WIDTH), tok),
            pl.BlockSpec((1, LANES), c2),
        ],
        out_shape=[
            jax.ShapeDtypeStruct((T, ROW_WIDTH), F32),
            jax.ShapeDtypeStruct((T, ROUTE_WIDTH), F32),
            jax.ShapeDtypeStruct((1, LANES), F32),
        ],
        scratch_shapes=[pltpu.VMEM((1, LANES), F32)],
        compiler_params=pltpu.CompilerParams(
            dimension_semantics=("arbitrary",),
            vmem_limit_bytes=V7X_VMEM_LIMIT_BYTES),
        name="merge",
    )(x, u, u, attn, ln_g, ln_b, wgate, bgate, poolw, pscale, wpp, wpa, wout, ln1g, ln1b,
      wrh, wrl, br, jnp.tril(jnp.ones((tm, tm), BF16)))


def _row_copy(src_ref, src_row, dst_ref, dst_row, sem):
    return pltpu.make_async_copy(src_ref.at[pl.ds(src_row, 1)], dst_ref.at[pl.ds(dst_row, 1)], sem)


def _scatter_rows_kernel(dest_ref, src_ref, init_ref, out_ref, sem):
    del init_ref
    base = pl.program_id(0) * PERMUTE_ROWS

    def start(r, carry):
        _row_copy(src_ref, base + r, out_ref, dest_ref[0, r], sem).start()
        return carry

    def wait(r, carry):
        _row_copy(src_ref, base + r, out_ref, dest_ref[0, r], sem).wait()
        return carry

    lax.fori_loop(0, PERMUTE_ROWS, start, 0)
    lax.fori_loop(0, PERMUTE_ROWS, wait, 0)


def _gather_rows_kernel(dest_ref, src_ref, out_ref, sem):
    base = pl.program_id(0) * PERMUTE_ROWS

    def start(r, carry):
        _row_copy(src_ref, dest_ref[0, r], out_ref, base + r, sem).start()
        return carry

    def wait(r, carry):
        _row_copy(src_ref, dest_ref[0, r], out_ref, base + r, sem).wait()
        return carry

    lax.fori_loop(0, PERMUTE_ROWS, start, 0)
    lax.fori_loop(0, PERMUTE_ROWS, wait, 0)


def _permute_rows(kernel_fn, dest, src, out_rows, name, init=None):
    T = dest.shape[0]
    width = src.shape[1]
    steps = T // PERMUTE_ROWS
    dest3 = dest.reshape(steps, 1, PERMUTE_ROWS)
    any_spec = pl.BlockSpec(memory_space=pl.ANY)
    operands = [dest3, src] + ([] if init is None else [init])
    return pl.pallas_call(
        kernel_fn,
        grid=(steps,),
        in_specs=[pl.BlockSpec((None, 1, PERMUTE_ROWS), lambda i: (i, 0, 0),
                               memory_space=pltpu.SMEM)] + [any_spec] * (len(operands) - 1),
        out_specs=any_spec,
        out_shape=jax.ShapeDtypeStruct((out_rows, width), src.dtype),
        scratch_shapes=[pltpu.SemaphoreType.DMA],
        input_output_aliases={} if init is None else {2: 0},
        compiler_params=pltpu.CompilerParams(dimension_semantics=("arbitrary",)),
        name=name,
    )(*operands)


def _moe_kernel(group_ref, valid_ref, row_ref, wgu_ref, wd_ref, ln2g_ref, ln2b_ref, o_ref):
    del group_ref
    i = pl.program_id(0)

    @pl.when(valid_ref[i] == 0)
    def _():
        o_ref[...] = jnp.zeros(o_ref.shape, F32)

    @pl.when(valid_ref[i] != 0)
    def _():
        x = row_ref[:, :D_MODEL]
        gates = row_ref[:, D_MODEL:]
        ff = EXPERTS_PER_GROUP * EXPERT_FF
        hc = _dot(x.astype(BF16), wgu_ref[...])
        hid = jax.nn.silu(hc[:, :ff]) * hc[:, ff:]
        parts = []
        for j in range(EXPERTS_PER_GROUP):
            parts.append(hid[:, j * EXPERT_FF:(j + 1) * EXPERT_FF] * gates[:, j:j + 1])
        y = _dot(jnp.concatenate(parts, axis=1).astype(BF16), wd_ref[...])
        o_ref[...] = _layer_norm(DEEPNORM_ALPHA * x + y, ln2g_ref[...], ln2b_ref[...])


def _moe(tile_group, tile_valid, rows, wgu, wd, ln2g, ln2b):
    Tp = rows.shape[0]
    D = D_MODEL
    tm = TM_MOE
    ff = EXPERTS_PER_GROUP * EXPERT_FF
    grid_spec = pltpu.PrefetchScalarGridSpec(
        num_scalar_prefetch=2,
        grid=(Tp // tm,),
        in_specs=[
            pl.BlockSpec((tm, ROW_WIDTH), lambda i, tg, tv: (i, 0)),
            pl.BlockSpec((None, D, 2 * ff), lambda i, tg, tv: (tg[i], 0, 0)),
            pl.BlockSpec((None, ff, D), lambda i, tg, tv: (tg[i], 0, 0)),
            pl.BlockSpec((1, D), lambda i, tg, tv: (0, 0)),
            pl.BlockSpec((1, D), lambda i, tg, tv: (0, 0)),
        ],
        out_specs=pl.BlockSpec((tm, D), lambda i, tg, tv: (i, 0)),
    )
    return pl.pallas_call(
        _moe_kernel,
        grid_spec=grid_spec,
        out_shape=jax.ShapeDtypeStruct((Tp, D), F32),
        compiler_params=pltpu.CompilerParams(
            dimension_semantics=("arbitrary",),
            vmem_limit_bytes=V7X_VMEM_LIMIT_BYTES),
        name="moe",
    )(tile_group, tile_valid, rows, wgu, wd, ln2g, ln2b)


def kernel(x, ln_in_g, ln_in_b, w_in, b_gate, pool_w, pool_scale, w_proj_pool, w_proj_attn, w_out,
           ln1_g, ln1_b, w_group, b_group, w_router, b_router, w_gate, w_up, w_down, ln2_g, ln2_b):
    B, L, D = x.shape
    assert D == D_MODEL and w_in.shape[0] == DEPTH == 1
    assert L % TQ == 0 and L % TM_MERGE == 0 and L % TM_PROJ == 0
    assert L <= 4096
    T = B * L
    row = lambda v: v.reshape(1, -1).astype(F32)

    w = w_in[0]
    o = 0
    w_u = w[:, o:o + POOL_WIDTH]; o += POOL_WIDTH
    w_q = w[:, o:o + ATTN_WIDTH]; o += ATTN_WIDTH
    w_k = w[:, o:o + KV_DIM]; o += KV_DIM
    w_v = w[:, o:o + KV_DIM]; o += KV_DIM
    w_iq = w[:, o:o + IDX_HEADS * IDX_DIM]; o += IDX_HEADS * IDX_DIM
    w_ik = w[:, o:o + IDX_DIM]; o += IDX_DIM
    w_iw = w[:, o:o + IDX_HEADS]; o += IDX_HEADS
    w_g = w[:, o:]
    sm_scale = math.log2(math.e) / math.sqrt(HEAD_DIM)
    bf16_rows = 16
    w_qd = (w_q * sm_scale).reshape(D, N_HEADS, HEAD_DIM).transpose(0, 2, 1).reshape(D, ATTN_WIDTH)
    w_t = jnp.concatenate([w_qd, w_v, w_iq, w_iw,
                           jnp.zeros((D, bf16_rows - IDX_HEADS), F32)], axis=1).T.astype(BF16)

    u, k, ik, qt, vt, iqt, iwt = _in_proj(
        x, row(ln_in_g), row(ln_in_b), w_u.astype(BF16), w_k.astype(BF16), w_ik.astype(BF16), w_t)

    nch = L // TK
    ones_rows = 16
    vaug = jnp.concatenate([vt, jnp.ones((B, ones_rows, L), BF16)], axis=1)
    vaug = vaug.reshape(B, KV_DIM + ones_rows, nch, TK).transpose(0, 2, 1, 3)
    qw = qt.reshape(B, L // TQ, HEAD_DIM, N_HEADS * TQ)
    attn = _dsa_attention(iqt, iwt, ik.reshape(B, nch, TK, IDX_DIM), qw,
                          k.reshape(B, nch, TK, HEAD_DIM), vaug)

    w_r = jnp.zeros((D, LANES), F32)
    w_r = w_r.at[:, :N_EXPERTS].set(w_router[0]).at[:, N_EXPERTS:N_EXPERTS + N_GROUPS].set(w_group[0])
    b_r = jnp.zeros((1, LANES), F32)
    b_r = b_r.at[0, :N_EXPERTS].set(b_router[0]).at[0, N_EXPERTS:N_EXPERTS + N_GROUPS].set(b_group[0])
    w_rh = w_r.astype(BF16)
    w_rl = (w_r - w_rh.astype(F32)).astype(BF16)

    rows, route, counts = _merge(
        x.reshape(T, D), u.reshape(T, POOL_WIDTH), attn.reshape(T, ATTN_WIDTH),
        row(ln_in_g), row(ln_in_b), w_g.astype(BF16), row(b_gate[0]),
        pool_w[0].astype(BF16), row(pool_scale[0]), w_proj_pool[0].astype(BF16),
        w_proj_attn[0].astype(BF16), w_out[0].astype(BF16), row(ln1_g[0]), row(ln1_b[0]),
        w_rh, w_rl, b_r, L)

    ff = EXPERTS_PER_GROUP * EXPERT_FF
    wg = w_gate[0].reshape(N_GROUPS, EXPERTS_PER_GROUP, D, EXPERT_FF).transpose(0, 2, 1, 3)
    wu = w_up[0].reshape(N_GROUPS, EXPERTS_PER_GROUP, D, EXPERT_FF).transpose(0, 2, 1, 3)
    wgu = jnp.concatenate([wg.reshape(N_GROUPS, D, ff), wu.reshape(N_GROUPS, D, ff)], axis=2).astype(BF16)
    wd = w_down[0].reshape(N_GROUPS, ff, D).astype(BF16)

    tm = TM_MOE
    n_rows = T + N_GROUPS * tm
    cnt = counts[0, :N_GROUPS].astype(I32)
    padded = (cnt + (tm - 1)) // tm * tm
    ends = jnp.cumsum(padded)
    starts = ends - padded
    gsel = route[:, 0].astype(I32)
    rank = route[:, 1].astype(I32)
    dest = rank + sum(jnp.where(gsel == g, starts[g], 0) for g in range(N_GROUPS))
    tile_start = jnp.arange(n_rows // tm, dtype=I32) * tm
    tile_group = jnp.minimum(jnp.sum(tile_start[:, None] >= ends[None, :], axis=1), N_GROUPS - 1)
    tile_valid = (tile_start < ends[N_GROUPS - 1]).astype(I32)

    sorted_rows = _permute_rows(_scatter_rows_kernel, dest, rows, n_rows, "scatter_rows",
                                init=jnp.zeros((n_rows, ROW_WIDTH), F32))
    sorted_out = _moe(tile_group.astype(I32), tile_valid, sorted_rows, wgu, wd,
                      row(ln2_g[0]), row(ln2_b[0]))
    out = _permute_rows(_gather_rows_kernel, dest, sorted_out, T, "gather_rows")
    return out.reshape(B, L, D)
```

```python
import functools
import math

import jax
import jax.numpy as jnp
import numpy as np
from jax import lax
from jax.experimental import pallas as pl
from jax.experimental.pallas import tpu as pltpu

D_MODEL = 1024
POOL_WINDOWS = (2, 4, 8, 16)
POOL_GROUPS = 4
POOL_WIDTH = 512
POOL_GROUP_DIM = 128
N_HEADS = 8
HEAD_DIM = 64
ATTN_WIDTH = 512
KV_DIM = 64
IDX_HEADS = 8
IDX_DIM = 32
TOPK_MAX = 256
NEG_INF = float(np.float32(-1e30))
N_BRANCHES = 2
N_GROUPS = 4
EXPERTS_PER_GROUP = 8
N_EXPERTS = 32
EXPERT_FF = 256
LN_EPS = 1e-5
DEPTH = 1
DEEPNORM_ALPHA = (2.0 * DEPTH) ** 0.25

V7X_VMEM_LIMIT_BYTES = 56 * 1024 * 1024
LANES = 128

F32 = jnp.float32
BF16 = jnp.bfloat16
I32 = jnp.int32
INT_MIN = -2 ** 31
INT_MAX = 2 ** 31 - 1

TM_PROJ = 512
TQ = 256
TK = 128
TM_MERGE = 512
POOL_HALO = 16
V_ONES_ROWS = 16
TM_MOE = 512


def _layer_norm(x, g, b):
    mu = jnp.mean(x, axis=-1, keepdims=True)
    xc = x - mu
    var = jnp.mean(xc * xc, axis=-1, keepdims=True)
    return xc * lax.rsqrt(var + LN_EPS) * g + b


def _dot(a, b):
    return jnp.dot(a, b, preferred_element_type=F32)


def _dot_nt(a, b):
    return lax.dot_general(a, b, (((1,), (1,)), ((), ())), preferred_element_type=F32)


def _in_proj_kernel(x_ref, g_ref, b_ref, wu_ref, wk_ref, wik_ref, wt_ref,
                    u_ref, k_ref, ik_ref, qt_ref, vt_ref, iqt_ref, iwt_ref):
    h = _layer_norm(x_ref[...], g_ref[...], b_ref[...])
    hb = h.astype(BF16)
    u_ref[...] = _dot(hb, wu_ref[...])
    n_chunks = TM_PROJ // TK
    k_ref[...] = _dot(hb, wk_ref[...]).astype(BF16).reshape(n_chunks, TK, KV_DIM)
    ik_ref[...] = _dot(hb, wik_ref[...]).astype(BF16).reshape(n_chunks, TK, IDX_DIM)
    pt = _dot_nt(wt_ref[...], hb)
    r0 = 0
    for j in range(TM_PROJ // TQ):
        for h in range(N_HEADS):
            qt_ref[j, :, h * TQ:(h + 1) * TQ] = pt[r0 + h * HEAD_DIM:r0 + (h + 1) * HEAD_DIM,
                                                   j * TQ:(j + 1) * TQ].astype(BF16)
    r0 += ATTN_WIDTH
    for c in range(n_chunks):
        vt_ref[c, :KV_DIM, :] = pt[r0:r0 + KV_DIM, c * TK:(c + 1) * TK].astype(BF16)
        vt_ref[c, KV_DIM:, :] = jnp.ones((V_ONES_ROWS, TK), BF16)
    r0 += KV_DIM
    iqt_ref[...] = pt[r0:r0 + IDX_HEADS * IDX_DIM].astype(BF16)
    r0 += IDX_HEADS * IDX_DIM
    iwt_ref[...] = pt[r0:r0 + IDX_HEADS]


def _in_proj(x, ln_g, ln_b, wu, wk, wik, wt):
    B, L, D = x.shape
    tm = TM_PROJ
    grid = (B, L // tm)
    tok = lambda b, i: (b, i, 0)
    tokt = lambda b, i: (b, 0, i)
    chunked = lambda b, i: (b, i, 0, 0)
    const2 = lambda b, i: (0, 0)
    n_t = wt.shape[0]
    return pl.pallas_call(
        _in_proj_kernel,
        grid=grid,
        in_specs=[
            pl.BlockSpec((None, tm, D), tok),
            pl.BlockSpec((1, D), const2),
            pl.BlockSpec((1, D), const2),
            pl.BlockSpec((D, POOL_WIDTH), const2),
            pl.BlockSpec((D, KV_DIM), const2),
            pl.BlockSpec((D, IDX_DIM), const2),
            pl.BlockSpec((n_t, D), const2),
        ],
        out_specs=[
            pl.BlockSpec((None, tm, POOL_WIDTH), tok),
            pl.BlockSpec((None, tm // TK, TK, KV_DIM), chunked),
            pl.BlockSpec((None, tm // TK, TK, IDX_DIM), chunked),
            pl.BlockSpec((None, tm // TQ, HEAD_DIM, N_HEADS * TQ), chunked),
            pl.BlockSpec((None, tm // TK, KV_DIM + V_ONES_ROWS, TK), chunked),
            pl.BlockSpec((None, IDX_HEADS * IDX_DIM, tm), tokt),
            pl.BlockSpec((None, IDX_HEADS, tm), tokt),
        ],
        out_shape=[
            jax.ShapeDtypeStruct((B, L, POOL_WIDTH), F32),
            jax.ShapeDtypeStruct((B, L // TK, TK, KV_DIM), BF16),
            jax.ShapeDtypeStruct((B, L // TK, TK, IDX_DIM), BF16),
            jax.ShapeDtypeStruct((B, L // TQ, HEAD_DIM, N_HEADS * TQ), BF16),
            jax.ShapeDtypeStruct((B, L // TK, KV_DIM + V_ONES_ROWS, TK), BF16),
            jax.ShapeDtypeStruct((B, IDX_HEADS * IDX_DIM, L), BF16),
            jax.ShapeDtypeStruct((B, IDX_HEADS, L), F32),
        ],
        compiler_params=pltpu.CompilerParams(
            dimension_semantics=("arbitrary", "arbitrary"),
            vmem_limit_bytes=V7X_VMEM_LIMIT_BYTES),
        name="in_proj",
    )(x, ln_g, ln_b, wu, wk, wik, wt)


BRACKET_PASSES = 14
NO_TIE = 1e9


def _rows_to_sublanes(x, op):
    return op(x.reshape(x.shape[0] // 8, 8, TQ), axis=0)


SCAN_CHUNKS = 2
SCAN_ROWS = SCAN_CHUNKS * TK


def _dsa_kernel(iqt_ref, iwt_ref, ik_ref, qw_ref, k_ref, va_ref, o_ref,
                sc_ref, thr_ref, tie_ref, m_ref, acc_ref, lga_ref, lgb_ref, *, seq_len, n_sel):
    assert (TQ // TK) % 2 == 0
    qi = pl.program_id(1)
    q0 = qi * TQ
    nkc = (qi + 1) * (TQ // TK)
    n_beyond = seq_len - (qi + 1) * TQ
    k_sel = jnp.float32(n_sel)

    row_iota = lax.broadcasted_iota(I32, (TK, TQ), 0)
    t_idx = q0 + lax.broadcasted_iota(I32, (TK, TQ), 1)
    scan_rows = lax.broadcasted_iota(I32, (SCAN_ROWS, TQ), 0)
    scan_t = q0 + lax.broadcasted_iota(I32, (SCAN_ROWS, TQ), 1)
    n_scan = nkc // SCAN_CHUNKS

    def scores_at(b):
        return sc_ref[pl.ds(b * SCAN_CHUNKS, SCAN_CHUNKS)].reshape(SCAN_ROWS, TQ)

    def score_step(b, carry):
        smin, smax = carry
        ikc = ik_ref[pl.ds(b * SCAN_CHUNKS, SCAN_CHUNKS)].reshape(SCAN_ROWS, IDX_DIM)
        score = jnp.zeros((SCAN_ROWS, TQ), F32)
        for h in range(IDX_HEADS):
            lg = _dot(ikc, iqt_ref[h * IDX_DIM:(h + 1) * IDX_DIM, :])
            score = score + iwt_ref[h:h + 1, :] * jnp.maximum(lg, 0.0)
        smin = jnp.minimum(smin, jnp.min(score, axis=0, keepdims=True))
        smax = jnp.maximum(smax, jnp.max(score, axis=0, keepdims=True))
        score = jnp.where(b * SCAN_ROWS + scan_rows <= scan_t, score, NEG_INF)
        sc_ref[pl.ds(b * SCAN_CHUNKS, SCAN_CHUNKS)] = score.reshape(SCAN_CHUNKS, TK, TQ)
        return smin, smax

    big = jnp.float32(3e38)
    smin, smax = lax.fori_loop(0, n_scan, score_step,
                               (jnp.full((1, TQ), big, F32), jnp.full((1, TQ), -big, F32)))

    nb_f = n_beyond.astype(F32)
    ninf = jnp.float32(-jnp.inf)

    def count_ge(cand):
        def body(b, acc):
            return acc + _rows_to_sublanes(jnp.where(scores_at(b) >= cand, 1.0, 0.0), jnp.sum)
        acc = lax.fori_loop(0, n_scan, body, jnp.zeros((8, TQ), F32))
        return jnp.sum(acc, axis=0, keepdims=True) + jnp.where(cand <= NEG_INF, nb_f, 0.0)

    def max_below(h):
        def body(b, acc):
            s = scores_at(b)
            return jnp.maximum(acc, _rows_to_sublanes(jnp.where(s < h, s, ninf), jnp.max))
        acc = lax.fori_loop(0, n_scan, body, jnp.full((8, TQ), ninf, F32))
        return jnp.max(acc, axis=0, keepdims=True)

    def count_ge_and_max_below(v):
        def body(b, carry):
            acc, mx = carry
            s = scores_at(b)
            acc = acc + _rows_to_sublanes(jnp.where(s >= v, 1.0, 0.0), jnp.sum)
            mx = jnp.maximum(mx, _rows_to_sublanes(jnp.where(s < v, s, ninf), jnp.max))
            return acc, mx
        acc, mx = lax.fori_loop(0, n_scan, body,
                                (jnp.zeros((8, TQ), F32), jnp.full((8, TQ), ninf, F32)))
        cnt = jnp.sum(acc, axis=0, keepdims=True) + jnp.where(v <= NEG_INF, nb_f, 0.0)
        return cnt, jnp.max(mx, axis=0, keepdims=True)

    n_adm = (q0 + 1 + lax.broadcasted_iota(I32, (1, TQ), 1)).astype(F32)
    few = n_adm < k_sel
    lo0 = jnp.where(few, NEG_INF, smin)
    clo0 = jnp.where(few, k_sel, jnp.where(smin <= NEG_INF, jnp.float32(seq_len), n_adm))
    hi0 = smax + (jnp.abs(smax) * 1e-6 + 1e-30)
    chi0 = jnp.zeros((1, TQ), F32)

    def bracket_body(it, st):
        lo, hi, clo, chi, flo, fhi, side = st
        done = clo == k_sel
        frac = jnp.clip(flo / (flo - fhi), 1.0 / 512, 511.0 / 512)
        frac = jnp.where(clo - chi <= 2.0, 0.5, frac)
        cand = lo + (hi - lo) * frac
        zero_inside = jnp.logical_and(jnp.logical_and(lo < 0.0, hi > 0.0), it == 0)
        cand = jnp.where(done, lo, jnp.where(zero_inside, 0.0, cand))
        cnt = count_ge(cand)
        ge = cnt >= k_sel
        f = cnt - (k_sel - 0.5)
        new_side = jnp.where(ge, 1.0, -1.0)
        same = new_side == side
        flo_n = jnp.where(ge, f, jnp.where(same, flo * 0.5, flo))
        fhi_n = jnp.where(ge, jnp.where(same, fhi * 0.5, fhi), f)
        up_lo = jnp.logical_and(jnp.logical_not(done), ge)
        up_hi = jnp.logical_and(jnp.logical_not(done), jnp.logical_not(ge))
        lo = jnp.where(up_lo, cand, lo)
        clo = jnp.where(up_lo, cnt, clo)
        hi = jnp.where(up_hi, cand, hi)
        chi = jnp.where(up_hi, cnt, chi)
        flo = jnp.where(done, flo, flo_n)
        fhi = jnp.where(done, fhi, fhi_n)
        side = jnp.where(done, side, new_side)
        return lo, hi, clo, chi, flo, fhi, side

    lo, hi, clo, chi = lax.fori_loop(
        0, BRACKET_PASSES, bracket_body,
        (lo0, hi0, clo0, chi0, clo0 - (k_sel - 0.5), chi0 - (k_sel - 0.5),
         jnp.zeros((1, TQ), F32)))[:4]
    pending = jnp.sum(jnp.where(clo == k_sel, 0.0, 1.0))
    thr_ref[...] = lo
    tie_ref[...] = jnp.full((1, TQ), NO_TIE, F32)

    @pl.when(pending > 0.5)
    def _():
        fin0 = jnp.where(clo == k_sel, 1.0, 0.0)
        v0 = max_below(hi)

        def fin_cond(st):
            return jnp.logical_and(st[0] < seq_len + 2, st[1] > 0.5)

        def fin_body(st):
            j, _, fin, h, ch, v, kst, need = st
            cnt, v2 = count_ge_and_max_below(v)
            hit = jnp.logical_and(fin < 0.5, cnt >= k_sel)
            kst = jnp.where(hit, v, kst)
            need = jnp.where(jnp.logical_and(hit, cnt > k_sel), k_sel - ch, need)
            fin = jnp.where(hit, 1.0, fin)
            open_ = fin < 0.5
            h = jnp.where(open_, v, h)
            ch = jnp.where(open_, cnt, ch)
            v = jnp.where(open_, v2, v)
            return j + 1, jnp.sum(1.0 - fin), fin, h, ch, v, kst, need

        st2 = lax.while_loop(
            fin_cond, fin_body,
            (jnp.int32(0), pending, fin0, hi, chi, v0, lo, jnp.full((1, TQ), NO_TIE, F32)))
        thr_ref[...] = st2[6]
        tie_ref[...] = st2[7]

    kstar = thr_ref[...]
    need = tie_ref[...]
    fix = jnp.logical_or(need < NO_TIE, kstar <= NEG_INF)

    @pl.when(jnp.sum(jnp.where(fix, 1.0, 0.0)) > 0.5)
    def _():
        tri = jnp.where(lax.broadcasted_iota(I32, (TK, TK), 0) >= lax.broadcasted_iota(I32, (TK, TK), 1),
                        1.0, 0.0).astype(BF16)

        def rewrite(c, seen):
            s = sc_ref[c]
            tied = s == kstar
            rank = seen + _dot(tri, jnp.where(tied, 1.0, 0.0).astype(BF16))
            take = jnp.logical_or(s > kstar, jnp.logical_and(tied, rank <= need))
            take = jnp.logical_and(take, c * TK + row_iota <= t_idx)
            sc_ref[c] = jnp.where(take, 1.0, -1.0)
            return rank[TK - 1:TK, :]

        lax.fori_loop(0, nkc, rewrite, jnp.zeros((1, TQ), F32))
        thr_ref[...] = jnp.zeros((1, TQ), F32)

    m_ref[...] = jnp.full(m_ref.shape, NEG_INF, F32)
    acc_ref[...] = jnp.zeros(acc_ref.shape, F32)
    thr = thr_ref[...]

    def masked_logits(c):
        bias = jnp.where(sc_ref[c] >= thr, 0.0, NEG_INF)
        lg = _dot(k_ref[c], qw_ref[...])
        lg = jnp.concatenate(
            [lg[:, h * TQ:(h + 1) * TQ] + bias for h in range(N_HEADS)], axis=1)
        return lg, jnp.max(lg, axis=0, keepdims=True)

    def attend(c, lg, lg_max):
        m_old = m_ref[...]
        m_new = jnp.maximum(m_old, lg_max)
        alpha = jnp.exp2(m_old - m_new)
        p = jnp.exp2(lg - m_new).astype(BF16)
        acc_ref[...] = alpha * acc_ref[...] + _dot(va_ref[c], p)
        m_ref[...] = m_new

    def attend_pair(i, max_a):
        c = 2 * i
        lg_b, max_b = masked_logits(c + 1)
        lgb_ref[...] = lg_b
        attend(c, lga_ref[...], max_a)
        lg_a, max_a = masked_logits(jnp.minimum(c + 2, nkc - 1))
        attend(c + 1, lgb_ref[...], max_b)
        lga_ref[...] = lg_a
        return max_a

    lg_a, max_a = masked_logits(0)
    lga_ref[...] = lg_a
    lax.fori_loop(0, nkc // 2, attend_pair, max_a)

    a = acc_ref[...]
    o = a[:HEAD_DIM] / a[HEAD_DIM:HEAD_DIM + 1]
    o = jnp.concatenate([o[:, h * TQ:(h + 1) * TQ] for h in range(N_HEADS)], axis=0)
    o_ref[...] = o.T


def _dsa_attention(iqt, iwt, ik, qw, k, vaug):
    B, _, L = iqt.shape
    n_sel = min(TOPK_MAX, L // 4)
    nch = L // TK
    va_rows = vaug.shape[2]
    grid = (B, L // TQ)
    kern = functools.partial(_dsa_kernel, seq_len=L, n_sel=n_sel)
    return pl.pallas_call(
        kern,
        grid=grid,
        in_specs=[
            pl.BlockSpec((None, IDX_HEADS * IDX_DIM, TQ), lambda b, i: (b, 0, i)),
            pl.BlockSpec((None, IDX_HEADS, TQ), lambda b, i: (b, 0, i)),
            pl.BlockSpec((None, nch, TK, IDX_DIM), lambda b, i: (b, 0, 0, 0)),
            pl.BlockSpec((None, None, HEAD_DIM, N_HEADS * TQ), lambda b, i: (b, i, 0, 0)),
            pl.BlockSpec((None, nch, TK, HEAD_DIM), lambda b, i: (b, 0, 0, 0)),
            pl.BlockSpec((None, nch, va_rows, TK), lambda b, i: (b, 0, 0, 0)),
        ],
        out_specs=pl.BlockSpec((None, TQ, ATTN_WIDTH), lambda b, i: (b, i, 0)),
        out_shape=jax.ShapeDtypeStruct((B, L, ATTN_WIDTH), F32),
        scratch_shapes=[
            pltpu.VMEM((nch, TK, TQ), F32),
            pltpu.VMEM((1, TQ), F32),
            pltpu.VMEM((1, TQ), F32),
            pltpu.VMEM((1, N_HEADS * TQ), F32),
            pltpu.VMEM((va_rows, N_HEADS * TQ), F32),
            pltpu.VMEM((TK, N_HEADS * TQ), F32),
            pltpu.VMEM((TK, N_HEADS * TQ), F32),
        ],
        compiler_params=pltpu.CompilerParams(
            dimension_semantics=("arbitrary", "arbitrary"),
            vmem_limit_bytes=V7X_VMEM_LIMIT_BYTES),
        name="dsa_attn",
    )(iqt, iwt, ik, qw, k, vaug)


def _merge_kernel(x_ref, u_ref, uh_ref, a_ref, lng_ref, lnb_ref, wgate_ref, bgate_ref,
                  poolw_ref, pscale_ref, wpp_ref, wpa_ref, wout_ref, ln1g_ref, ln1b_ref,
                  wrh_ref, wrl_ref, br_ref, h1_ref, gates_ref, *, tiles_per_seq):
    tm = TM_MERGE
    i = pl.program_id(0)
    seq_start = (i % tiles_per_seq) == 0
    h = _layer_norm(x_ref[...], lng_ref[...], lnb_ref[...])
    hb = h.astype(BF16)

    halo = jnp.where(seq_start, 0.0, uh_ref[...])
    u = u_ref[...]
    ext = jnp.concatenate([halo, u], axis=0)
    pos = (i % tiles_per_seq) * tm + lax.broadcasted_iota(I32, (tm, 1), 0)
    mixed = []
    for g, w in enumerate(POOL_WINDOWS):
        s = ext[:, g * POOL_GROUP_DIM:(g + 1) * POOL_GROUP_DIM]
        span = 1
        while span < w:
            s = s + pltpu.roll(s, span, 0)
            span *= 2
        win = s[POOL_HALO:]
        cnt = jnp.minimum(pos + 1, w).astype(F32)
        ug = u[:, g * POOL_GROUP_DIM:(g + 1) * POOL_GROUP_DIM]
        delta = win / cnt - ug
        mixed.append(_dot(delta.astype(BF16), poolw_ref[g]))
    pool_out = jnp.concatenate(mixed, axis=1) * pscale_ref[...]

    gate_pre = _dot(hb, wgate_ref[...]) + bgate_ref[...]
    gates = jax.nn.sigmoid(gate_pre)
    bp = _dot(pool_out.astype(BF16), wpp_ref[...])
    ba = _dot(a_ref[...].astype(BF16), wpa_ref[...])
    merged = gates[:, :D_MODEL] * bp + gates[:, D_MODEL:] * ba
    mix = _dot(merged.astype(BF16), wout_ref[...])
    h1 = _layer_norm(DEEPNORM_ALPHA * h + mix, ln1g_ref[...], ln1b_ref[...])
    h1_ref[...] = h1

    hi = h1.astype(BF16)
    lo = (h1 - hi.astype(F32)).astype(BF16)
    lg = _dot(hi, wrh_ref[...]) + (_dot(lo, wrh_ref[...]) + _dot(hi, wrl_ref[...])) + br_ref[...]
    lane = lax.broadcasted_iota(I32, (tm, LANES), 1).astype(F32)
    big = jnp.float32(1 << 20)
    ninf = jnp.float32(-jnp.inf)
    is_g = jnp.logical_and(lane >= N_EXPERTS, lane < N_EXPERTS + N_GROUPS)
    gl = jnp.where(is_g, lg, ninf)
    gmax = jnp.max(gl, axis=1, keepdims=True)
    gsel = jnp.min(jnp.where(gl == gmax, lane, big), axis=1, keepdims=True) - N_EXPERTS
    sumexp = jnp.sum(jnp.where(is_g, jnp.exp(gl - gmax), 0.0), axis=1, keepdims=True)
    p_group = 1.0 / sumexp
    e_lo = gsel * EXPERTS_PER_GROUP
    in_grp = jnp.logical_and(lane >= e_lo, lane < e_lo + EXPERTS_PER_GROUP)
    el = jnp.where(in_grp, lg, ninf)
    m1 = jnp.max(el, axis=1, keepdims=True)
    i1 = jnp.min(jnp.where(el == m1, lane, big), axis=1, keepdims=True)
    el2 = jnp.where(lane == i1, ninf, el)
    m2 = jnp.max(el2, axis=1, keepdims=True)
    i2 = jnp.min(jnp.where(el2 == m2, lane, big), axis=1, keepdims=True)
    e2 = jnp.exp(m2 - m1)
    den = 1.0 + e2
    w1 = (1.0 / den) * p_group
    w2 = (e2 / den) * p_group
    dense = jnp.where(lane == i1, w1, 0.0) + jnp.where(lane == i2, w2, 0.0)
    gates_ref[...] = dense[:, :N_EXPERTS]


def _merge(x, u, attn, ln_g, ln_b, wgate, bgate, poolw, pscale, wpp, wpa, wout, ln1g, ln1b,
           wrh, wrl, br, seq_len):
    T, D = x.shape
    tm = TM_MERGE
    tiles_per_seq = seq_len // tm
    grid = (T // tm,)
    tok = lambda i: (i, 0)
    c2 = lambda i: (0, 0)
    c3 = lambda i: (0, 0, 0)
    halo_blocks = tm // POOL_HALO
    kern = functools.partial(_merge_kernel, tiles_per_seq=tiles_per_seq)
    return pl.pallas_call(
        kern,
        grid=grid,
        in_specs=[
            pl.BlockSpec((tm, D), tok),
            pl.BlockSpec((tm, POOL_WIDTH), tok),
            pl.BlockSpec((POOL_HALO, POOL_WIDTH),
                         lambda i: (jnp.maximum(i * halo_blocks - 1, 0), 0)),
            pl.BlockSpec((tm, ATTN_WIDTH), tok),
            pl.BlockSpec((1, D), c2),
            pl.BlockSpec((1, D), c2),
            pl.BlockSpec((D, N_BRANCHES * D), c2),
            pl.BlockSpec((1, N_BRANCHES * D), c2),
            pl.BlockSpec((POOL_GROUPS, POOL_GROUP_DIM, POOL_GROUP_DIM), c3),
            pl.BlockSpec((1, POOL_WIDTH), c2),
            pl.BlockSpec((POOL_WIDTH, D), c2),
            pl.BlockSpec((ATTN_WIDTH, D), c2),
            pl.BlockSpec((D, D), c2),
            pl.BlockSpec((1, D), c2),
            pl.BlockSpec((1, D), c2),
            pl.BlockSpec((D, LANES), c2),
            pl.BlockSpec((D, LANES), c2),
            pl.BlockSpec((1, LANES), c2),
        ],
        out_specs=[
            pl.BlockSpec((tm, D), tok),
            pl.BlockSpec((tm, N_EXPERTS), tok),
        ],
        out_shape=[
            jax.ShapeDtypeStruct((T, D), F32),
            jax.ShapeDtypeStruct((T, N_EXPERTS), F32),
        ],
        compiler_params=pltpu.CompilerParams(
            dimension_semantics=("arbitrary",),
            vmem_limit_bytes=V7X_VMEM_LIMIT_BYTES),
        name="merge",
    )(x, u, u, attn, ln_g, ln_b, wgate, bgate, poolw, pscale, wpp, wpa, wout, ln1g, ln1b,
      wrh, wrl, br)


def _moe_kernel(h1_ref, g_ref, wg_ref, wu_ref, wd_ref, ln2g_ref, ln2b_ref, o_ref, acc_ref):
    g = pl.program_id(1)
    x = h1_ref[...]
    xb = x.astype(BF16)
    all_gates = g_ref[...]
    gates = jnp.zeros((x.shape[0], EXPERTS_PER_GROUP), F32)
    for gg in range(N_GROUPS):
        sl = all_gates[:, gg * EXPERTS_PER_GROUP:(gg + 1) * EXPERTS_PER_GROUP]
        gates = jnp.where(g == gg, sl, gates)
    parts = []
    for j in range(EXPERTS_PER_GROUP):
        hid = jax.nn.silu(_dot(xb, wg_ref[j])) * _dot(xb, wu_ref[j])
        parts.append((hid * gates[:, j:j + 1]).astype(BF16))
    ff = EXPERTS_PER_GROUP * EXPERT_FF
    y = _dot(jnp.concatenate(parts, axis=1), wd_ref[...].reshape(ff, D_MODEL))

    @pl.when(g == 0)
    def _():
        acc_ref[...] = y

    @pl.when(g > 0)
    def _():
        acc_ref[...] = acc_ref[...] + y

    @pl.when(g == N_GROUPS - 1)
    def _():
        o_ref[...] = _layer_norm(DEEPNORM_ALPHA * x + acc_ref[...], ln2g_ref[...], ln2b_ref[...])


def _moe(h1, gates, wg, wu, wd, ln2g, ln2b):
    T, D = h1.shape
    tm = TM_MOE
    epg = EXPERTS_PER_GROUP
    grid = (T // tm, N_GROUPS)
    return pl.pallas_call(
        _moe_kernel,
        grid=grid,
        in_specs=[
            pl.BlockSpec((tm, D), lambda i, g: (i, 0)),
            pl.BlockSpec((tm, N_EXPERTS), lambda i, g: (i, 0)),
            pl.BlockSpec((epg, D, EXPERT_FF), lambda i, g: (g, 0, 0)),
            pl.BlockSpec((epg, D, EXPERT_FF), lambda i, g: (g, 0, 0)),
            pl.BlockSpec((epg, EXPERT_FF, D), lambda i, g: (g, 0, 0)),
            pl.BlockSpec((1, D), lambda i, g: (0, 0)),
            pl.BlockSpec((1, D), lambda i, g: (0, 0)),
        ],
        out_specs=pl.BlockSpec((tm, D), lambda i, g: (i, 0)),
        out_shape=jax.ShapeDtypeStruct((T, D), F32),
        scratch_shapes=[pltpu.VMEM((tm, D), F32)],
        compiler_params=pltpu.CompilerParams(
            dimension_semantics=("arbitrary", "arbitrary"),
            vmem_limit_bytes=V7X_VMEM_LIMIT_BYTES),
        name="moe",
    )(h1, gates, wg, wu, wd, ln2g, ln2b)


def kernel(x, ln_in_g, ln_in_b, w_in, b_gate, pool_w, pool_scale, w_proj_pool, w_proj_attn, w_out,
           ln1_g, ln1_b, w_group, b_group, w_router, b_router, w_gate, w_up, w_down, ln2_g, ln2_b):
    B, L, D = x.shape
    assert D == D_MODEL and w_in.shape[0] == DEPTH == 1
    assert L % TQ == 0 and L % TM_MERGE == 0 and L % TM_PROJ == 0
    assert L <= 4096
    T = B * L
    row = lambda v: v.reshape(1, -1).astype(F32)

    w = w_in[0]
    o = 0
    w_u = w[:, o:o + POOL_WIDTH]; o += POOL_WIDTH
    w_q = w[:, o:o + ATTN_WIDTH]; o += ATTN_WIDTH
    w_k = w[:, o:o + KV_DIM]; o += KV_DIM
    w_v = w[:, o:o + KV_DIM]; o += KV_DIM
    w_iq = w[:, o:o + IDX_HEADS * IDX_DIM]; o += IDX_HEADS * IDX_DIM
    w_ik = w[:, o:o + IDX_DIM]; o += IDX_DIM
    w_iw = w[:, o:o + IDX_HEADS]; o += IDX_HEADS
    w_g = w[:, o:]
    sm_scale = math.log2(math.e) / math.sqrt(HEAD_DIM)
    bf16_rows = 16
    w_t = jnp.concatenate([w_q * sm_scale, w_v, w_iq, w_iw,
                           jnp.zeros((D, bf16_rows - IDX_HEADS), F32)], axis=1).T.astype(BF16)

    u, k, ik, qw, vaug, iqt, iwt = _in_proj(
        x, row(ln_in_g), row(ln_in_b), w_u.astype(BF16), w_k.astype(BF16), w_ik.astype(BF16), w_t)
    attn = _dsa_attention(iqt, iwt, ik, qw, k, vaug)

    w_r = jnp.zeros((D, LANES), F32)
    w_r = w_r.at[:, :N_EXPERTS].set(w_router[0]).at[:, N_EXPERTS:N_EXPERTS + N_GROUPS].set(w_group[0])
    b_r = jnp.zeros((1, LANES), F32)
    b_r = b_r.at[0, :N_EXPERTS].set(b_router[0]).at[0, N_EXPERTS:N_EXPERTS + N_GROUPS].set(b_group[0])
    w_rh = w_r.astype(BF16)
    w_rl = (w_r - w_rh.astype(F32)).astype(BF16)

    h1, gates = _merge(
        x.reshape(T, D), u.reshape(T, POOL_WIDTH), attn.reshape(T, ATTN_WIDTH),
        row(ln_in_g), row(ln_in_b), w_g.astype(BF16), row(b_gate[0]),
        pool_w[0].astype(BF16), row(pool_scale[0]), w_proj_pool[0].astype(BF16),
        w_proj_attn[0].astype(BF16), w_out[0].astype(BF16), row(ln1_g[0]), row(ln1_b[0]),
        w_rh, w_rl, b_r, L)

    out = _moe(h1, gates, w_gate[0].astype(BF16), w_up[0].astype(BF16), w_down[0].astype(BF16),
               row(ln2_g[0]), row(ln2_b[0]))
    return out.reshape(B, L, D)
```

```python
import functools
import math

import jax
import jax.numpy as jnp
import numpy as np
from jax import lax
from jax.experimental import pallas as pl
from jax.experimental.pallas import tpu as pltpu

D_MODEL = 1024
POOL_WINDOWS = (2, 4, 8, 16)
POOL_GROUPS = 4
POOL_WIDTH = 512
POOL_GROUP_DIM = 128
N_HEADS = 8
HEAD_DIM = 64
ATTN_WIDTH = 512
KV_DIM = 64
IDX_HEADS = 8
IDX_DIM = 32
TOPK_MAX = 256
NEG_INF = float(np.float32(-1e30))
N_BRANCHES = 2
N_GROUPS = 4
EXPERTS_PER_GROUP = 8
N_EXPERTS = 32
EXPERT_FF = 256
LN_EPS = 1e-5
DEPTH = 1
DEEPNORM_ALPHA = (2.0 * DEPTH) ** 0.25

V7X_VMEM_LIMIT_BYTES = 56 * 1024 * 1024
LANES = 128

F32 = jnp.float32
BF16 = jnp.bfloat16
I32 = jnp.int32
INT_MIN = -2 ** 31
INT_MAX = 2 ** 31 - 1

TM_PROJ = 512
TQ = 256
TK = 128
TM_MERGE = 512
POOL_HALO = 16
V_ONES_ROWS = 16
TM_MOE = 512
MOE_CHUNK = 128
MOE_SLOTS = TM_MOE + N_GROUPS * MOE_CHUNK
GSEL_LANE = N_EXPERTS


def _layer_norm(x, g, b):
    mu = jnp.mean(x, axis=-1, keepdims=True)
    xc = x - mu
    var = jnp.mean(xc * xc, axis=-1, keepdims=True)
    return xc * lax.rsqrt(var + LN_EPS) * g + b


def _dot(a, b):
    return jnp.dot(a, b, preferred_element_type=F32)


def _dot_nt(a, b):
    return lax.dot_general(a, b, (((1,), (1,)), ((), ())), preferred_element_type=F32)


def _in_proj_kernel(x_ref, g_ref, b_ref, wu_ref, wk_ref, wik_ref, wt_ref,
                    u_ref, k_ref, ik_ref, qt_ref, vt_ref, iqt_ref, iwt_ref):
    h = _layer_norm(x_ref[...], g_ref[...], b_ref[...])
    hb = h.astype(BF16)
    u_ref[...] = _dot(hb, wu_ref[...])
    n_chunks = TM_PROJ // TK
    k_ref[...] = _dot(hb, wk_ref[...]).astype(BF16).reshape(n_chunks, TK, KV_DIM)
    ik_ref[...] = _dot(hb, wik_ref[...]).astype(BF16).reshape(n_chunks, TK, IDX_DIM)
    pt = _dot_nt(wt_ref[...], hb)
    r0 = 0
    for j in range(TM_PROJ // TQ):
        for h in range(N_HEADS):
            qt_ref[j, :, h * TQ:(h + 1) * TQ] = pt[r0 + h * HEAD_DIM:r0 + (h + 1) * HEAD_DIM,
                                                   j * TQ:(j + 1) * TQ].astype(BF16)
    r0 += ATTN_WIDTH
    for c in range(n_chunks):
        vt_ref[c, :KV_DIM, :] = pt[r0:r0 + KV_DIM, c * TK:(c + 1) * TK].astype(BF16)
        vt_ref[c, KV_DIM:, :] = jnp.ones((V_ONES_ROWS, TK), BF16)
    r0 += KV_DIM
    iqt_ref[...] = pt[r0:r0 + IDX_HEADS * IDX_DIM].astype(BF16)
    r0 += IDX_HEADS * IDX_DIM
    iwt_ref[...] = pt[r0:r0 + IDX_HEADS]


def _in_proj(x, ln_g, ln_b, wu, wk, wik, wt):
    B, L, D = x.shape
    tm = TM_PROJ
    grid = (B, L // tm)
    tok = lambda b, i: (b, i, 0)
    tokt = lambda b, i: (b, 0, i)
    chunked = lambda b, i: (b, i, 0, 0)
    const2 = lambda b, i: (0, 0)
    n_t = wt.shape[0]
    return pl.pallas_call(
        _in_proj_kernel,
        grid=grid,
        in_specs=[
            pl.BlockSpec((None, tm, D), tok),
            pl.BlockSpec((1, D), const2),
            pl.BlockSpec((1, D), const2),
            pl.BlockSpec((D, POOL_WIDTH), const2),
            pl.BlockSpec((D, KV_DIM), const2),
            pl.BlockSpec((D, IDX_DIM), const2),
            pl.BlockSpec((n_t, D), const2),
        ],
        out_specs=[
            pl.BlockSpec((None, tm, POOL_WIDTH), tok),
            pl.BlockSpec((None, tm // TK, TK, KV_DIM), chunked),
            pl.BlockSpec((None, tm // TK, TK, IDX_DIM), chunked),
            pl.BlockSpec((None, tm // TQ, HEAD_DIM, N_HEADS * TQ), chunked),
            pl.BlockSpec((None, tm // TK, KV_DIM + V_ONES_ROWS, TK), chunked),
            pl.BlockSpec((None, IDX_HEADS * IDX_DIM, tm), tokt),
            pl.BlockSpec((None, IDX_HEADS, tm), tokt),
        ],
        out_shape=[
            jax.ShapeDtypeStruct((B, L, POOL_WIDTH), F32),
            jax.ShapeDtypeStruct((B, L // TK, TK, KV_DIM), BF16),
            jax.ShapeDtypeStruct((B, L // TK, TK, IDX_DIM), BF16),
            jax.ShapeDtypeStruct((B, L // TQ, HEAD_DIM, N_HEADS * TQ), BF16),
            jax.ShapeDtypeStruct((B, L // TK, KV_DIM + V_ONES_ROWS, TK), BF16),
            jax.ShapeDtypeStruct((B, IDX_HEADS * IDX_DIM, L), BF16),
            jax.ShapeDtypeStruct((B, IDX_HEADS, L), F32),
        ],
        compiler_params=pltpu.CompilerParams(
            dimension_semantics=("arbitrary", "arbitrary"),
            vmem_limit_bytes=V7X_VMEM_LIMIT_BYTES),
        name="in_proj",
    )(x, ln_g, ln_b, wu, wk, wik, wt)


BRACKET_PASSES = 14
NO_TIE = 1e9


def _rows_to_sublanes(x, op):
    return op(x.reshape(x.shape[0] // 8, 8, TQ), axis=0)


SCAN_CHUNKS = 2
SCAN_ROWS = SCAN_CHUNKS * TK


def _dsa_kernel(iqt_ref, iwt_ref, ik_ref, qw_ref, k_ref, va_ref, o_ref,
                sc_ref, thr_ref, tie_ref, m_ref, acc_ref, lga_ref, lgb_ref, *, seq_len, n_sel):
    assert (TQ // TK) % 2 == 0
    qi = pl.program_id(1)
    q0 = qi * TQ
    nkc = (qi + 1) * (TQ // TK)
    n_beyond = seq_len - (qi + 1) * TQ
    k_sel = jnp.float32(n_sel)

    row_iota = lax.broadcasted_iota(I32, (TK, TQ), 0)
    t_idx = q0 + lax.broadcasted_iota(I32, (TK, TQ), 1)
    scan_rows = lax.broadcasted_iota(I32, (SCAN_ROWS, TQ), 0)
    scan_t = q0 + lax.broadcasted_iota(I32, (SCAN_ROWS, TQ), 1)
    n_scan = nkc // SCAN_CHUNKS

    def scores_at(b):
        return sc_ref[pl.ds(b * SCAN_CHUNKS, SCAN_CHUNKS)].reshape(SCAN_ROWS, TQ)

    def score_step(b, carry):
        smin, smax = carry
        ikc = ik_ref[pl.ds(b * SCAN_CHUNKS, SCAN_CHUNKS)].reshape(SCAN_ROWS, IDX_DIM)
        score = jnp.zeros((SCAN_ROWS, TQ), F32)
        for h in range(IDX_HEADS):
            lg = _dot(ikc, iqt_ref[h * IDX_DIM:(h + 1) * IDX_DIM, :])
            score = score + iwt_ref[h:h + 1, :] * jnp.maximum(lg, 0.0)
        smin = jnp.minimum(smin, jnp.min(score, axis=0, keepdims=True))
        smax = jnp.maximum(smax, jnp.max(score, axis=0, keepdims=True))
        score = jnp.where(b * SCAN_ROWS + scan_rows <= scan_t, score, NEG_INF)
        sc_ref[pl.ds(b * SCAN_CHUNKS, SCAN_CHUNKS)] = score.reshape(SCAN_CHUNKS, TK, TQ)
        return smin, smax

    big = jnp.float32(3e38)
    smin, smax = lax.fori_loop(0, n_scan, score_step,
                               (jnp.full((1, TQ), big, F32), jnp.full((1, TQ), -big, F32)))

    nb_f = n_beyond.astype(F32)
    ninf = jnp.float32(-jnp.inf)

    def count_ge(cand):
        def body(b, acc):
            return acc + _rows_to_sublanes(jnp.where(scores_at(b) >= cand, 1.0, 0.0), jnp.sum)
        acc = lax.fori_loop(0, n_scan, body, jnp.zeros((8, TQ), F32))
        return jnp.sum(acc, axis=0, keepdims=True) + jnp.where(cand <= NEG_INF, nb_f, 0.0)

    def max_below(h):
        def body(b, acc):
            s = scores_at(b)
            return jnp.maximum(acc, _rows_to_sublanes(jnp.where(s < h, s, ninf), jnp.max))
        acc = lax.fori_loop(0, n_scan, body, jnp.full((8, TQ), ninf, F32))
        return jnp.max(acc, axis=0, keepdims=True)

    def count_ge_and_max_below(v):
        def body(b, carry):
            acc, mx = carry
            s = scores_at(b)
            acc = acc + _rows_to_sublanes(jnp.where(s >= v, 1.0, 0.0), jnp.sum)
            mx = jnp.maximum(mx, _rows_to_sublanes(jnp.where(s < v, s, ninf), jnp.max))
            return acc, mx
        acc, mx = lax.fori_loop(0, n_scan, body,
                                (jnp.zeros((8, TQ), F32), jnp.full((8, TQ), ninf, F32)))
        cnt = jnp.sum(acc, axis=0, keepdims=True) + jnp.where(v <= NEG_INF, nb_f, 0.0)
        return cnt, jnp.max(mx, axis=0, keepdims=True)

    n_adm = (q0 + 1 + lax.broadcasted_iota(I32, (1, TQ), 1)).astype(F32)
    few = n_adm < k_sel
    lo0 = jnp.where(few, NEG_INF, smin)
    clo0 = jnp.where(few, k_sel, jnp.where(smin <= NEG_INF, jnp.float32(seq_len), n_adm))
    hi0 = smax + (jnp.abs(smax) * 1e-6 + 1e-30)
    chi0 = jnp.zeros((1, TQ), F32)

    def bracket_body(it, st):
        lo, hi, clo, chi, flo, fhi, side = st
        done = clo == k_sel
        frac = jnp.clip(flo / (flo - fhi), 1.0 / 512, 511.0 / 512)
        frac = jnp.where(clo - chi <= 2.0, 0.5, frac)
        cand = lo + (hi - lo) * frac
        zero_inside = jnp.logical_and(jnp.logical_and(lo < 0.0, hi > 0.0), it == 0)
        cand = jnp.where(done, lo, jnp.where(zero_inside, 0.0, cand))
        cnt = count_ge(cand)
        ge = cnt >= k_sel
        f = cnt - (k_sel - 0.5)
        new_side = jnp.where(ge, 1.0, -1.0)
        same = new_side == side
        flo_n = jnp.where(ge, f, jnp.where(same, flo * 0.5, flo))
        fhi_n = jnp.where(ge, jnp.where(same, fhi * 0.5, fhi), f)
        up_lo = jnp.logical_and(jnp.logical_not(done), ge)
        up_hi = jnp.logical_and(jnp.logical_not(done), jnp.logical_not(ge))
        lo = jnp.where(up_lo, cand, lo)
        clo = jnp.where(up_lo, cnt, clo)
        hi = jnp.where(up_hi, cand, hi)
        chi = jnp.where(up_hi, cnt, chi)
        flo = jnp.where(done, flo, flo_n)
        fhi = jnp.where(done, fhi, fhi_n)
        side = jnp.where(done, side, new_side)
        return lo, hi, clo, chi, flo, fhi, side

    lo, hi, clo, chi = lax.fori_loop(
        0, BRACKET_PASSES, bracket_body,
        (lo0, hi0, clo0, chi0, clo0 - (k_sel - 0.5), chi0 - (k_sel - 0.5),
         jnp.zeros((1, TQ), F32)))[:4]
    pending = jnp.sum(jnp.where(clo == k_sel, 0.0, 1.0))
    thr_ref[...] = lo
    tie_ref[...] = jnp.full((1, TQ), NO_TIE, F32)

    @pl.when(pending > 0.5)
    def _():
        fin0 = jnp.where(clo == k_sel, 1.0, 0.0)
        v0 = max_below(hi)

        def fin_cond(st):
            return jnp.logical_and(st[0] < seq_len + 2, st[1] > 0.5)

        def fin_body(st):
            j, _, fin, h, ch, v, kst, need = st
            cnt, v2 = count_ge_and_max_below(v)
            hit = jnp.logical_and(fin < 0.5, cnt >= k_sel)
            kst = jnp.where(hit, v, kst)
            need = jnp.where(jnp.logical_and(hit, cnt > k_sel), k_sel - ch, need)
            fin = jnp.where(hit, 1.0, fin)
            open_ = fin < 0.5
            h = jnp.where(open_, v, h)
            ch = jnp.where(open_, cnt, ch)
            v = jnp.where(open_, v2, v)
            return j + 1, jnp.sum(1.0 - fin), fin, h, ch, v, kst, need

        st2 = lax.while_loop(
            fin_cond, fin_body,
            (jnp.int32(0), pending, fin0, hi, chi, v0, lo, jnp.full((1, TQ), NO_TIE, F32)))
        thr_ref[...] = st2[6]
        tie_ref[...] = st2[7]

    kstar = thr_ref[...]
    need = tie_ref[...]
    fix = jnp.logical_or(need < NO_TIE, kstar <= NEG_INF)

    @pl.when(jnp.sum(jnp.where(fix, 1.0, 0.0)) > 0.5)
    def _():
        tri = jnp.where(lax.broadcasted_iota(I32, (TK, TK), 0) >= lax.broadcasted_iota(I32, (TK, TK), 1),
                        1.0, 0.0).astype(BF16)

        def rewrite(c, seen):
            s = sc_ref[c]
            tied = s == kstar
            rank = seen + _dot(tri, jnp.where(tied, 1.0, 0.0).astype(BF16))
            take = jnp.logical_or(s > kstar, jnp.logical_and(tied, rank <= need))
            take = jnp.logical_and(take, c * TK + row_iota <= t_idx)
            sc_ref[c] = jnp.where(take, 1.0, -1.0)
            return rank[TK - 1:TK, :]

        lax.fori_loop(0, nkc, rewrite, jnp.zeros((1, TQ), F32))
        thr_ref[...] = jnp.zeros((1, TQ), F32)

    m_ref[...] = jnp.full(m_ref.shape, NEG_INF, F32)
    acc_ref[...] = jnp.zeros(acc_ref.shape, F32)
    thr = thr_ref[...]

    def masked_logits(c):
        bias = jnp.where(sc_ref[c] >= thr, 0.0, NEG_INF)
        lg = _dot(k_ref[c], qw_ref[...])
        lg = jnp.concatenate(
            [lg[:, h * TQ:(h + 1) * TQ] + bias for h in range(N_HEADS)], axis=1)
        return lg, jnp.max(lg, axis=0, keepdims=True)

    def attend(c, lg, lg_max):
        m_old = m_ref[...]
        m_new = jnp.maximum(m_old, lg_max)
        alpha = jnp.exp2(m_old - m_new)
        p = jnp.exp2(lg - m_new).astype(BF16)
        acc_ref[...] = alpha * acc_ref[...] + _dot(va_ref[c], p)
        m_ref[...] = m_new

    def attend_pair(i, max_a):
        c = 2 * i
        lg_b, max_b = masked_logits(c + 1)
        lgb_ref[...] = lg_b
        attend(c, lga_ref[...], max_a)
        lg_a, max_a = masked_logits(jnp.minimum(c + 2, nkc - 1))
        attend(c + 1, lgb_ref[...], max_b)
        lga_ref[...] = lg_a
        return max_a

    lg_a, max_a = masked_logits(0)
    lga_ref[...] = lg_a
    lax.fori_loop(0, nkc // 2, attend_pair, max_a)

    a = acc_ref[...]
    o = a[:HEAD_DIM] / a[HEAD_DIM:HEAD_DIM + 1]
    o = jnp.concatenate([o[:, h * TQ:(h + 1) * TQ] for h in range(N_HEADS)], axis=0)
    o_ref[...] = o.T


def _dsa_attention(iqt, iwt, ik, qw, k, vaug):
    B, _, L = iqt.shape
    n_sel = min(TOPK_MAX, L // 4)
    nch = L // TK
    va_rows = vaug.shape[2]
    grid = (B, L // TQ)
    kern = functools.partial(_dsa_kernel, seq_len=L, n_sel=n_sel)
    return pl.pallas_call(
        kern,
        grid=grid,
        in_specs=[
            pl.BlockSpec((None, IDX_HEADS * IDX_DIM, TQ), lambda b, i: (b, 0, i)),
            pl.BlockSpec((None, IDX_HEADS, TQ), lambda b, i: (b, 0, i)),
            pl.BlockSpec((None, nch, TK, IDX_DIM), lambda b, i: (b, 0, 0, 0)),
            pl.BlockSpec((None, None, HEAD_DIM, N_HEADS * TQ), lambda b, i: (b, i, 0, 0)),
            pl.BlockSpec((None, nch, TK, HEAD_DIM), lambda b, i: (b, 0, 0, 0)),
            pl.BlockSpec((None, nch, va_rows, TK), lambda b, i: (b, 0, 0, 0)),
        ],
        out_specs=pl.BlockSpec((None, TQ, ATTN_WIDTH), lambda b, i: (b, i, 0)),
        out_shape=jax.ShapeDtypeStruct((B, L, ATTN_WIDTH), F32),
        scratch_shapes=[
            pltpu.VMEM((nch, TK, TQ), F32),
            pltpu.VMEM((1, TQ), F32),
            pltpu.VMEM((1, TQ), F32),
            pltpu.VMEM((1, N_HEADS * TQ), F32),
            pltpu.VMEM((va_rows, N_HEADS * TQ), F32),
            pltpu.VMEM((TK, N_HEADS * TQ), F32),
            pltpu.VMEM((TK, N_HEADS * TQ), F32),
        ],
        compiler_params=pltpu.CompilerParams(
            dimension_semantics=("arbitrary", "arbitrary"),
            vmem_limit_bytes=V7X_VMEM_LIMIT_BYTES),
        name="dsa_attn",
    )(iqt, iwt, ik, qw, k, vaug)


def _merge_kernel(x_ref, u_ref, uh_ref, a_ref, lng_ref, lnb_ref, wgate_ref, bgate_ref,
                  poolw_ref, pscale_ref, wpp_ref, wpa_ref, wout_ref, ln1g_ref, ln1b_ref,
                  wrh_ref, wrl_ref, br_ref, h1_ref, gates_ref, *, tiles_per_seq):
    tm = TM_MERGE
    i = pl.program_id(0)
    seq_start = (i % tiles_per_seq) == 0
    h = _layer_norm(x_ref[...], lng_ref[...], lnb_ref[...])
    hb = h.astype(BF16)

    halo = jnp.where(seq_start, 0.0, uh_ref[...])
    u = u_ref[...]
    ext = jnp.concatenate([halo, u], axis=0)
    pos = (i % tiles_per_seq) * tm + lax.broadcasted_iota(I32, (tm, 1), 0)
    mixed = []
    for g, w in enumerate(POOL_WINDOWS):
        s = ext[:, g * POOL_GROUP_DIM:(g + 1) * POOL_GROUP_DIM]
        span = 1
        while span < w:
            s = s + pltpu.roll(s, span, 0)
            span *= 2
        win = s[POOL_HALO:]
        cnt = jnp.minimum(pos + 1, w).astype(F32)
        ug = u[:, g * POOL_GROUP_DIM:(g + 1) * POOL_GROUP_DIM]
        delta = win / cnt - ug
        mixed.append(_dot(delta.astype(BF16), poolw_ref[g]))
    pool_out = jnp.concatenate(mixed, axis=1) * pscale_ref[...]

    gate_pre = _dot(hb, wgate_ref[...]) + bgate_ref[...]
    gates = jax.nn.sigmoid(gate_pre)
    bp = _dot(pool_out.astype(BF16), wpp_ref[...])
    ba = _dot(a_ref[...].astype(BF16), wpa_ref[...])
    merged = gates[:, :D_MODEL] * bp + gates[:, D_MODEL:] * ba
    mix = _dot(merged.astype(BF16), wout_ref[...])
    h1 = _layer_norm(DEEPNORM_ALPHA * h + mix, ln1g_ref[...], ln1b_ref[...])
    h1_ref[...] = h1

    hi = h1.astype(BF16)
    lo = (h1 - hi.astype(F32)).astype(BF16)
    lg = _dot(hi, wrh_ref[...]) + (_dot(lo, wrh_ref[...]) + _dot(hi, wrl_ref[...])) + br_ref[...]
    lane = lax.broadcasted_iota(I32, (tm, LANES), 1).astype(F32)
    big = jnp.float32(1 << 20)
    ninf = jnp.float32(-jnp.inf)
    is_g = jnp.logical_and(lane >= N_EXPERTS, lane < N_EXPERTS + N_GROUPS)
    gl = jnp.where(is_g, lg, ninf)
    gmax = jnp.max(gl, axis=1, keepdims=True)
    gsel = jnp.min(jnp.where(gl == gmax, lane, big), axis=1, keepdims=True) - N_EXPERTS
    sumexp = jnp.sum(jnp.where(is_g, jnp.exp(gl - gmax), 0.0), axis=1, keepdims=True)
    p_group = 1.0 / sumexp
    e_lo = gsel * EXPERTS_PER_GROUP
    in_grp = jnp.logical_and(lane >= e_lo, lane < e_lo + EXPERTS_PER_GROUP)
    el = jnp.where(in_grp, lg, ninf)
    m1 = jnp.max(el, axis=1, keepdims=True)
    i1 = jnp.min(jnp.where(el == m1, lane, big), axis=1, keepdims=True)
    el2 = jnp.where(lane == i1, ninf, el)
    m2 = jnp.max(el2, axis=1, keepdims=True)
    i2 = jnp.min(jnp.where(el2 == m2, lane, big), axis=1, keepdims=True)
    e2 = jnp.exp(m2 - m1)
    den = 1.0 + e2
    w1 = (1.0 / den) * p_group
    w2 = (e2 / den) * p_group
    dense = jnp.where(lane == i1, w1, 0.0) + jnp.where(lane == i2, w2, 0.0)
    gates_ref[...] = jnp.where(lane == float(GSEL_LANE), gsel, dense)


def _merge(x, u, attn, ln_g, ln_b, wgate, bgate, poolw, pscale, wpp, wpa, wout, ln1g, ln1b,
           wrh, wrl, br, seq_len):
    T, D = x.shape
    tm = TM_MERGE
    tiles_per_seq = seq_len // tm
    grid = (T // tm,)
    tok = lambda i: (i, 0)
    c2 = lambda i: (0, 0)
    c3 = lambda i: (0, 0, 0)
    halo_blocks = tm // POOL_HALO
    kern = functools.partial(_merge_kernel, tiles_per_seq=tiles_per_seq)
    return pl.pallas_call(
        kern,
        grid=grid,
        in_specs=[
            pl.BlockSpec((tm, D), tok),
            pl.BlockSpec((tm, POOL_WIDTH), tok),
            pl.BlockSpec((POOL_HALO, POOL_WIDTH),
                         lambda i: (jnp.maximum(i * halo_blocks - 1, 0), 0)),
            pl.BlockSpec((tm, ATTN_WIDTH), tok),
            pl.BlockSpec((1, D), c2),
            pl.BlockSpec((1, D), c2),
            pl.BlockSpec((D, N_BRANCHES * D), c2),
            pl.BlockSpec((1, N_BRANCHES * D), c2),
            pl.BlockSpec((POOL_GROUPS, POOL_GROUP_DIM, POOL_GROUP_DIM), c3),
            pl.BlockSpec((1, POOL_WIDTH), c2),
            pl.BlockSpec((POOL_WIDTH, D), c2),
            pl.BlockSpec((ATTN_WIDTH, D), c2),
            pl.BlockSpec((D, D), c2),
            pl.BlockSpec((1, D), c2),
            pl.BlockSpec((1, D), c2),
            pl.BlockSpec((D, LANES), c2),
            pl.BlockSpec((D, LANES), c2),
            pl.BlockSpec((1, LANES), c2),
        ],
        out_specs=[
            pl.BlockSpec((tm, D), tok),
            pl.BlockSpec((tm, LANES), tok),
        ],
        out_shape=[
            jax.ShapeDtypeStruct((T, D), F32),
            jax.ShapeDtypeStruct((T, LANES), F32),
        ],
        compiler_params=pltpu.CompilerParams(
            dimension_semantics=("arbitrary",),
            vmem_limit_bytes=V7X_VMEM_LIMIT_BYTES),
        name="merge",
    )(x, u, u, attn, ln_g, ln_b, wgate, bgate, poolw, pscale, wpp, wpa, wout, ln1g, ln1b,
      wrh, wrl, br)


def _moe_kernel(h1_ref, g_ref, tri_ref, wg_ref, wu_ref, wd_ref, ln2g_ref, ln2b_ref, o_ref,
                xs_ref, yh_ref, yl_ref, dcol_ref, tab_ref):
    tm = TM_MOE
    g = pl.program_id(1)
    lane = lax.broadcasted_iota(I32, (tm, LANES), 1).astype(F32)
    lane1 = lax.broadcasted_iota(I32, (1, LANES), 1).astype(F32)

    @pl.when(g == 0)
    def _():
        rec = g_ref[...]
        gsel = jnp.sum(jnp.where(lane == float(GSEL_LANE), rec, 0.0), axis=1, keepdims=True)
        onehot = jnp.where(lane == gsel, 1.0, 0.0)
        cum = _dot(tri_ref[...], onehot.astype(BF16))
        cnt = cum[tm - 1:tm, :]
        padded = jnp.floor((cnt + (MOE_CHUNK - 1)) * (1.0 / MOE_CHUNK)) * MOE_CHUNK
        start = jnp.zeros((1, LANES), F32)
        for gg in range(N_GROUPS - 1):
            p = jnp.sum(jnp.where(lane1 == float(gg), padded, 0.0), axis=1, keepdims=True)
            start = start + jnp.where(lane1 > float(gg), p, 0.0)
        tab_ref[0:1, :] = start
        tab_ref[1:2, :] = padded
        slot = jnp.sum(onehot * (start + cum - 1.0), axis=1, keepdims=True)
        slot_b = jnp.broadcast_to(slot, (tm, LANES))
        dcol_ref[...] = slot_b
        slot_row = slot_b.T[0:1, :]

        dense = jnp.where(lane < float(N_EXPERTS), rec, 0.0)
        own = jnp.where(gsel == 0.0, dense, 0.0)
        for gg in range(1, N_GROUPS):
            own = own + jnp.where(gsel == float(gg),
                                  pltpu.roll(dense, LANES - gg * EXPERTS_PER_GROUP, 1), 0.0)
        own = jnp.where(lane < float(EXPERTS_PER_GROUP), own, 0.0)
        own_hi = own.astype(BF16).astype(F32)
        side = (own_hi + pltpu.roll(own - own_hi, EXPERTS_PER_GROUP, 1)).astype(BF16)
        xa = jnp.concatenate([h1_ref[...].astype(BF16), side], axis=1)

        blk = 256
        for rb in range(MOE_SLOTS // blk):
            rows = rb * blk + lax.broadcasted_iota(I32, (blk, tm), 0).astype(F32)
            perm = jnp.where(rows == slot_row, 1.0, 0.0).astype(BF16)
            xs_ref[rb * blk:(rb + 1) * blk, :] = _dot(perm, xa).astype(BF16)
        yh_ref[...] = jnp.zeros(yh_ref.shape, BF16)
        yl_ref[...] = jnp.zeros(yl_ref.shape, BF16)

    gf = g.astype(F32)
    first = jnp.sum(jnp.where(lane1 == gf, tab_ref[0:1, :], 0.0)).astype(I32)
    n_chunks = jnp.sum(jnp.where(lane1 == gf, tab_ref[1:2, :], 0.0)).astype(I32) // MOE_CHUNK
    ff = EXPERTS_PER_GROUP * EXPERT_FF

    def ffn_chunk(c, carry):
        r0 = pl.multiple_of(first + c * MOE_CHUNK, MOE_CHUNK)
        xc = xs_ref[pl.ds(r0, MOE_CHUNK), :]
        xb = xc[:, :D_MODEL]
        gl = xc[:, D_MODEL:].astype(F32)
        gates = gl[:, :EXPERTS_PER_GROUP] + gl[:, EXPERTS_PER_GROUP:2 * EXPERTS_PER_GROUP]
        parts = []
        for j in range(EXPERTS_PER_GROUP):
            hid = jax.nn.silu(_dot(xb, wg_ref[j])) * _dot(xb, wu_ref[j])
            parts.append((hid * gates[:, j:j + 1]).astype(BF16))
        y = _dot(jnp.concatenate(parts, axis=1), wd_ref[...].reshape(ff, D_MODEL))
        y_hi = y.astype(BF16)
        yh_ref[pl.ds(r0, MOE_CHUNK), :] = y_hi
        yl_ref[pl.ds(r0, MOE_CHUNK), :] = (y - y_hi.astype(F32)).astype(BF16)
        return carry

    lax.fori_loop(0, n_chunks, ffn_chunk, 0)

    @pl.when(g == N_GROUPS - 1)
    def _():
        slots = lax.broadcasted_iota(I32, (tm, MOE_SLOTS), 1).astype(F32)
        back = jnp.where(slots == dcol_ref[:, 0:1], 1.0, 0.0).astype(BF16)
        ffn = _dot(back, yh_ref[...]) + _dot(back, yl_ref[...])
        o_ref[...] = _layer_norm(DEEPNORM_ALPHA * h1_ref[...] + ffn, ln2g_ref[...], ln2b_ref[...])


def _moe(h1, route, wg, wu, wd, ln2g, ln2b):
    T, D = h1.shape
    tm = TM_MOE
    epg = EXPERTS_PER_GROUP
    grid = (T // tm, N_GROUPS)
    return pl.pallas_call(
        _moe_kernel,
        grid=grid,
        in_specs=[
            pl.BlockSpec((tm, D), lambda i, g: (i, 0)),
            pl.BlockSpec((tm, LANES), lambda i, g: (i, 0)),
            pl.BlockSpec((tm, tm), lambda i, g: (0, 0)),
            pl.BlockSpec((epg, D, EXPERT_FF), lambda i, g: (g, 0, 0)),
            pl.BlockSpec((epg, D, EXPERT_FF), lambda i, g: (g, 0, 0)),
            pl.BlockSpec((epg, EXPERT_FF, D), lambda i, g: (g, 0, 0)),
            pl.BlockSpec((1, D), lambda i, g: (0, 0)),
            pl.BlockSpec((1, D), lambda i, g: (0, 0)),
        ],
        out_specs=pl.BlockSpec((tm, D), lambda i, g: (i, 0)),
        out_shape=jax.ShapeDtypeStruct((T, D), F32),
        scratch_shapes=[
            pltpu.VMEM((MOE_SLOTS, D + LANES), BF16),
            pltpu.VMEM((MOE_SLOTS, D), BF16),
            pltpu.VMEM((MOE_SLOTS, D), BF16),
            pltpu.VMEM((tm, LANES), F32),
            pltpu.VMEM((8, LANES), F32),
        ],
        compiler_params=pltpu.CompilerParams(
            dimension_semantics=("arbitrary", "arbitrary"),
            vmem_limit_bytes=V7X_VMEM_LIMIT_BYTES),
        name="moe",
    )(h1, route, jnp.tril(jnp.ones((tm, tm), BF16)), wg, wu, wd, ln2g, ln2b)


def kernel(x, ln_in_g, ln_in_b, w_in, b_gate, pool_w, pool_scale, w_proj_pool, w_proj_attn, w_out,
           ln1_g, ln1_b, w_group, b_group, w_router, b_router, w_gate, w_up, w_down, ln2_g, ln2_b):
    B, L, D = x.shape
    assert D == D_MODEL and w_in.shape[0] == DEPTH == 1
    assert L % TQ == 0 and L % TM_MERGE == 0 and L % TM_PROJ == 0
    assert L <= 4096
    T = B * L
    row = lambda v: v.reshape(1, -1).astype(F32)

    w = w_in[0]
    o = 0
    w_u = w[:, o:o + POOL_WIDTH]; o += POOL_WIDTH
    w_q = w[:, o:o + ATTN_WIDTH]; o += ATTN_WIDTH
    w_k = w[:, o:o + KV_DIM]; o += KV_DIM
    w_v = w[:, o:o + KV_DIM]; o += KV_DIM
    w_iq = w[:, o:o + IDX_HEADS * IDX_DIM]; o += IDX_HEADS * IDX_DIM
    w_ik = w[:, o:o + IDX_DIM]; o += IDX_DIM
    w_iw = w[:, o:o + IDX_HEADS]; o += IDX_HEADS
    w_g = w[:, o:]
    sm_scale = math.log2(math.e) / math.sqrt(HEAD_DIM)
    bf16_rows = 16
    w_t = jnp.concatenate([w_q * sm_scale, w_v, w_iq, w_iw,
                           jnp.zeros((D, bf16_rows - IDX_HEADS), F32)], axis=1).T.astype(BF16)

    u, k, ik, qw, vaug, iqt, iwt = _in_proj(
        x, row(ln_in_g), row(ln_in_b), w_u.astype(BF16), w_k.astype(BF16), w_ik.astype(BF16), w_t)
    attn = _dsa_attention(iqt, iwt, ik, qw, k, vaug)

    w_r = jnp.zeros((D, LANES), F32)
    w_r = w_r.at[:, :N_EXPERTS].set(w_router[0]).at[:, N_EXPERTS:N_EXPERTS + N_GROUPS].set(w_group[0])
    b_r = jnp.zeros((1, LANES), F32)
    b_r = b_r.at[0, :N_EXPERTS].set(b_router[0]).at[0, N_EXPERTS:N_EXPERTS + N_GROUPS].set(b_group[0])
    w_rh = w_r.astype(BF16)
    w_rl = (w_r - w_rh.astype(F32)).astype(BF16)

    h1, gates = _merge(
        x.reshape(T, D), u.reshape(T, POOL_WIDTH), attn.reshape(T, ATTN_WIDTH),
        row(ln_in_g), row(ln_in_b), w_g.astype(BF16), row(b_gate[0]),
        pool_w[0].astype(BF16), row(pool_scale[0]), w_proj_pool[0].astype(BF16),
        w_proj_attn[0].astype(BF16), w_out[0].astype(BF16), row(ln1_g[0]), row(ln1_b[0]),
        w_rh, w_rl, b_r, L)

    out = _moe(h1, gates, w_gate[0].astype(BF16), w_up[0].astype(BF16), w_down[0].astype(BF16),
               row(ln2_g[0]), row(ln2_b[0]))
    return out.reshape(B, L, D)
```

```python
import functools
import math

import jax
import jax.numpy as jnp
import numpy as np
from jax import lax
from jax.experimental import pallas as pl
from jax.experimental.pallas import tpu as pltpu

D_MODEL = 1024
POOL_WINDOWS = (2, 4, 8, 16)
POOL_GROUPS = 4
POOL_WIDTH = 512
POOL_GROUP_DIM = 128
N_HEADS = 8
HEAD_DIM = 64
ATTN_WIDTH = 512
KV_DIM = 64
IDX_HEADS = 8
IDX_DIM = 32
TOPK_MAX = 256
NEG_INF = float(np.float32(-1e30))
N_BRANCHES = 2
N_GROUPS = 4
EXPERTS_PER_GROUP = 8
N_EXPERTS = 32
EXPERT_FF = 256
LN_EPS = 1e-5
DEPTH = 1
DEEPNORM_ALPHA = (2.0 * DEPTH) ** 0.25

V7X_VMEM_LIMIT_BYTES = 56 * 1024 * 1024
V7X_VMEM_LIMIT_BYTES_MOE = 60 * 1024 * 1024
LANES = 128

F32 = jnp.float32
BF16 = jnp.bfloat16
I32 = jnp.int32
INT_MIN = -2 ** 31
INT_MAX = 2 ** 31 - 1

TM_PROJ = 512
TQ = 256
TK = 128
TM_MERGE = 512
POOL_HALO = 16
V_ONES_ROWS = 16
TM_MOE = 1024
MOE_CHUNK = 128
MOE_SLOTS = TM_MOE + N_GROUPS * MOE_CHUNK
GSEL_LANE = N_EXPERTS


def _layer_norm(x, g, b):
    mu = jnp.mean(x, axis=-1, keepdims=True)
    xc = x - mu
    var = jnp.mean(xc * xc, axis=-1, keepdims=True)
    return xc * lax.rsqrt(var + LN_EPS) * g + b


def _dot(a, b):
    return jnp.dot(a, b, preferred_element_type=F32)


def _dot_nt(a, b):
    return lax.dot_general(a, b, (((1,), (1,)), ((), ())), preferred_element_type=F32)


def _in_proj_kernel(x_ref, g_ref, b_ref, wu_ref, wk_ref, wik_ref, wt_ref,
                    u_ref, k_ref, ik_ref, qt_ref, vt_ref, iqt_ref, iwt_ref):
    h = _layer_norm(x_ref[...], g_ref[...], b_ref[...])
    hb = h.astype(BF16)
    u_ref[...] = _dot(hb, wu_ref[...])
    n_chunks = TM_PROJ // TK
    k_ref[...] = _dot(hb, wk_ref[...]).astype(BF16).reshape(n_chunks, TK, KV_DIM)
    ik_ref[...] = _dot(hb, wik_ref[...]).astype(BF16).reshape(n_chunks, TK, IDX_DIM)
    pt = _dot_nt(wt_ref[...], hb)
    r0 = 0
    for j in range(TM_PROJ // TQ):
        for h in range(N_HEADS):
            qt_ref[j, :, h * TQ:(h + 1) * TQ] = pt[r0 + h * HEAD_DIM:r0 + (h + 1) * HEAD_DIM,
                                                   j * TQ:(j + 1) * TQ].astype(BF16)
    r0 += ATTN_WIDTH
    for c in range(n_chunks):
        vt_ref[c, :KV_DIM, :] = pt[r0:r0 + KV_DIM, c * TK:(c + 1) * TK].astype(BF16)
        vt_ref[c, KV_DIM:, :] = jnp.ones((V_ONES_ROWS, TK), BF16)
    r0 += KV_DIM
    iqt_ref[...] = pt[r0:r0 + IDX_HEADS * IDX_DIM].astype(BF16)
    r0 += IDX_HEADS * IDX_DIM
    iwt_ref[...] = pt[r0:r0 + IDX_HEADS]


def _in_proj(x, ln_g, ln_b, wu, wk, wik, wt):
    B, L, D = x.shape
    tm = TM_PROJ
    grid = (B, L // tm)
    tok = lambda b, i: (b, i, 0)
    tokt = lambda b, i: (b, 0, i)
    chunked = lambda b, i: (b, i, 0, 0)
    const2 = lambda b, i: (0, 0)
    n_t = wt.shape[0]
    return pl.pallas_call(
        _in_proj_kernel,
        grid=grid,
        in_specs=[
            pl.BlockSpec((None, tm, D), tok),
            pl.BlockSpec((1, D), const2),
            pl.BlockSpec((1, D), const2),
            pl.BlockSpec((D, POOL_WIDTH), const2),
            pl.BlockSpec((D, KV_DIM), const2),
            pl.BlockSpec((D, IDX_DIM), const2),
            pl.BlockSpec((n_t, D), const2),
        ],
        out_specs=[
            pl.BlockSpec((None, tm, POOL_WIDTH), tok),
            pl.BlockSpec((None, tm // TK, TK, KV_DIM), chunked),
            pl.BlockSpec((None, tm // TK, TK, IDX_DIM), chunked),
            pl.BlockSpec((None, tm // TQ, HEAD_DIM, N_HEADS * TQ), chunked),
            pl.BlockSpec((None, tm // TK, KV_DIM + V_ONES_ROWS, TK), chunked),
            pl.BlockSpec((None, IDX_HEADS * IDX_DIM, tm), tokt),
            pl.BlockSpec((None, IDX_HEADS, tm), tokt),
        ],
        out_shape=[
            jax.ShapeDtypeStruct((B, L, POOL_WIDTH), F32),
            jax.ShapeDtypeStruct((B, L // TK, TK, KV_DIM), BF16),
            jax.ShapeDtypeStruct((B, L // TK, TK, IDX_DIM), BF16),
            jax.ShapeDtypeStruct((B, L // TQ, HEAD_DIM, N_HEADS * TQ), BF16),
            jax.ShapeDtypeStruct((B, L // TK, KV_DIM + V_ONES_ROWS, TK), BF16),
            jax.ShapeDtypeStruct((B, IDX_HEADS * IDX_DIM, L), BF16),
            jax.ShapeDtypeStruct((B, IDX_HEADS, L), F32),
        ],
        compiler_params=pltpu.CompilerParams(
            dimension_semantics=("arbitrary", "arbitrary"),
            vmem_limit_bytes=V7X_VMEM_LIMIT_BYTES),
        name="in_proj",
    )(x, ln_g, ln_b, wu, wk, wik, wt)


BRACKET_PASSES = 14
NO_TIE = 1e9


def _rows_to_sublanes(x, op):
    return op(x.reshape(x.shape[0] // 8, 8, TQ), axis=0)


SCAN_CHUNKS = 2
SCAN_ROWS = SCAN_CHUNKS * TK


def _dsa_kernel(iqt_ref, iwt_ref, ik_ref, qw_ref, k_ref, va_ref, o_ref,
                sc_ref, thr_ref, tie_ref, m_ref, acc_ref, lga_ref, lgb_ref, *, seq_len, n_sel):
    assert (TQ // TK) % 2 == 0
    qi = pl.program_id(1)
    q0 = qi * TQ
    nkc = (qi + 1) * (TQ // TK)
    n_beyond = seq_len - (qi + 1) * TQ
    k_sel = jnp.float32(n_sel)

    row_iota = lax.broadcasted_iota(I32, (TK, TQ), 0)
    t_idx = q0 + lax.broadcasted_iota(I32, (TK, TQ), 1)
    scan_rows = lax.broadcasted_iota(I32, (SCAN_ROWS, TQ), 0)
    scan_t = q0 + lax.broadcasted_iota(I32, (SCAN_ROWS, TQ), 1)
    n_scan = nkc // SCAN_CHUNKS

    def scores_at(b):
        return sc_ref[pl.ds(b * SCAN_CHUNKS, SCAN_CHUNKS)].reshape(SCAN_ROWS, TQ)

    def score_step(b, carry):
        smin, smax = carry
        ikc = ik_ref[pl.ds(b * SCAN_CHUNKS, SCAN_CHUNKS)].reshape(SCAN_ROWS, IDX_DIM)
        score = jnp.zeros((SCAN_ROWS, TQ), F32)
        for h in range(IDX_HEADS):
            lg = _dot(ikc, iqt_ref[h * IDX_DIM:(h + 1) * IDX_DIM, :])
            score = score + iwt_ref[h:h + 1, :] * jnp.maximum(lg, 0.0)
        smin = jnp.minimum(smin, jnp.min(score, axis=0, keepdims=True))
        smax = jnp.maximum(smax, jnp.max(score, axis=0, keepdims=True))
        score = jnp.where(b * SCAN_ROWS + scan_rows <= scan_t, score, NEG_INF)
        sc_ref[pl.ds(b * SCAN_CHUNKS, SCAN_CHUNKS)] = score.reshape(SCAN_CHUNKS, TK, TQ)
        return smin, smax

    big = jnp.float32(3e38)
    smin, smax = lax.fori_loop(0, n_scan, score_step,
                               (jnp.full((1, TQ), big, F32), jnp.full((1, TQ), -big, F32)))

    nb_f = n_beyond.astype(F32)
    ninf = jnp.float32(-jnp.inf)

    def count_ge(cand):
        def body(b, acc):
            return acc + _rows_to_sublanes(jnp.where(scores_at(b) >= cand, 1.0, 0.0), jnp.sum)
        acc = lax.fori_loop(0, n_scan, body, jnp.zeros((8, TQ), F32))
        return jnp.sum(acc, axis=0, keepdims=True) + jnp.where(cand <= NEG_INF, nb_f, 0.0)

    def max_below(h):
        def body(b, acc):
            s = scores_at(b)
            return jnp.maximum(acc, _rows_to_sublanes(jnp.where(s < h, s, ninf), jnp.max))
        acc = lax.fori_loop(0, n_scan, body, jnp.full((8, TQ), ninf, F32))
        return jnp.max(acc, axis=0, keepdims=True)

    def count_ge_and_max_below(v):
        def body(b, carry):
            acc, mx = carry
            s = scores_at(b)
            acc = acc + _rows_to_sublanes(jnp.where(s >= v, 1.0, 0.0), jnp.sum)
            mx = jnp.maximum(mx, _rows_to_sublanes(jnp.where(s < v, s, ninf), jnp.max))
            return acc, mx
        acc, mx = lax.fori_loop(0, n_scan, body,
                                (jnp.zeros((8, TQ), F32), jnp.full((8, TQ), ninf, F32)))
        cnt = jnp.sum(acc, axis=0, keepdims=True) + jnp.where(v <= NEG_INF, nb_f, 0.0)
        return cnt, jnp.max(mx, axis=0, keepdims=True)

    n_adm = (q0 + 1 + lax.broadcasted_iota(I32, (1, TQ), 1)).astype(F32)
    few = n_adm < k_sel
    lo0 = jnp.where(few, NEG_INF, smin)
    clo0 = jnp.where(few, k_sel, jnp.where(smin <= NEG_INF, jnp.float32(seq_len), n_adm))
    hi0 = smax + (jnp.abs(smax) * 1e-6 + 1e-30)
    chi0 = jnp.zeros((1, TQ), F32)

    def bracket_body(it, st):
        lo, hi, clo, chi, flo, fhi, side = st
        done = clo == k_sel
        frac = jnp.clip(flo / (flo - fhi), 1.0 / 512, 511.0 / 512)
        frac = jnp.where(clo - chi <= 2.0, 0.5, frac)
        cand = lo + (hi - lo) * frac
        zero_inside = jnp.logical_and(jnp.logical_and(lo < 0.0, hi > 0.0), it == 0)
        cand = jnp.where(done, lo, jnp.where(zero_inside, 0.0, cand))
        cnt = count_ge(cand)
        ge = cnt >= k_sel
        f = cnt - (k_sel - 0.5)
        new_side = jnp.where(ge, 1.0, -1.0)
        same = new_side == side
        flo_n = jnp.where(ge, f, jnp.where(same, flo * 0.5, flo))
        fhi_n = jnp.where(ge, jnp.where(same, fhi * 0.5, fhi), f)
        up_lo = jnp.logical_and(jnp.logical_not(done), ge)
        up_hi = jnp.logical_and(jnp.logical_not(done), jnp.logical_not(ge))
        lo = jnp.where(up_lo, cand, lo)
        clo = jnp.where(up_lo, cnt, clo)
        hi = jnp.where(up_hi, cand, hi)
        chi = jnp.where(up_hi, cnt, chi)
        flo = jnp.where(done, flo, flo_n)
        fhi = jnp.where(done, fhi, fhi_n)
        side = jnp.where(done, side, new_side)
        return lo, hi, clo, chi, flo, fhi, side

    lo, hi, clo, chi = lax.fori_loop(
        0, BRACKET_PASSES, bracket_body,
        (lo0, hi0, clo0, chi0, clo0 - (k_sel - 0.5), chi0 - (k_sel - 0.5),
         jnp.zeros((1, TQ), F32)))[:4]
    pending = jnp.sum(jnp.where(clo == k_sel, 0.0, 1.0))
    thr_ref[...] = lo
    tie_ref[...] = jnp.full((1, TQ), NO_TIE, F32)

    @pl.when(pending > 0.5)
    def _():
        fin0 = jnp.where(clo == k_sel, 1.0, 0.0)
        v0 = max_below(hi)

        def fin_cond(st):
            return jnp.logical_and(st[0] < seq_len + 2, st[1] > 0.5)

        def fin_body(st):
            j, _, fin, h, ch, v, kst, need = st
            cnt, v2 = count_ge_and_max_below(v)
            hit = jnp.logical_and(fin < 0.5, cnt >= k_sel)
            kst = jnp.where(hit, v, kst)
            need = jnp.where(jnp.logical_and(hit, cnt > k_sel), k_sel - ch, need)
            fin = jnp.where(hit, 1.0, fin)
            open_ = fin < 0.5
            h = jnp.where(open_, v, h)
            ch = jnp.where(open_, cnt, ch)
            v = jnp.where(open_, v2, v)
            return j + 1, jnp.sum(1.0 - fin), fin, h, ch, v, kst, need

        st2 = lax.while_loop(
            fin_cond, fin_body,
            (jnp.int32(0), pending, fin0, hi, chi, v0, lo, jnp.full((1, TQ), NO_TIE, F32)))
        thr_ref[...] = st2[6]
        tie_ref[...] = st2[7]

    kstar = thr_ref[...]
    need = tie_ref[...]
    fix = jnp.logical_or(need < NO_TIE, kstar <= NEG_INF)

    @pl.when(jnp.sum(jnp.where(fix, 1.0, 0.0)) > 0.5)
    def _():
        tri = jnp.where(lax.broadcasted_iota(I32, (TK, TK), 0) >= lax.broadcasted_iota(I32, (TK, TK), 1),
                        1.0, 0.0).astype(BF16)

        def rewrite(c, seen):
            s = sc_ref[c]
            tied = s == kstar
            rank = seen + _dot(tri, jnp.where(tied, 1.0, 0.0).astype(BF16))
            take = jnp.logical_or(s > kstar, jnp.logical_and(tied, rank <= need))
            take = jnp.logical_and(take, c * TK + row_iota <= t_idx)
            sc_ref[c] = jnp.where(take, 1.0, -1.0)
            return rank[TK - 1:TK, :]

        lax.fori_loop(0, nkc, rewrite, jnp.zeros((1, TQ), F32))
        thr_ref[...] = jnp.zeros((1, TQ), F32)

    m_ref[...] = jnp.full(m_ref.shape, NEG_INF, F32)
    acc_ref[...] = jnp.zeros(acc_ref.shape, F32)
    thr = thr_ref[...]

    def masked_logits(c):
        bias = jnp.where(sc_ref[c] >= thr, 0.0, NEG_INF)
        lg = _dot(k_ref[c], qw_ref[...])
        lg = jnp.concatenate(
            [lg[:, h * TQ:(h + 1) * TQ] + bias for h in range(N_HEADS)], axis=1)
        return lg, jnp.max(lg, axis=0, keepdims=True)

    def attend(c, lg, lg_max):
        m_old = m_ref[...]
        m_new = jnp.maximum(m_old, lg_max)
        alpha = jnp.exp2(m_old - m_new)
        p = jnp.exp2(lg - m_new).astype(BF16)
        acc_ref[...] = alpha * acc_ref[...] + _dot(va_ref[c], p)
        m_ref[...] = m_new

    def attend_pair(i, max_a):
        c = 2 * i
        lg_b, max_b = masked_logits(c + 1)
        lgb_ref[...] = lg_b
        attend(c, lga_ref[...], max_a)
        lg_a, max_a = masked_logits(jnp.minimum(c + 2, nkc - 1))
        attend(c + 1, lgb_ref[...], max_b)
        lga_ref[...] = lg_a
        return max_a

    lg_a, max_a = masked_logits(0)
    lga_ref[...] = lg_a
    lax.fori_loop(0, nkc // 2, attend_pair, max_a)

    a = acc_ref[...]
    o = a[:HEAD_DIM] / a[HEAD_DIM:HEAD_DIM + 1]
    o = jnp.concatenate([o[:, h * TQ:(h + 1) * TQ] for h in range(N_HEADS)], axis=0)
    o_ref[...] = o.T


def _dsa_attention(iqt, iwt, ik, qw, k, vaug):
    B, _, L = iqt.shape
    n_sel = min(TOPK_MAX, L // 4)
    nch = L // TK
    va_rows = vaug.shape[2]
    grid = (B, L // TQ)
    kern = functools.partial(_dsa_kernel, seq_len=L, n_sel=n_sel)
    return pl.pallas_call(
        kern,
        grid=grid,
        in_specs=[
            pl.BlockSpec((None, IDX_HEADS * IDX_DIM, TQ), lambda b, i: (b, 0, i)),
            pl.BlockSpec((None, IDX_HEADS, TQ), lambda b, i: (b, 0, i)),
            pl.BlockSpec((None, nch, TK, IDX_DIM), lambda b, i: (b, 0, 0, 0)),
            pl.BlockSpec((None, None, HEAD_DIM, N_HEADS * TQ), lambda b, i: (b, i, 0, 0)),
            pl.BlockSpec((None, nch, TK, HEAD_DIM), lambda b, i: (b, 0, 0, 0)),
            pl.BlockSpec((None, nch, va_rows, TK), lambda b, i: (b, 0, 0, 0)),
        ],
        out_specs=pl.BlockSpec((None, TQ, ATTN_WIDTH), lambda b, i: (b, i, 0)),
        out_shape=jax.ShapeDtypeStruct((B, L, ATTN_WIDTH), F32),
        scratch_shapes=[
            pltpu.VMEM((nch, TK, TQ), F32),
            pltpu.VMEM((1, TQ), F32),
            pltpu.VMEM((1, TQ), F32),
            pltpu.VMEM((1, N_HEADS * TQ), F32),
            pltpu.VMEM((va_rows, N_HEADS * TQ), F32),
            pltpu.VMEM((TK, N_HEADS * TQ), F32),
            pltpu.VMEM((TK, N_HEADS * TQ), F32),
        ],
        compiler_params=pltpu.CompilerParams(
            dimension_semantics=("arbitrary", "arbitrary"),
            vmem_limit_bytes=V7X_VMEM_LIMIT_BYTES),
        name="dsa_attn",
    )(iqt, iwt, ik, qw, k, vaug)


def _merge_kernel(x_ref, u_ref, uh_ref, a_ref, lng_ref, lnb_ref, wgate_ref, bgate_ref,
                  poolw_ref, pscale_ref, wpp_ref, wpa_ref, wout_ref, ln1g_ref, ln1b_ref,
                  wrh_ref, wrl_ref, br_ref, h1_ref, gates_ref, *, tiles_per_seq):
    tm = TM_MERGE
    i = pl.program_id(0)
    seq_start = (i % tiles_per_seq) == 0
    h = _layer_norm(x_ref[...], lng_ref[...], lnb_ref[...])
    hb = h.astype(BF16)

    halo = jnp.where(seq_start, 0.0, uh_ref[...])
    u = u_ref[...]
    ext = jnp.concatenate([halo, u], axis=0)
    pos = (i % tiles_per_seq) * tm + lax.broadcasted_iota(I32, (tm, 1), 0)
    mixed = []
    for g, w in enumerate(POOL_WINDOWS):
        s = ext[:, g * POOL_GROUP_DIM:(g + 1) * POOL_GROUP_DIM]
        span = 1
        while span < w:
            s = s + pltpu.roll(s, span, 0)
            span *= 2
        win = s[POOL_HALO:]
        cnt = jnp.minimum(pos + 1, w).astype(F32)
        ug = u[:, g * POOL_GROUP_DIM:(g + 1) * POOL_GROUP_DIM]
        delta = win / cnt - ug
        mixed.append(_dot(delta.astype(BF16), poolw_ref[g]))
    pool_out = jnp.concatenate(mixed, axis=1) * pscale_ref[...]

    gate_pre = _dot(hb, wgate_ref[...]) + bgate_ref[...]
    gates = jax.nn.sigmoid(gate_pre)
    bp = _dot(pool_out.astype(BF16), wpp_ref[...])
    ba = _dot(a_ref[...].astype(BF16), wpa_ref[...])
    merged = gates[:, :D_MODEL] * bp + gates[:, D_MODEL:] * ba
    mix = _dot(merged.astype(BF16), wout_ref[...])
    h1 = _layer_norm(DEEPNORM_ALPHA * h + mix, ln1g_ref[...], ln1b_ref[...])
    h1_ref[...] = h1

    hi = h1.astype(BF16)
    lo = (h1 - hi.astype(F32)).astype(BF16)
    lg = _dot(hi, wrh_ref[...]) + (_dot(lo, wrh_ref[...]) + _dot(hi, wrl_ref[...])) + br_ref[...]
    lane = lax.broadcasted_iota(I32, (tm, LANES), 1).astype(F32)
    big = jnp.float32(1 << 20)
    ninf = jnp.float32(-jnp.inf)
    is_g = jnp.logical_and(lane >= N_EXPERTS, lane < N_EXPERTS + N_GROUPS)
    gl = jnp.where(is_g, lg, ninf)
    gmax = jnp.max(gl, axis=1, keepdims=True)
    gsel = jnp.min(jnp.where(gl == gmax, lane, big), axis=1, keepdims=True) - N_EXPERTS
    sumexp = jnp.sum(jnp.where(is_g, jnp.exp(gl - gmax), 0.0), axis=1, keepdims=True)
    p_group = 1.0 / sumexp
    e_lo = gsel * EXPERTS_PER_GROUP
    in_grp = jnp.logical_and(lane >= e_lo, lane < e_lo + EXPERTS_PER_GROUP)
    el = jnp.where(in_grp, lg, ninf)
    m1 = jnp.max(el, axis=1, keepdims=True)
    i1 = jnp.min(jnp.where(el == m1, lane, big), axis=1, keepdims=True)
    el2 = jnp.where(lane == i1, ninf, el)
    m2 = jnp.max(el2, axis=1, keepdims=True)
    i2 = jnp.min(jnp.where(el2 == m2, lane, big), axis=1, keepdims=True)
    e2 = jnp.exp(m2 - m1)
    den = 1.0 + e2
    w1 = (1.0 / den) * p_group
    w2 = (e2 / den) * p_group
    dense = jnp.where(lane == i1, w1, 0.0) + jnp.where(lane == i2, w2, 0.0)
    gates_ref[...] = jnp.where(lane == float(GSEL_LANE), gsel, dense)


def _merge(x, u, attn, ln_g, ln_b, wgate, bgate, poolw, pscale, wpp, wpa, wout, ln1g, ln1b,
           wrh, wrl, br, seq_len):
    T, D = x.shape
    tm = TM_MERGE
    tiles_per_seq = seq_len // tm
    grid = (T // tm,)
    tok = lambda i: (i, 0)
    c2 = lambda i: (0, 0)
    c3 = lambda i: (0, 0, 0)
    halo_blocks = tm // POOL_HALO
    kern = functools.partial(_merge_kernel, tiles_per_seq=tiles_per_seq)
    return pl.pallas_call(
        kern,
        grid=grid,
        in_specs=[
            pl.BlockSpec((tm, D), tok),
            pl.BlockSpec((tm, POOL_WIDTH), tok),
            pl.BlockSpec((POOL_HALO, POOL_WIDTH),
                         lambda i: (jnp.maximum(i * halo_blocks - 1, 0), 0)),
            pl.BlockSpec((tm, ATTN_WIDTH), tok),
            pl.BlockSpec((1, D), c2),
            pl.BlockSpec((1, D), c2),
            pl.BlockSpec((D, N_BRANCHES * D), c2),
            pl.BlockSpec((1, N_BRANCHES * D), c2),
            pl.BlockSpec((POOL_GROUPS, POOL_GROUP_DIM, POOL_GROUP_DIM), c3),
            pl.BlockSpec((1, POOL_WIDTH), c2),
            pl.BlockSpec((POOL_WIDTH, D), c2),
            pl.BlockSpec((ATTN_WIDTH, D), c2),
            pl.BlockSpec((D, D), c2),
            pl.BlockSpec((1, D), c2),
            pl.BlockSpec((1, D), c2),
            pl.BlockSpec((D, LANES), c2),
            pl.BlockSpec((D, LANES), c2),
            pl.BlockSpec((1, LANES), c2),
        ],
        out_specs=[
            pl.BlockSpec((tm, D), tok),
            pl.BlockSpec((tm, LANES), tok),
        ],
        out_shape=[
            jax.ShapeDtypeStruct((T, D), F32),
            jax.ShapeDtypeStruct((T, LANES), F32),
        ],
        compiler_params=pltpu.CompilerParams(
            dimension_semantics=("arbitrary",),
            vmem_limit_bytes=V7X_VMEM_LIMIT_BYTES),
        name="merge",
    )(x, u, u, attn, ln_g, ln_b, wgate, bgate, poolw, pscale, wpp, wpa, wout, ln1g, ln1b,
      wrh, wrl, br)


def _snake_group(tile, step):
    return jnp.where(tile % 2 == 0, step, N_GROUPS - 1 - step)


def _moe_kernel(h1_ref, g_ref, tri_ref, wg_ref, wu_ref, wd_ref, ln2g_ref, ln2b_ref, o_ref,
                xs_ref, yh_ref, yl_ref, dcol_ref, tab_ref):
    tm = TM_MOE
    step = pl.program_id(1)
    g = _snake_group(pl.program_id(0), step)
    lane = lax.broadcasted_iota(I32, (tm, LANES), 1).astype(F32)
    lane1 = lax.broadcasted_iota(I32, (1, LANES), 1).astype(F32)

    @pl.when(step == 0)
    def _():
        rec = g_ref[...]
        gsel = jnp.sum(jnp.where(lane == float(GSEL_LANE), rec, 0.0), axis=1, keepdims=True)
        onehot = jnp.where(lane == gsel, 1.0, 0.0)
        cum = _dot(tri_ref[...], onehot.astype(BF16))
        cnt = cum[tm - 1:tm, :]
        padded = jnp.floor((cnt + (MOE_CHUNK - 1)) * (1.0 / MOE_CHUNK)) * MOE_CHUNK
        start = jnp.zeros((1, LANES), F32)
        for gg in range(N_GROUPS - 1):
            p = jnp.sum(jnp.where(lane1 == float(gg), padded, 0.0), axis=1, keepdims=True)
            start = start + jnp.where(lane1 > float(gg), p, 0.0)
        tab_ref[0:1, :] = start
        tab_ref[1:2, :] = padded
        slot = jnp.sum(onehot * (start + cum - 1.0), axis=1, keepdims=True)
        slot_b = jnp.broadcast_to(slot, (tm, LANES))
        dcol_ref[...] = slot_b
        slot_row = slot_b.T[0:1, :]

        dense = jnp.where(lane < float(N_EXPERTS), rec, 0.0)
        own = jnp.where(gsel == 0.0, dense, 0.0)
        for gg in range(1, N_GROUPS):
            own = own + jnp.where(gsel == float(gg),
                                  pltpu.roll(dense, LANES - gg * EXPERTS_PER_GROUP, 1), 0.0)
        own = jnp.where(lane < float(EXPERTS_PER_GROUP), own, 0.0)
        own_hi = own.astype(BF16).astype(F32)
        side = (own_hi + pltpu.roll(own - own_hi, EXPERTS_PER_GROUP, 1)).astype(BF16)
        xa = jnp.concatenate([h1_ref[...].astype(BF16), side], axis=1)

        blk = 256
        for rb in range(MOE_SLOTS // blk):
            rows = rb * blk + lax.broadcasted_iota(I32, (blk, tm), 0).astype(F32)
            perm = jnp.where(rows == slot_row, 1.0, 0.0).astype(BF16)
            xs_ref[rb * blk:(rb + 1) * blk, :] = _dot(perm, xa).astype(BF16)
        yh_ref[...] = jnp.zeros(yh_ref.shape, BF16)
        yl_ref[...] = jnp.zeros(yl_ref.shape, BF16)

    gf = g.astype(F32)
    first = jnp.sum(jnp.where(lane1 == gf, tab_ref[0:1, :], 0.0)).astype(I32)
    n_chunks = jnp.sum(jnp.where(lane1 == gf, tab_ref[1:2, :], 0.0)).astype(I32) // MOE_CHUNK
    ff = EXPERTS_PER_GROUP * EXPERT_FF

    def ffn_chunk(c, carry):
        r0 = pl.multiple_of(first + c * MOE_CHUNK, MOE_CHUNK)
        xc = xs_ref[pl.ds(r0, MOE_CHUNK), :]
        xb = xc[:, :D_MODEL]
        gl = xc[:, D_MODEL:].astype(F32)
        gates = gl[:, :EXPERTS_PER_GROUP] + gl[:, EXPERTS_PER_GROUP:2 * EXPERTS_PER_GROUP]
        parts = []
        for j in range(EXPERTS_PER_GROUP):
            hid = jax.nn.silu(_dot(xb, wg_ref[j])) * _dot(xb, wu_ref[j])
            parts.append((hid * gates[:, j:j + 1]).astype(BF16))
        y = _dot(jnp.concatenate(parts, axis=1), wd_ref[...].reshape(ff, D_MODEL))
        y_hi = y.astype(BF16)
        yh_ref[pl.ds(r0, MOE_CHUNK), :] = y_hi
        yl_ref[pl.ds(r0, MOE_CHUNK), :] = (y - y_hi.astype(F32)).astype(BF16)
        return carry

    lax.fori_loop(0, n_chunks, ffn_chunk, 0)

    @pl.when(step == N_GROUPS - 1)
    def _():
        slots = lax.broadcasted_iota(I32, (tm, MOE_SLOTS), 1).astype(F32)
        back = jnp.where(slots == dcol_ref[:, 0:1], 1.0, 0.0).astype(BF16)
        ffn = _dot(back, yh_ref[...]) + _dot(back, yl_ref[...])
        o_ref[...] = _layer_norm(DEEPNORM_ALPHA * h1_ref[...] + ffn, ln2g_ref[...], ln2b_ref[...])


def _moe(h1, route, wg, wu, wd, ln2g, ln2b):
    T, D = h1.shape
    tm = TM_MOE
    epg = EXPERTS_PER_GROUP
    grid = (T // tm, N_GROUPS)
    once = pl.Buffered(1)
    return pl.pallas_call(
        _moe_kernel,
        grid=grid,
        in_specs=[
            pl.BlockSpec((tm, D), lambda i, s: (i, 0), pipeline_mode=once),
            pl.BlockSpec((tm, LANES), lambda i, s: (i, 0), pipeline_mode=once),
            pl.BlockSpec((tm, tm), lambda i, s: (0, 0), pipeline_mode=once),
            pl.BlockSpec((epg, D, EXPERT_FF), lambda i, s: (_snake_group(i, s), 0, 0)),
            pl.BlockSpec((epg, D, EXPERT_FF), lambda i, s: (_snake_group(i, s), 0, 0)),
            pl.BlockSpec((epg, EXPERT_FF, D), lambda i, s: (_snake_group(i, s), 0, 0)),
            pl.BlockSpec((1, D), lambda i, s: (0, 0)),
            pl.BlockSpec((1, D), lambda i, s: (0, 0)),
        ],
        out_specs=pl.BlockSpec((tm, D), lambda i, s: (i, 0)),
        out_shape=jax.ShapeDtypeStruct((T, D), F32),
        scratch_shapes=[
            pltpu.VMEM((MOE_SLOTS, D + LANES), BF16),
            pltpu.VMEM((MOE_SLOTS, D), BF16),
            pltpu.VMEM((MOE_SLOTS, D), BF16),
            pltpu.VMEM((tm, LANES), F32),
            pltpu.VMEM((8, LANES), F32),
        ],
        compiler_params=pltpu.CompilerParams(
            dimension_semantics=("arbitrary", "arbitrary"),
            vmem_limit_bytes=V7X_VMEM_LIMIT_BYTES_MOE),
        name="moe",
    )(h1, route, jnp.tril(jnp.ones((tm, tm), BF16)), wg, wu, wd, ln2g, ln2b)


def kernel(x, ln_in_g, ln_in_b, w_in, b_gate, pool_w, pool_scale, w_proj_pool, w_proj_attn, w_out,
           ln1_g, ln1_b, w_group, b_group, w_router, b_router, w_gate, w_up, w_down, ln2_g, ln2_b):
    B, L, D = x.shape
    assert D == D_MODEL and w_in.shape[0] == DEPTH == 1
    assert L % TQ == 0 and L % TM_MERGE == 0 and L % TM_PROJ == 0
    assert L <= 4096
    T = B * L
    row = lambda v: v.reshape(1, -1).astype(F32)

    w = w_in[0]
    o = 0
    w_u = w[:, o:o + POOL_WIDTH]; o += POOL_WIDTH
    w_q = w[:, o:o + ATTN_WIDTH]; o += ATTN_WIDTH
    w_k = w[:, o:o + KV_DIM]; o += KV_DIM
    w_v = w[:, o:o + KV_DIM]; o += KV_DIM
    w_iq = w[:, o:o + IDX_HEADS * IDX_DIM]; o += IDX_HEADS * IDX_DIM
    w_ik = w[:, o:o + IDX_DIM]; o += IDX_DIM
    w_iw = w[:, o:o + IDX_HEADS]; o += IDX_HEADS
    w_g = w[:, o:]
    sm_scale = math.log2(math.e) / math.sqrt(HEAD_DIM)
    bf16_rows = 16
    w_t = jnp.concatenate([w_q * sm_scale, w_v, w_iq, w_iw,
                           jnp.zeros((D, bf16_rows - IDX_HEADS), F32)], axis=1).T.astype(BF16)

    u, k, ik, qw, vaug, iqt, iwt = _in_proj(
        x, row(ln_in_g), row(ln_in_b), w_u.astype(BF16), w_k.astype(BF16), w_ik.astype(BF16), w_t)
    attn = _dsa_attention(iqt, iwt, ik, qw, k, vaug)

    w_r = jnp.zeros((D, LANES), F32)
    w_r = w_r.at[:, :N_EXPERTS].set(w_router[0]).at[:, N_EXPERTS:N_EXPERTS + N_GROUPS].set(w_group[0])
    b_r = jnp.zeros((1, LANES), F32)
    b_r = b_r.at[0, :N_EXPERTS].set(b_router[0]).at[0, N_EXPERTS:N_EXPERTS + N_GROUPS].set(b_group[0])
    w_rh = w_r.astype(BF16)
    w_rl = (w_r - w_rh.astype(F32)).astype(BF16)

    h1, gates = _merge(
        x.reshape(T, D), u.reshape(T, POOL_WIDTH), attn.reshape(T, ATTN_WIDTH),
        row(ln_in_g), row(ln_in_b), w_g.astype(BF16), row(b_gate[0]),
        pool_w[0].astype(BF16), row(pool_scale[0]), w_proj_pool[0].astype(BF16),
        w_proj_attn[0].astype(BF16), w_out[0].astype(BF16), row(ln1_g[0]), row(ln1_b[0]),
        w_rh, w_rl, b_r, L)

    out = _moe(h1, gates, w_gate[0].astype(BF16), w_up[0].astype(BF16), w_down[0].astype(BF16),
               row(ln2_g[0]), row(ln2_b[0]))
    return out.reshape(B, L, D)
```

```python
import functools
import math

import jax
import jax.numpy as jnp
import numpy as np
from jax import lax
from jax.experimental import pallas as pl
from jax.experimental.pallas import tpu as pltpu

D_MODEL = 1024
POOL_WINDOWS = (2, 4, 8, 16)
POOL_GROUPS = 4
POOL_WIDTH = 512
POOL_GROUP_DIM = 128
N_HEADS = 8
HEAD_DIM = 64
ATTN_WIDTH = 512
KV_DIM = 64
IDX_HEADS = 8
IDX_DIM = 32
TOPK_MAX = 256
NEG_INF = float(np.float32(-1e30))
N_BRANCHES = 2
N_GROUPS = 4
EXPERTS_PER_GROUP = 8
N_EXPERTS = 32
EXPERT_FF = 256
LN_EPS = 1e-5
DEPTH = 1
DEEPNORM_ALPHA = (2.0 * DEPTH) ** 0.25

V7X_VMEM_LIMIT_BYTES = 56 * 1024 * 1024
V7X_VMEM_LIMIT_BYTES_MOE = 60 * 1024 * 1024
LANES = 128

F32 = jnp.float32
BF16 = jnp.bfloat16
I32 = jnp.int32
INT_MIN = -2 ** 31
INT_MAX = 2 ** 31 - 1

TM_PROJ = 512
TQ = 256
TK = 128
TM_MERGE = 512
POOL_HALO = 16
V_ONES_ROWS = 16
TM_MOE = 1024
MOE_CHUNK = 128
MOE_SLOTS = TM_MOE + N_GROUPS * MOE_CHUNK
GSEL_LANE = N_EXPERTS


def _layer_norm(x, g, b):
    mu = jnp.mean(x, axis=-1, keepdims=True)
    xc = x - mu
    var = jnp.mean(xc * xc, axis=-1, keepdims=True)
    return xc * lax.rsqrt(var + LN_EPS) * g + b


def _dot(a, b):
    return jnp.dot(a, b, preferred_element_type=F32)


def _dot_nt(a, b):
    return lax.dot_general(a, b, (((1,), (1,)), ((), ())), preferred_element_type=F32)


def _in_proj_kernel(x_ref, g_ref, b_ref, wu_ref, wk_ref, wik_ref, wt_ref,
                    u_ref, k_ref, ik_ref, qt_ref, vt_ref, iqt_ref, iwt_ref):
    h = _layer_norm(x_ref[...], g_ref[...], b_ref[...])
    hb = h.astype(BF16)
    u_ref[...] = _dot(hb, wu_ref[...])
    n_chunks = TM_PROJ // TK
    k_ref[...] = _dot(hb, wk_ref[...]).astype(BF16).reshape(n_chunks, TK, KV_DIM)
    ik_ref[...] = _dot(hb, wik_ref[...]).astype(BF16).reshape(n_chunks, TK, IDX_DIM)
    pt = _dot_nt(wt_ref[...], hb)
    r0 = 0
    for j in range(TM_PROJ // TQ):
        for h in range(N_HEADS):
            qt_ref[j, :, h * TQ:(h + 1) * TQ] = pt[r0 + h * HEAD_DIM:r0 + (h + 1) * HEAD_DIM,
                                                   j * TQ:(j + 1) * TQ].astype(BF16)
    r0 += ATTN_WIDTH
    for c in range(n_chunks):
        vt_ref[c, :KV_DIM, :] = pt[r0:r0 + KV_DIM, c * TK:(c + 1) * TK].astype(BF16)
        vt_ref[c, KV_DIM:, :] = jnp.ones((V_ONES_ROWS, TK), BF16)
    r0 += KV_DIM
    iqt_ref[...] = pt[r0:r0 + IDX_HEADS * IDX_DIM].astype(BF16)
    r0 += IDX_HEADS * IDX_DIM
    iwt_ref[...] = pt[r0:r0 + IDX_HEADS]


def _in_proj(x, ln_g, ln_b, wu, wk, wik, wt):
    B, L, D = x.shape
    tm = TM_PROJ
    grid = (B, L // tm)
    tok = lambda b, i: (b, i, 0)
    tokt = lambda b, i: (b, 0, i)
    chunked = lambda b, i: (b, i, 0, 0)
    const2 = lambda b, i: (0, 0)
    n_t = wt.shape[0]
    return pl.pallas_call(
        _in_proj_kernel,
        grid=grid,
        in_specs=[
            pl.BlockSpec((None, tm, D), tok),
            pl.BlockSpec((1, D), const2),
            pl.BlockSpec((1, D), const2),
            pl.BlockSpec((D, POOL_WIDTH), const2),
            pl.BlockSpec((D, KV_DIM), const2),
            pl.BlockSpec((D, IDX_DIM), const2),
            pl.BlockSpec((n_t, D), const2),
        ],
        out_specs=[
            pl.BlockSpec((None, tm, POOL_WIDTH), tok),
            pl.BlockSpec((None, tm // TK, TK, KV_DIM), chunked),
            pl.BlockSpec((None, tm // TK, TK, IDX_DIM), chunked),
            pl.BlockSpec((None, tm // TQ, HEAD_DIM, N_HEADS * TQ), chunked),
            pl.BlockSpec((None, tm // TK, KV_DIM + V_ONES_ROWS, TK), chunked),
            pl.BlockSpec((None, IDX_HEADS * IDX_DIM, tm), tokt),
            pl.BlockSpec((None, IDX_HEADS, tm), tokt),
        ],
        out_shape=[
            jax.ShapeDtypeStruct((B, L, POOL_WIDTH), F32),
            jax.ShapeDtypeStruct((B, L // TK, TK, KV_DIM), BF16),
            jax.ShapeDtypeStruct((B, L // TK, TK, IDX_DIM), BF16),
            jax.ShapeDtypeStruct((B, L // TQ, HEAD_DIM, N_HEADS * TQ), BF16),
            jax.ShapeDtypeStruct((B, L // TK, KV_DIM + V_ONES_ROWS, TK), BF16),
            jax.ShapeDtypeStruct((B, IDX_HEADS * IDX_DIM, L), BF16),
            jax.ShapeDtypeStruct((B, IDX_HEADS, L), F32),
        ],
        compiler_params=pltpu.CompilerParams(
            dimension_semantics=("arbitrary", "arbitrary"),
            vmem_limit_bytes=V7X_VMEM_LIMIT_BYTES),
        name="in_proj",
    )(x, ln_g, ln_b, wu, wk, wik, wt)


BRACKET_PASSES = 14
NO_TIE = 1e9


def _fori_pairs(n, body, init):
    def pair(j, carry):
        return body(2 * j + 1, body(2 * j, carry))
    carry = lax.fori_loop(0, n // 2, pair, init)
    return lax.fori_loop(2 * (n // 2), n, body, carry)


def _rows_to_sublanes(x, op):
    return op(x.reshape(x.shape[0] // 8, 8, TQ), axis=0)


SCAN_CHUNKS = 2
SCAN_ROWS = SCAN_CHUNKS * TK


def _dsa_kernel(iqt_ref, iwt_ref, ik_ref, qw_ref, k_ref, va_ref, o_ref,
                sc_ref, thr_ref, tie_ref, m_ref, acc_ref, lga_ref, lgb_ref, *, seq_len, n_sel):
    assert (TQ // TK) % 2 == 0
    qi = pl.program_id(1)
    q0 = qi * TQ
    nkc = (qi + 1) * (TQ // TK)
    n_beyond = seq_len - (qi + 1) * TQ
    k_sel = jnp.float32(n_sel)

    row_iota = lax.broadcasted_iota(I32, (TK, TQ), 0)
    t_idx = q0 + lax.broadcasted_iota(I32, (TK, TQ), 1)
    scan_rows = lax.broadcasted_iota(I32, (SCAN_ROWS, TQ), 0)
    scan_t = q0 + lax.broadcasted_iota(I32, (SCAN_ROWS, TQ), 1)
    n_scan = nkc // SCAN_CHUNKS

    def scores_at(b):
        return sc_ref[pl.ds(b * SCAN_CHUNKS, SCAN_CHUNKS)].reshape(SCAN_ROWS, TQ)

    def score_step(b, carry):
        smin, smax = carry
        ikc = ik_ref[pl.ds(b * SCAN_CHUNKS, SCAN_CHUNKS)].reshape(SCAN_ROWS, IDX_DIM)
        score = jnp.zeros((SCAN_ROWS, TQ), F32)
        for h in range(IDX_HEADS):
            lg = _dot(ikc, iqt_ref[h * IDX_DIM:(h + 1) * IDX_DIM, :])
            score = score + iwt_ref[h:h + 1, :] * jnp.maximum(lg, 0.0)
        smin = jnp.minimum(smin, jnp.min(score, axis=0, keepdims=True))
        smax = jnp.maximum(smax, jnp.max(score, axis=0, keepdims=True))
        score = jnp.where(b * SCAN_ROWS + scan_rows <= scan_t, score, NEG_INF)
        sc_ref[pl.ds(b * SCAN_CHUNKS, SCAN_CHUNKS)] = score.reshape(SCAN_CHUNKS, TK, TQ)
        return smin, smax

    big = jnp.float32(3e38)
    smin, smax = _fori_pairs(n_scan, score_step,
                             (jnp.full((1, TQ), big, F32), jnp.full((1, TQ), -big, F32)))

    nb_f = n_beyond.astype(F32)
    ninf = jnp.float32(-jnp.inf)

    def count_ge(cand):
        def body(b, acc):
            return acc + _rows_to_sublanes(jnp.where(scores_at(b) >= cand, 1.0, 0.0), jnp.sum)
        acc = _fori_pairs(n_scan, body, jnp.zeros((8, TQ), F32))
        return jnp.sum(acc, axis=0, keepdims=True) + jnp.where(cand <= NEG_INF, nb_f, 0.0)

    def max_below(h):
        def body(b, acc):
            s = scores_at(b)
            return jnp.maximum(acc, _rows_to_sublanes(jnp.where(s < h, s, ninf), jnp.max))
        acc = _fori_pairs(n_scan, body, jnp.full((8, TQ), ninf, F32))
        return jnp.max(acc, axis=0, keepdims=True)

    def count_ge_and_max_below(v):
        def body(b, carry):
            acc, mx = carry
            s = scores_at(b)
            acc = acc + _rows_to_sublanes(jnp.where(s >= v, 1.0, 0.0), jnp.sum)
            mx = jnp.maximum(mx, _rows_to_sublanes(jnp.where(s < v, s, ninf), jnp.max))
            return acc, mx
        acc, mx = _fori_pairs(n_scan, body,
                              (jnp.zeros((8, TQ), F32), jnp.full((8, TQ), ninf, F32)))
        cnt = jnp.sum(acc, axis=0, keepdims=True) + jnp.where(v <= NEG_INF, nb_f, 0.0)
        return cnt, jnp.max(mx, axis=0, keepdims=True)

    n_adm = (q0 + 1 + lax.broadcasted_iota(I32, (1, TQ), 1)).astype(F32)
    few = n_adm < k_sel
    lo0 = jnp.where(few, NEG_INF, smin)
    clo0 = jnp.where(few, k_sel, jnp.where(smin <= NEG_INF, jnp.float32(seq_len), n_adm))
    hi0 = smax + (jnp.abs(smax) * 1e-6 + 1e-30)
    chi0 = jnp.zeros((1, TQ), F32)

    def bracket_body(it, st):
        lo, hi, clo, chi, flo, fhi, side = st
        done = clo == k_sel
        frac = jnp.clip(flo / (flo - fhi), 1.0 / 512, 511.0 / 512)
        frac = jnp.where(clo - chi <= 2.0, 0.5, frac)
        cand = lo + (hi - lo) * frac
        zero_inside = jnp.logical_and(jnp.logical_and(lo < 0.0, hi > 0.0), it == 0)
        cand = jnp.where(done, lo, jnp.where(zero_inside, 0.0, cand))
        cnt = count_ge(cand)
        ge = cnt >= k_sel
        f = cnt - (k_sel - 0.5)
        new_side = jnp.where(ge, 1.0, -1.0)
        same = new_side == side
        flo_n = jnp.where(ge, f, jnp.where(same, flo * 0.5, flo))
        fhi_n = jnp.where(ge, jnp.where(same, fhi * 0.5, fhi), f)
        up_lo = jnp.logical_and(jnp.logical_not(done), ge)
        up_hi = jnp.logical_and(jnp.logical_not(done), jnp.logical_not(ge))
        lo = jnp.where(up_lo, cand, lo)
        clo = jnp.where(up_lo, cnt, clo)
        hi = jnp.where(up_hi, cand, hi)
        chi = jnp.where(up_hi, cnt, chi)
        flo = jnp.where(done, flo, flo_n)
        fhi = jnp.where(done, fhi, fhi_n)
        side = jnp.where(done, side, new_side)
        return lo, hi, clo, chi, flo, fhi, side

    lo, hi, clo, chi = lax.fori_loop(
        0, BRACKET_PASSES, bracket_body,
        (lo0, hi0, clo0, chi0, clo0 - (k_sel - 0.5), chi0 - (k_sel - 0.5),
         jnp.zeros((1, TQ), F32)))[:4]
    pending = jnp.sum(jnp.where(clo == k_sel, 0.0, 1.0))
    thr_ref[...] = lo
    tie_ref[...] = jnp.full((1, TQ), NO_TIE, F32)

    @pl.when(pending > 0.5)
    def _():
        fin0 = jnp.where(clo == k_sel, 1.0, 0.0)
        v0 = max_below(hi)

        def fin_cond(st):
            return jnp.logical_and(st[0] < seq_len + 2, st[1] > 0.5)

        def fin_body(st):
            j, _, fin, h, ch, v, kst, need = st
            cnt, v2 = count_ge_and_max_below(v)
            hit = jnp.logical_and(fin < 0.5, cnt >= k_sel)
            kst = jnp.where(hit, v, kst)
            need = jnp.where(jnp.logical_and(hit, cnt > k_sel), k_sel - ch, need)
            fin = jnp.where(hit, 1.0, fin)
            open_ = fin < 0.5
            h = jnp.where(open_, v, h)
            ch = jnp.where(open_, cnt, ch)
            v = jnp.where(open_, v2, v)
            return j + 1, jnp.sum(1.0 - fin), fin, h, ch, v, kst, need

        st2 = lax.while_loop(
            fin_cond, fin_body,
            (jnp.int32(0), pending, fin0, hi, chi, v0, lo, jnp.full((1, TQ), NO_TIE, F32)))
        thr_ref[...] = st2[6]
        tie_ref[...] = st2[7]

    kstar = thr_ref[...]
    need = tie_ref[...]
    fix = jnp.logical_or(need < NO_TIE, kstar <= NEG_INF)

    @pl.when(jnp.sum(jnp.where(fix, 1.0, 0.0)) > 0.5)
    def _():
        tri = jnp.where(lax.broadcasted_iota(I32, (TK, TK), 0) >= lax.broadcasted_iota(I32, (TK, TK), 1),
                        1.0, 0.0).astype(BF16)

        def rewrite(c, seen):
            s = sc_ref[c]
            tied = s == kstar
            rank = seen + _dot(tri, jnp.where(tied, 1.0, 0.0).astype(BF16))
            take = jnp.logical_or(s > kstar, jnp.logical_and(tied, rank <= need))
            take = jnp.logical_and(take, c * TK + row_iota <= t_idx)
            sc_ref[c] = jnp.where(take, 1.0, -1.0)
            return rank[TK - 1:TK, :]

        lax.fori_loop(0, nkc, rewrite, jnp.zeros((1, TQ), F32))
        thr_ref[...] = jnp.zeros((1, TQ), F32)

    m_ref[...] = jnp.full(m_ref.shape, NEG_INF, F32)
    acc_ref[...] = jnp.zeros(acc_ref.shape, F32)
    thr = thr_ref[...]

    def masked_logits(c):
        bias = jnp.where(sc_ref[c] >= thr, 0.0, NEG_INF)
        lg = _dot(k_ref[c], qw_ref[...])
        lg = jnp.concatenate(
            [lg[:, h * TQ:(h + 1) * TQ] + bias for h in range(N_HEADS)], axis=1)
        return lg, jnp.max(lg, axis=0, keepdims=True)

    def attend(c, lg, lg_max):
        m_old = m_ref[...]
        m_new = jnp.maximum(m_old, lg_max)
        alpha = jnp.exp2(m_old - m_new)
        p = jnp.exp2(lg - m_new).astype(BF16)
        acc_ref[...] = alpha * acc_ref[...] + _dot(va_ref[c], p)
        m_ref[...] = m_new

    def attend_pair(i, max_a):
        c = 2 * i
        lg_b, max_b = masked_logits(c + 1)
        lgb_ref[...] = lg_b
        attend(c, lga_ref[...], max_a)
        lg_a, max_a = masked_logits(jnp.minimum(c + 2, nkc - 1))
        attend(c + 1, lgb_ref[...], max_b)
        lga_ref[...] = lg_a
        return max_a

    lg_a, max_a = masked_logits(0)
    lga_ref[...] = lg_a
    lax.fori_loop(0, nkc // 2, attend_pair, max_a)

    a = acc_ref[...]
    o = a[:HEAD_DIM] / a[HEAD_DIM:HEAD_DIM + 1]
    o = jnp.concatenate([o[:, h * TQ:(h + 1) * TQ] for h in range(N_HEADS)], axis=0)
    o_ref[...] = o.T


def _dsa_attention(iqt, iwt, ik, qw, k, vaug):
    B, _, L = iqt.shape
    n_sel = min(TOPK_MAX, L // 4)
    nch = L // TK
    va_rows = vaug.shape[2]
    grid = (B, L // TQ)
    kern = functools.partial(_dsa_kernel, seq_len=L, n_sel=n_sel)
    return pl.pallas_call(
        kern,
        grid=grid,
        in_specs=[
            pl.BlockSpec((None, IDX_HEADS * IDX_DIM, TQ), lambda b, i: (b, 0, i)),
            pl.BlockSpec((None, IDX_HEADS, TQ), lambda b, i: (b, 0, i)),
            pl.BlockSpec((None, nch, TK, IDX_DIM), lambda b, i: (b, 0, 0, 0)),
            pl.BlockSpec((None, None, HEAD_DIM, N_HEADS * TQ), lambda b, i: (b, i, 0, 0)),
            pl.BlockSpec((None, nch, TK, HEAD_DIM), lambda b, i: (b, 0, 0, 0)),
            pl.BlockSpec((None, nch, va_rows, TK), lambda b, i: (b, 0, 0, 0)),
        ],
        out_specs=pl.BlockSpec((None, TQ, ATTN_WIDTH), lambda b, i: (b, i, 0)),
        out_shape=jax.ShapeDtypeStruct((B, L, ATTN_WIDTH), F32),
        scratch_shapes=[
            pltpu.VMEM((nch, TK, TQ), F32),
            pltpu.VMEM((1, TQ), F32),
            pltpu.VMEM((1, TQ), F32),
            pltpu.VMEM((1, N_HEADS * TQ), F32),
            pltpu.VMEM((va_rows, N_HEADS * TQ), F32),
            pltpu.VMEM((TK, N_HEADS * TQ), F32),
            pltpu.VMEM((TK, N_HEADS * TQ), F32),
        ],
        compiler_params=pltpu.CompilerParams(
            dimension_semantics=("arbitrary", "arbitrary"),
            vmem_limit_bytes=V7X_VMEM_LIMIT_BYTES),
        name="dsa_attn",
    )(iqt, iwt, ik, qw, k, vaug)


def _merge_kernel(x_ref, u_ref, uh_ref, a_ref, lng_ref, lnb_ref, wgate_ref, bgate_ref,
                  poolw_ref, pscale_ref, wpp_ref, wpa_ref, wout_ref, ln1g_ref, ln1b_ref,
                  wrh_ref, wrl_ref, br_ref, h1_ref, gates_ref, *, tiles_per_seq):
    tm = TM_MERGE
    i = pl.program_id(0)
    seq_start = (i % tiles_per_seq) == 0
    h = _layer_norm(x_ref[...], lng_ref[...], lnb_ref[...])
    hb = h.astype(BF16)

    halo = jnp.where(seq_start, 0.0, uh_ref[...])
    u = u_ref[...]
    ext = jnp.concatenate([halo, u], axis=0)
    pos = (i % tiles_per_seq) * tm + lax.broadcasted_iota(I32, (tm, 1), 0)
    mixed = []
    for g, w in enumerate(POOL_WINDOWS):
        s = ext[:, g * POOL_GROUP_DIM:(g + 1) * POOL_GROUP_DIM]
        span = 1
        while span < w:
            s = s + pltpu.roll(s, span, 0)
            span *= 2
        win = s[POOL_HALO:]
        cnt = jnp.minimum(pos + 1, w).astype(F32)
        ug = u[:, g * POOL_GROUP_DIM:(g + 1) * POOL_GROUP_DIM]
        delta = win / cnt - ug
        mixed.append(_dot(delta.astype(BF16), poolw_ref[g]))
    pool_out = jnp.concatenate(mixed, axis=1) * pscale_ref[...]

    gate_pre = _dot(hb, wgate_ref[...]) + bgate_ref[...]
    gates = jax.nn.sigmoid(gate_pre)
    bp = _dot(pool_out.astype(BF16), wpp_ref[...])
    ba = _dot(a_ref[...].astype(BF16), wpa_ref[...])
    merged = gates[:, :D_MODEL] * bp + gates[:, D_MODEL:] * ba
    mix = _dot(merged.astype(BF16), wout_ref[...])
    h1 = _layer_norm(DEEPNORM_ALPHA * h + mix, ln1g_ref[...], ln1b_ref[...])
    h1_ref[...] = h1

    hi = h1.astype(BF16)
    lo = (h1 - hi.astype(F32)).astype(BF16)
    lg = _dot(hi, wrh_ref[...]) + (_dot(lo, wrh_ref[...]) + _dot(hi, wrl_ref[...])) + br_ref[...]
    lane = lax.broadcasted_iota(I32, (tm, LANES), 1).astype(F32)
    big = jnp.float32(1 << 20)
    ninf = jnp.float32(-jnp.inf)
    is_g = jnp.logical_and(lane >= N_EXPERTS, lane < N_EXPERTS + N_GROUPS)
    gl = jnp.where(is_g, lg, ninf)
    gmax = jnp.max(gl, axis=1, keepdims=True)
    gsel = jnp.min(jnp.where(gl == gmax, lane, big), axis=1, keepdims=True) - N_EXPERTS
    sumexp = jnp.sum(jnp.where(is_g, jnp.exp(gl - gmax), 0.0), axis=1, keepdims=True)
    p_group = 1.0 / sumexp
    e_lo = gsel * EXPERTS_PER_GROUP
    in_grp = jnp.logical_and(lane >= e_lo, lane < e_lo + EXPERTS_PER_GROUP)
    el = jnp.where(in_grp, lg, ninf)
    m1 = jnp.max(el, axis=1, keepdims=True)
    i1 = jnp.min(jnp.where(el == m1, lane, big), axis=1, keepdims=True)
    el2 = jnp.where(lane == i1, ninf, el)
    m2 = jnp.max(el2, axis=1, keepdims=True)
    i2 = jnp.min(jnp.where(el2 == m2, lane, big), axis=1, keepdims=True)
    e2 = jnp.exp(m2 - m1)
    den = 1.0 + e2
    w1 = (1.0 / den) * p_group
    w2 = (e2 / den) * p_group
    dense = jnp.where(lane == i1, w1, 0.0) + jnp.where(lane == i2, w2, 0.0)
    gates_ref[...] = jnp.where(lane == float(GSEL_LANE), gsel, dense)


def _merge(x, u, attn, ln_g, ln_b, wgate, bgate, poolw, pscale, wpp, wpa, wout, ln1g, ln1b,
           wrh, wrl, br, seq_len):
    T, D = x.shape
    tm = TM_MERGE
    tiles_per_seq = seq_len // tm
    grid = (T // tm,)
    tok = lambda i: (i, 0)
    c2 = lambda i: (0, 0)
    c3 = lambda i: (0, 0, 0)
    halo_blocks = tm // POOL_HALO
    kern = functools.partial(_merge_kernel, tiles_per_seq=tiles_per_seq)
    return pl.pallas_call(
        kern,
        grid=grid,
        in_specs=[
            pl.BlockSpec((tm, D), tok),
            pl.BlockSpec((tm, POOL_WIDTH), tok),
            pl.BlockSpec((POOL_HALO, POOL_WIDTH),
                         lambda i: (jnp.maximum(i * halo_blocks - 1, 0), 0)),
            pl.BlockSpec((tm, ATTN_WIDTH), tok),
            pl.BlockSpec((1, D), c2),
            pl.BlockSpec((1, D), c2),
            pl.BlockSpec((D, N_BRANCHES * D), c2),
            pl.BlockSpec((1, N_BRANCHES * D), c2),
            pl.BlockSpec((POOL_GROUPS, POOL_GROUP_DIM, POOL_GROUP_DIM), c3),
            pl.BlockSpec((1, POOL_WIDTH), c2),
            pl.BlockSpec((POOL_WIDTH, D), c2),
            pl.BlockSpec((ATTN_WIDTH, D), c2),
            pl.BlockSpec((D, D), c2),
            pl.BlockSpec((1, D), c2),
            pl.BlockSpec((1, D), c2),
            pl.BlockSpec((D, LANES), c2),
            pl.BlockSpec((D, LANES), c2),
            pl.BlockSpec((1, LANES), c2),
        ],
        out_specs=[
            pl.BlockSpec((tm, D), tok),
            pl.BlockSpec((tm, LANES), tok),
        ],
        out_shape=[
            jax.ShapeDtypeStruct((T, D), F32),
            jax.ShapeDtypeStruct((T, LANES), F32),
        ],
        compiler_params=pltpu.CompilerParams(
            dimension_semantics=("arbitrary",),
            vmem_limit_bytes=V7X_VMEM_LIMIT_BYTES),
        name="merge",
    )(x, u, u, attn, ln_g, ln_b, wgate, bgate, poolw, pscale, wpp, wpa, wout, ln1g, ln1b,
      wrh, wrl, br)


def _snake_group(tile, step):
    return jnp.where(tile % 2 == 0, step, N_GROUPS - 1 - step)


def _moe_kernel(h1_ref, g_ref, tri_ref, wg_ref, wu_ref, wd_ref, ln2g_ref, ln2b_ref, o_ref,
                xs_ref, yh_ref, yl_ref, dcol_ref, tab_ref):
    tm = TM_MOE
    step = pl.program_id(1)
    g = _snake_group(pl.program_id(0), step)
    lane = lax.broadcasted_iota(I32, (tm, LANES), 1).astype(F32)
    lane1 = lax.broadcasted_iota(I32, (1, LANES), 1).astype(F32)

    @pl.when(step == 0)
    def _():
        rec = g_ref[...]
        gsel = jnp.sum(jnp.where(lane == float(GSEL_LANE), rec, 0.0), axis=1, keepdims=True)
        onehot = jnp.where(lane == gsel, 1.0, 0.0)
        cum = _dot(tri_ref[...], onehot.astype(BF16))
        cnt = cum[tm - 1:tm, :]
        padded = jnp.floor((cnt + (MOE_CHUNK - 1)) * (1.0 / MOE_CHUNK)) * MOE_CHUNK
        start = jnp.zeros((1, LANES), F32)
        for gg in range(N_GROUPS - 1):
            p = jnp.sum(jnp.where(lane1 == float(gg), padded, 0.0), axis=1, keepdims=True)
            start = start + jnp.where(lane1 > float(gg), p, 0.0)
        tab_ref[0:1, :] = start
        tab_ref[1:2, :] = padded
        slot = jnp.sum(onehot * (start + cum - 1.0), axis=1, keepdims=True)
        slot_b = jnp.broadcast_to(slot, (tm, LANES))
        dcol_ref[...] = slot_b
        slot_row = slot_b.T[0:1, :]

        dense = jnp.where(lane < float(N_EXPERTS), rec, 0.0)
        own = jnp.where(gsel == 0.0, dense, 0.0)
        for gg in range(1, N_GROUPS):
            own = own + jnp.where(gsel == float(gg),
                                  pltpu.roll(dense, LANES - gg * EXPERTS_PER_GROUP, 1), 0.0)
        own = jnp.where(lane < float(EXPERTS_PER_GROUP), own, 0.0)
        own_hi = own.astype(BF16).astype(F32)
        side = (own_hi + pltpu.roll(own - own_hi, EXPERTS_PER_GROUP, 1)).astype(BF16)
        xa = jnp.concatenate([h1_ref[...].astype(BF16), side], axis=1)

        blk = 256
        for rb in range(MOE_SLOTS // blk):
            rows = rb * blk + lax.broadcasted_iota(I32, (blk, tm), 0).astype(F32)
            perm = jnp.where(rows == slot_row, 1.0, 0.0).astype(BF16)
            xs_ref[rb * blk:(rb + 1) * blk, :] = _dot(perm, xa).astype(BF16)
        yh_ref[...] = jnp.zeros(yh_ref.shape, BF16)
        yl_ref[...] = jnp.zeros(yl_ref.shape, BF16)

    gf = g.astype(F32)
    first = jnp.sum(jnp.where(lane1 == gf, tab_ref[0:1, :], 0.0)).astype(I32)
    n_chunks = jnp.sum(jnp.where(lane1 == gf, tab_ref[1:2, :], 0.0)).astype(I32) // MOE_CHUNK
    ff = EXPERTS_PER_GROUP * EXPERT_FF

    def ffn_chunk(c, carry):
        r0 = pl.multiple_of(first + c * MOE_CHUNK, MOE_CHUNK)
        xc = xs_ref[pl.ds(r0, MOE_CHUNK), :]
        xb = xc[:, :D_MODEL]
        gl = xc[:, D_MODEL:].astype(F32)
        gates = gl[:, :EXPERTS_PER_GROUP] + gl[:, EXPERTS_PER_GROUP:2 * EXPERTS_PER_GROUP]
        parts = []
        for j in range(EXPERTS_PER_GROUP):
            hid = jax.nn.silu(_dot(xb, wg_ref[j])) * _dot(xb, wu_ref[j])
            parts.append((hid * gates[:, j:j + 1]).astype(BF16))
        y = _dot(jnp.concatenate(parts, axis=1), wd_ref[...].reshape(ff, D_MODEL))
        y_hi = y.astype(BF16)
        yh_ref[pl.ds(r0, MOE_CHUNK), :] = y_hi
        yl_ref[pl.ds(r0, MOE_CHUNK), :] = (y - y_hi.astype(F32)).astype(BF16)
        return carry

    lax.fori_loop(0, n_chunks, ffn_chunk, 0)

    @pl.when(step == N_GROUPS - 1)
    def _():
        slots = lax.broadcasted_iota(I32, (tm, MOE_SLOTS), 1).astype(F32)
        back = jnp.where(slots == dcol_ref[:, 0:1], 1.0, 0.0).astype(BF16)
        ffn = _dot(back, yh_ref[...]) + _dot(back, yl_ref[...])
        o_ref[...] = _layer_norm(DEEPNORM_ALPHA * h1_ref[...] + ffn, ln2g_ref[...], ln2b_ref[...])


def _moe(h1, route, wg, wu, wd, ln2g, ln2b):
    T, D = h1.shape
    tm = TM_MOE
    epg = EXPERTS_PER_GROUP
    grid = (T // tm, N_GROUPS)
    once = pl.Buffered(1)
    return pl.pallas_call(
        _moe_kernel,
        grid=grid,
        in_specs=[
            pl.BlockSpec((tm, D), lambda i, s: (i, 0), pipeline_mode=once),
            pl.BlockSpec((tm, LANES), lambda i, s: (i, 0), pipeline_mode=once),
            pl.BlockSpec((tm, tm), lambda i, s: (0, 0), pipeline_mode=once),
            pl.BlockSpec((epg, D, EXPERT_FF), lambda i, s: (_snake_group(i, s), 0, 0)),
            pl.BlockSpec((epg, D, EXPERT_FF), lambda i, s: (_snake_group(i, s), 0, 0)),
            pl.BlockSpec((epg, EXPERT_FF, D), lambda i, s: (_snake_group(i, s), 0, 0)),
            pl.BlockSpec((1, D), lambda i, s: (0, 0)),
            pl.BlockSpec((1, D), lambda i, s: (0, 0)),
        ],
        out_specs=pl.BlockSpec((tm, D), lambda i, s: (i, 0)),
        out_shape=jax.ShapeDtypeStruct((T, D), F32),
        scratch_shapes=[
            pltpu.VMEM((MOE_SLOTS, D + LANES), BF16),
            pltpu.VMEM((MOE_SLOTS, D), BF16),
            pltpu.VMEM((MOE_SLOTS, D), BF16),
            pltpu.VMEM((tm, LANES), F32),
            pltpu.VMEM((8, LANES), F32),
        ],
        compiler_params=pltpu.CompilerParams(
            dimension_semantics=("arbitrary", "arbitrary"),
            vmem_limit_bytes=V7X_VMEM_LIMIT_BYTES_MOE),
        name="moe",
    )(h1, route, jnp.tril(jnp.ones((tm, tm), BF16)), wg, wu, wd, ln2g, ln2b)


def kernel(x, ln_in_g, ln_in_b, w_in, b_gate, pool_w, pool_scale, w_proj_pool, w_proj_attn, w_out,
           ln1_g, ln1_b, w_group, b_group, w_router, b_router, w_gate, w_up, w_down, ln2_g, ln2_b):
    B, L, D = x.shape
    assert D == D_MODEL and w_in.shape[0] == DEPTH == 1
    assert L % TQ == 0 and L % TM_MERGE == 0 and L % TM_PROJ == 0
    assert L <= 4096
    T = B * L
    row = lambda v: v.reshape(1, -1).astype(F32)

    w = w_in[0]
    o = 0
    w_u = w[:, o:o + POOL_WIDTH]; o += POOL_WIDTH
    w_q = w[:, o:o + ATTN_WIDTH]; o += ATTN_WIDTH
    w_k = w[:, o:o + KV_DIM]; o += KV_DIM
    w_v = w[:, o:o + KV_DIM]; o += KV_DIM
    w_iq = w[:, o:o + IDX_HEADS * IDX_DIM]; o += IDX_HEADS * IDX_DIM
    w_ik = w[:, o:o + IDX_DIM]; o += IDX_DIM
    w_iw = w[:, o:o + IDX_HEADS]; o += IDX_HEADS
    w_g = w[:, o:]
    sm_scale = math.log2(math.e) / math.sqrt(HEAD_DIM)
    bf16_rows = 16
    w_t = jnp.concatenate([w_q * sm_scale, w_v, w_iq, w_iw,
                           jnp.zeros((D, bf16_rows - IDX_HEADS), F32)], axis=1).T.astype(BF16)

    u, k, ik, qw, vaug, iqt, iwt = _in_proj(
        x, row(ln_in_g), row(ln_in_b), w_u.astype(BF16), w_k.astype(BF16), w_ik.astype(BF16), w_t)
    attn = _dsa_attention(iqt, iwt, ik, qw, k, vaug)

    w_r = jnp.zeros((D, LANES), F32)
    w_r = w_r.at[:, :N_EXPERTS].set(w_router[0]).at[:, N_EXPERTS:N_EXPERTS + N_GROUPS].set(w_group[0])
    b_r = jnp.zeros((1, LANES), F32)
    b_r = b_r.at[0, :N_EXPERTS].set(b_router[0]).at[0, N_EXPERTS:N_EXPERTS + N_GROUPS].set(b_group[0])
    w_rh = w_r.astype(BF16)
    w_rl = (w_r - w_rh.astype(F32)).astype(BF16)

    h1, gates = _merge(
        x.reshape(T, D), u.reshape(T, POOL_WIDTH), attn.reshape(T, ATTN_WIDTH),
        row(ln_in_g), row(ln_in_b), w_g.astype(BF16), row(b_gate[0]),
        pool_w[0].astype(BF16), row(pool_scale[0]), w_proj_pool[0].astype(BF16),
        w_proj_attn[0].astype(BF16), w_out[0].astype(BF16), row(ln1_g[0]), row(ln1_b[0]),
        w_rh, w_rl, b_r, L)

    out = _moe(h1, gates, w_gate[0].astype(BF16), w_up[0].astype(BF16), w_down[0].astype(BF16),
               row(ln2_g[0]), row(ln2_b[0]))
    return out.reshape(B, L, D)
```

```python
import functools
import math

import jax
import jax.numpy as jnp
import numpy as np
from jax import lax
from jax.experimental import pallas as pl
from jax.experimental.pallas import tpu as pltpu

D_MODEL = 1024
POOL_WINDOWS = (2, 4, 8, 16)
POOL_GROUPS = 4
POOL_WIDTH = 512
POOL_GROUP_DIM = 128
N_HEADS = 8
HEAD_DIM = 64
ATTN_WIDTH = 512
KV_DIM = 64
IDX_HEADS = 8
IDX_DIM = 32
TOPK_MAX = 256
NEG_INF = float(np.float32(-1e30))
N_BRANCHES = 2
N_GROUPS = 4
EXPERTS_PER_GROUP = 8
N_EXPERTS = 32
EXPERT_FF = 256
LN_EPS = 1e-5
DEPTH = 1
DEEPNORM_ALPHA = (2.0 * DEPTH) ** 0.25

V7X_VMEM_LIMIT_BYTES = 56 * 1024 * 1024
V7X_VMEM_LIMIT_BYTES_MOE = 60 * 1024 * 1024
LANES = 128

F32 = jnp.float32
BF16 = jnp.bfloat16
I32 = jnp.int32
INT_MIN = -2 ** 31
INT_MAX = 2 ** 31 - 1

TM_PROJ = 512
TQ = 256
TK = 128
TM_MERGE = 512
POOL_HALO = 16
V_ONES_ROWS = 16
TM_MOE = 1024
MOE_CHUNK = 128
MOE_SLOTS = TM_MOE + N_GROUPS * MOE_CHUNK
GSEL_LANE = N_EXPERTS


def _layer_norm(x, g, b):
    mu = jnp.mean(x, axis=-1, keepdims=True)
    xc = x - mu
    var = jnp.mean(xc * xc, axis=-1, keepdims=True)
    return xc * lax.rsqrt(var + LN_EPS) * g + b


def _dot(a, b):
    return jnp.dot(a, b, preferred_element_type=F32)


def _dot_nt(a, b):
    return lax.dot_general(a, b, (((1,), (1,)), ((), ())), preferred_element_type=F32)


def _in_proj_kernel(x_ref, g_ref, b_ref, wu_ref, wk_ref, wik_ref, wt_ref,
                    u_ref, k_ref, ik_ref, qt_ref, vt_ref, iqt_ref, iwt_ref):
    h = _layer_norm(x_ref[...], g_ref[...], b_ref[...])
    hb = h.astype(BF16)
    u_ref[...] = _dot(hb, wu_ref[...])
    n_chunks = TM_PROJ // TK
    k_ref[...] = _dot(hb, wk_ref[...]).astype(BF16).reshape(n_chunks, TK, KV_DIM)
    ik_ref[...] = _dot(hb, wik_ref[...]).astype(BF16).reshape(n_chunks, TK, IDX_DIM)
    pt = _dot_nt(wt_ref[...], hb)
    r0 = 0
    for j in range(TM_PROJ // TQ):
        for h in range(N_HEADS):
            qt_ref[j, :, h * TQ:(h + 1) * TQ] = pt[r0 + h * HEAD_DIM:r0 + (h + 1) * HEAD_DIM,
                                                   j * TQ:(j + 1) * TQ].astype(BF16)
    r0 += ATTN_WIDTH
    for c in range(n_chunks):
        vt_ref[c, :KV_DIM, :] = pt[r0:r0 + KV_DIM, c * TK:(c + 1) * TK].astype(BF16)
        vt_ref[c, KV_DIM:, :] = jnp.ones((V_ONES_ROWS, TK), BF16)
    r0 += KV_DIM
    iqt_ref[...] = pt[r0:r0 + IDX_HEADS * IDX_DIM].astype(BF16)
    r0 += IDX_HEADS * IDX_DIM
    iwt_ref[...] = pt[r0:r0 + IDX_HEADS]


def _in_proj(x, ln_g, ln_b, wu, wk, wik, wt):
    B, L, D = x.shape
    tm = TM_PROJ
    grid = (B, L // tm)
    tok = lambda b, i: (b, i, 0)
    tokt = lambda b, i: (b, 0, i)
    chunked = lambda b, i: (b, i, 0, 0)
    const2 = lambda b, i: (0, 0)
    n_t = wt.shape[0]
    return pl.pallas_call(
        _in_proj_kernel,
        grid=grid,
        in_specs=[
            pl.BlockSpec((None, tm, D), tok),
            pl.BlockSpec((1, D), const2),
            pl.BlockSpec((1, D), const2),
            pl.BlockSpec((D, POOL_WIDTH), const2),
            pl.BlockSpec((D, KV_DIM), const2),
            pl.BlockSpec((D, IDX_DIM), const2),
            pl.BlockSpec((n_t, D), const2),
        ],
        out_specs=[
            pl.BlockSpec((None, tm, POOL_WIDTH), tok),
            pl.BlockSpec((None, tm // TK, TK, KV_DIM), chunked),
            pl.BlockSpec((None, tm // TK, TK, IDX_DIM), chunked),
            pl.BlockSpec((None, tm // TQ, HEAD_DIM, N_HEADS * TQ), chunked),
            pl.BlockSpec((None, tm // TK, KV_DIM + V_ONES_ROWS, TK), chunked),
            pl.BlockSpec((None, IDX_HEADS * IDX_DIM, tm), tokt),
            pl.BlockSpec((None, IDX_HEADS, tm), tokt),
        ],
        out_shape=[
            jax.ShapeDtypeStruct((B, L, POOL_WIDTH), F32),
            jax.ShapeDtypeStruct((B, L // TK, TK, KV_DIM), BF16),
            jax.ShapeDtypeStruct((B, L // TK, TK, IDX_DIM), BF16),
            jax.ShapeDtypeStruct((B, L // TQ, HEAD_DIM, N_HEADS * TQ), BF16),
            jax.ShapeDtypeStruct((B, L // TK, KV_DIM + V_ONES_ROWS, TK), BF16),
            jax.ShapeDtypeStruct((B, IDX_HEADS * IDX_DIM, L), BF16),
            jax.ShapeDtypeStruct((B, IDX_HEADS, L), F32),
        ],
        compiler_params=pltpu.CompilerParams(
            dimension_semantics=("arbitrary", "arbitrary"),
            vmem_limit_bytes=V7X_VMEM_LIMIT_BYTES),
        name="in_proj",
    )(x, ln_g, ln_b, wu, wk, wik, wt)


BRACKET_PASSES = 14
NO_TIE = 1e9


def _fori_pairs(n, body, init):
    def pair(j, carry):
        return body(2 * j + 1, body(2 * j, carry))
    carry = lax.fori_loop(0, n // 2, pair, init)
    return lax.fori_loop(2 * (n // 2), n, body, carry)


def _rows_to_sublanes(x, op):
    return op(x.reshape(x.shape[0] // 8, 8, TQ), axis=0)


SCAN_CHUNKS = 2
SCAN_ROWS = SCAN_CHUNKS * TK


def _dsa_kernel(iqt_ref, iwt_ref, ik_ref, qw_ref, k_ref, va_ref, o_ref,
                sc_ref, thr_ref, tie_ref, m_ref, acc_ref, lga_ref, lgb_ref, *, seq_len, n_sel):
    assert (TQ // TK) % 2 == 0
    qi = pl.program_id(1)
    q0 = qi * TQ
    nkc = (qi + 1) * (TQ // TK)
    n_beyond = seq_len - (qi + 1) * TQ
    k_sel = jnp.float32(n_sel)

    row_iota = lax.broadcasted_iota(I32, (TK, TQ), 0)
    t_idx = q0 + lax.broadcasted_iota(I32, (TK, TQ), 1)
    scan_rows = lax.broadcasted_iota(I32, (SCAN_ROWS, TQ), 0)
    scan_t = q0 + lax.broadcasted_iota(I32, (SCAN_ROWS, TQ), 1)
    n_scan = nkc // SCAN_CHUNKS

    def scores_at(b):
        return sc_ref[pl.ds(b * SCAN_CHUNKS, SCAN_CHUNKS)].reshape(SCAN_ROWS, TQ)

    def score_step(b, carry):
        smin, smax = carry
        ikc = ik_ref[pl.ds(b * SCAN_CHUNKS, SCAN_CHUNKS)].reshape(SCAN_ROWS, IDX_DIM)
        score = jnp.zeros((SCAN_ROWS, TQ), F32)
        for h in range(IDX_HEADS):
            lg = _dot(ikc, iqt_ref[h * IDX_DIM:(h + 1) * IDX_DIM, :])
            score = score + iwt_ref[h:h + 1, :] * jnp.maximum(lg, 0.0)
        smin = jnp.minimum(smin, jnp.min(score, axis=0, keepdims=True))
        smax = jnp.maximum(smax, jnp.max(score, axis=0, keepdims=True))
        score = jnp.where(b * SCAN_ROWS + scan_rows <= scan_t, score, NEG_INF)
        sc_ref[pl.ds(b * SCAN_CHUNKS, SCAN_CHUNKS)] = score.reshape(SCAN_CHUNKS, TK, TQ)
        return smin, smax

    big = jnp.float32(3e38)
    smin, smax = _fori_pairs(n_scan, score_step,
                             (jnp.full((1, TQ), big, F32), jnp.full((1, TQ), -big, F32)))

    nb_f = n_beyond.astype(F32)
    ninf = jnp.float32(-jnp.inf)

    def count_ge(cand):
        def body(b, acc):
            return acc + _rows_to_sublanes(jnp.where(scores_at(b) >= cand, 1.0, 0.0), jnp.sum)
        acc = _fori_pairs(n_scan, body, jnp.zeros((8, TQ), F32))
        return jnp.sum(acc, axis=0, keepdims=True) + jnp.where(cand <= NEG_INF, nb_f, 0.0)

    def max_below(h):
        def body(b, acc):
            s = scores_at(b)
            return jnp.maximum(acc, _rows_to_sublanes(jnp.where(s < h, s, ninf), jnp.max))
        acc = _fori_pairs(n_scan, body, jnp.full((8, TQ), ninf, F32))
        return jnp.max(acc, axis=0, keepdims=True)

    def count_ge_and_max_below(v):
        def body(b, carry):
            acc, mx = carry
            s = scores_at(b)
            acc = acc + _rows_to_sublanes(jnp.where(s >= v, 1.0, 0.0), jnp.sum)
            mx = jnp.maximum(mx, _rows_to_sublanes(jnp.where(s < v, s, ninf), jnp.max))
            return acc, mx
        acc, mx = _fori_pairs(n_scan, body,
                              (jnp.zeros((8, TQ), F32), jnp.full((8, TQ), ninf, F32)))
        cnt = jnp.sum(acc, axis=0, keepdims=True) + jnp.where(v <= NEG_INF, nb_f, 0.0)
        return cnt, jnp.max(mx, axis=0, keepdims=True)

    n_adm = (q0 + 1 + lax.broadcasted_iota(I32, (1, TQ), 1)).astype(F32)
    few = n_adm < k_sel
    lo0 = jnp.where(few, NEG_INF, smin)
    clo0 = jnp.where(few, k_sel, jnp.where(smin <= NEG_INF, jnp.float32(seq_len), n_adm))
    hi0 = smax + (jnp.abs(smax) * 1e-6 + 1e-30)
    chi0 = jnp.zeros((1, TQ), F32)

    def bracket_body(it, st):
        lo, hi, clo, chi, flo, fhi, side = st
        done = clo == k_sel
        frac = jnp.clip(flo / (flo - fhi), 1.0 / 512, 511.0 / 512)
        frac = jnp.where(clo - chi <= 2.0, 0.5, frac)
        cand = lo + (hi - lo) * frac
        zero_inside = jnp.logical_and(jnp.logical_and(lo < 0.0, hi > 0.0), it == 0)
        cand = jnp.where(done, lo, jnp.where(zero_inside, 0.0, cand))
        cnt = count_ge(cand)
        ge = cnt >= k_sel
        f = cnt - (k_sel - 0.5)
        new_side = jnp.where(ge, 1.0, -1.0)
        same = new_side == side
        flo_n = jnp.where(ge, f, jnp.where(same, flo * 0.5, flo))
        fhi_n = jnp.where(ge, jnp.where(same, fhi * 0.5, fhi), f)
        up_lo = jnp.logical_and(jnp.logical_not(done), ge)
        up_hi = jnp.logical_and(jnp.logical_not(done), jnp.logical_not(ge))
        lo = jnp.where(up_lo, cand, lo)
        clo = jnp.where(up_lo, cnt, clo)
        hi = jnp.where(up_hi, cand, hi)
        chi = jnp.where(up_hi, cnt, chi)
        flo = jnp.where(done, flo, flo_n)
        fhi = jnp.where(done, fhi, fhi_n)
        side = jnp.where(done, side, new_side)
        return lo, hi, clo, chi, flo, fhi, side

    lo, hi, clo, chi = lax.fori_loop(
        0, BRACKET_PASSES, bracket_body,
        (lo0, hi0, clo0, chi0, clo0 - (k_sel - 0.5), chi0 - (k_sel - 0.5),
         jnp.zeros((1, TQ), F32)))[:4]
    pending = jnp.sum(jnp.where(clo == k_sel, 0.0, 1.0))
    thr_ref[...] = lo
    tie_ref[...] = jnp.full((1, TQ), NO_TIE, F32)

    @pl.when(pending > 0.5)
    def _():
        fin0 = jnp.where(clo == k_sel, 1.0, 0.0)
        v0 = max_below(hi)

        def fin_cond(st):
            return jnp.logical_and(st[0] < seq_len + 2, st[1] > 0.5)

        def fin_body(st):
            j, _, fin, h, ch, v, kst, need = st
            cnt, v2 = count_ge_and_max_below(v)
            hit = jnp.logical_and(fin < 0.5, cnt >= k_sel)
            kst = jnp.where(hit, v, kst)
            need = jnp.where(jnp.logical_and(hit, cnt > k_sel), k_sel - ch, need)
            fin = jnp.where(hit, 1.0, fin)
            open_ = fin < 0.5
            h = jnp.where(open_, v, h)
            ch = jnp.where(open_, cnt, ch)
            v = jnp.where(open_, v2, v)
            return j + 1, jnp.sum(1.0 - fin), fin, h, ch, v, kst, need

        st2 = lax.while_loop(
            fin_cond, fin_body,
            (jnp.int32(0), pending, fin0, hi, chi, v0, lo, jnp.full((1, TQ), NO_TIE, F32)))
        thr_ref[...] = st2[6]
        tie_ref[...] = st2[7]

    kstar = thr_ref[...]
    need = tie_ref[...]
    fix = jnp.logical_or(need < NO_TIE, kstar <= NEG_INF)

    @pl.when(jnp.sum(jnp.where(fix, 1.0, 0.0)) > 0.5)
    def _():
        tri = jnp.where(lax.broadcasted_iota(I32, (TK, TK), 0) >= lax.broadcasted_iota(I32, (TK, TK), 1),
                        1.0, 0.0).astype(BF16)

        def rewrite(c, seen):
            s = sc_ref[c]
            tied = jnp.where(s == kstar, 1.0, 0.0)
            rank = seen + _dot(tri, tied.astype(BF16))
            tie_take = jnp.where(rank <= need, 2.0 * tied - 1.0, -1.0)
            take = jnp.where(s > kstar, 1.0, tie_take)
            sc_ref[c] = jnp.where(c * TK + row_iota <= t_idx, take, -1.0)
            return rank[TK - 1:TK, :]

        lax.fori_loop(0, nkc, rewrite, jnp.zeros((1, TQ), F32))
        thr_ref[...] = jnp.zeros((1, TQ), F32)

    m_ref[...] = jnp.full(m_ref.shape, NEG_INF, F32)
    acc_ref[...] = jnp.zeros(acc_ref.shape, F32)
    thr = thr_ref[...]

    def masked_logits(c):
        bias = jnp.where(sc_ref[c] >= thr, 0.0, NEG_INF)
        lg = _dot(k_ref[c], qw_ref[...])
        lg = jnp.concatenate(
            [lg[:, h * TQ:(h + 1) * TQ] + bias for h in range(N_HEADS)], axis=1)
        return lg, jnp.max(lg, axis=0, keepdims=True)

    def attend(c, lg, lg_max):
        m_old = m_ref[...]
        m_new = jnp.maximum(m_old, lg_max)
        alpha = jnp.exp2(m_old - m_new)
        p = jnp.exp2(lg - m_new).astype(BF16)
        acc_ref[...] = alpha * acc_ref[...] + _dot(va_ref[c], p)
        m_ref[...] = m_new

    def attend_pair(i, max_a):
        c = 2 * i
        lg_b, max_b = masked_logits(c + 1)
        lgb_ref[...] = lg_b
        attend(c, lga_ref[...], max_a)
        lg_a, max_a = masked_logits(jnp.minimum(c + 2, nkc - 1))
        attend(c + 1, lgb_ref[...], max_b)
        lga_ref[...] = lg_a
        return max_a

    lg_a, max_a = masked_logits(0)
    lga_ref[...] = lg_a
    lax.fori_loop(0, nkc // 2, attend_pair, max_a)

    a = acc_ref[...]
    o = a[:HEAD_DIM] / a[HEAD_DIM:HEAD_DIM + 1]
    o = jnp.concatenate([o[:, h * TQ:(h + 1) * TQ] for h in range(N_HEADS)], axis=0)
    o_ref[...] = o.T


def _dsa_attention(iqt, iwt, ik, qw, k, vaug):
    B, _, L = iqt.shape
    n_sel = min(TOPK_MAX, L // 4)
    nch = L // TK
    va_rows = vaug.shape[2]
    grid = (B, L // TQ)
    kern = functools.partial(_dsa_kernel, seq_len=L, n_sel=n_sel)
    return pl.pallas_call(
        kern,
        grid=grid,
        in_specs=[
            pl.BlockSpec((None, IDX_HEADS * IDX_DIM, TQ), lambda b, i: (b, 0, i)),
            pl.BlockSpec((None, IDX_HEADS, TQ), lambda b, i: (b, 0, i)),
            pl.BlockSpec((None, nch, TK, IDX_DIM), lambda b, i: (b, 0, 0, 0)),
            pl.BlockSpec((None, None, HEAD_DIM, N_HEADS * TQ), lambda b, i: (b, i, 0, 0)),
            pl.BlockSpec((None, nch, TK, HEAD_DIM), lambda b, i: (b, 0, 0, 0)),
            pl.BlockSpec((None, nch, va_rows, TK), lambda b, i: (b, 0, 0, 0)),
        ],
        out_specs=pl.BlockSpec((None, TQ, ATTN_WIDTH), lambda b, i: (b, i, 0)),
        out_shape=jax.ShapeDtypeStruct((B, L, ATTN_WIDTH), F32),
        scratch_shapes=[
            pltpu.VMEM((nch, TK, TQ), F32),
            pltpu.VMEM((1, TQ), F32),
            pltpu.VMEM((1, TQ), F32),
            pltpu.VMEM((1, N_HEADS * TQ), F32),
            pltpu.VMEM((va_rows, N_HEADS * TQ), F32),
            pltpu.VMEM((TK, N_HEADS * TQ), F32),
            pltpu.VMEM((TK, N_HEADS * TQ), F32),
        ],
        compiler_params=pltpu.CompilerParams(
            dimension_semantics=("arbitrary", "arbitrary"),
            vmem_limit_bytes=V7X_VMEM_LIMIT_BYTES),
        name="dsa_attn",
    )(iqt, iwt, ik, qw, k, vaug)


def _merge_kernel(x_ref, u_ref, uh_ref, a_ref, lng_ref, lnb_ref, wgate_ref, bgate_ref,
                  poolw_ref, pscale_ref, wpp_ref, wpa_ref, wout_ref, ln1g_ref, ln1b_ref,
                  wrh_ref, wrl_ref, br_ref, h1_ref, gates_ref, *, tiles_per_seq):
    tm = TM_MERGE
    i = pl.program_id(0)
    seq_start = (i % tiles_per_seq) == 0
    h = _layer_norm(x_ref[...], lng_ref[...], lnb_ref[...])
    hb = h.astype(BF16)

    halo = jnp.where(seq_start, 0.0, uh_ref[...])
    u = u_ref[...]
    ext = jnp.concatenate([halo, u], axis=0)
    pos = (i % tiles_per_seq) * tm + lax.broadcasted_iota(I32, (tm, 1), 0)
    mixed = []
    for g, w in enumerate(POOL_WINDOWS):
        s = ext[:, g * POOL_GROUP_DIM:(g + 1) * POOL_GROUP_DIM]
        span = 1
        while span < w:
            s = s + pltpu.roll(s, span, 0)
            span *= 2
        win = s[POOL_HALO:]
        cnt = jnp.minimum(pos + 1, w).astype(F32)
        ug = u[:, g * POOL_GROUP_DIM:(g + 1) * POOL_GROUP_DIM]
        delta = win / cnt - ug
        mixed.append(_dot(delta.astype(BF16), poolw_ref[g]))
    pool_out = jnp.concatenate(mixed, axis=1) * pscale_ref[...]

    gate_pre = _dot(hb, wgate_ref[...]) + bgate_ref[...]
    gates = jax.nn.sigmoid(gate_pre)
    bp = _dot(pool_out.astype(BF16), wpp_ref[...])
    ba = _dot(a_ref[...].astype(BF16), wpa_ref[...])
    merged = gates[:, :D_MODEL] * bp + gates[:, D_MODEL:] * ba
    mix = _dot(merged.astype(BF16), wout_ref[...])
    h1 = _layer_norm(DEEPNORM_ALPHA * h + mix, ln1g_ref[...], ln1b_ref[...])
    h1_ref[...] = h1

    hi = h1.astype(BF16)
    lo = (h1 - hi.astype(F32)).astype(BF16)
    lg = _dot(hi, wrh_ref[...]) + (_dot(lo, wrh_ref[...]) + _dot(hi, wrl_ref[...])) + br_ref[...]
    lane = lax.broadcasted_iota(I32, (tm, LANES), 1).astype(F32)
    big = jnp.float32(1 << 20)
    ninf = jnp.float32(-jnp.inf)
    is_g = jnp.logical_and(lane >= N_EXPERTS, lane < N_EXPERTS + N_GROUPS)
    gl = jnp.where(is_g, lg, ninf)
    gmax = jnp.max(gl, axis=1, keepdims=True)
    gsel = jnp.min(jnp.where(gl == gmax, lane, big), axis=1, keepdims=True) - N_EXPERTS
    sumexp = jnp.sum(jnp.where(is_g, jnp.exp(gl - gmax), 0.0), axis=1, keepdims=True)
    p_group = 1.0 / sumexp
    e_lo = gsel * EXPERTS_PER_GROUP
    in_grp = jnp.logical_and(lane >= e_lo, lane < e_lo + EXPERTS_PER_GROUP)
    el = jnp.where(in_grp, lg, ninf)
    m1 = jnp.max(el, axis=1, keepdims=True)
    i1 = jnp.min(jnp.where(el == m1, lane, big), axis=1, keepdims=True)
    el2 = jnp.where(lane == i1, ninf, el)
    m2 = jnp.max(el2, axis=1, keepdims=True)
    i2 = jnp.min(jnp.where(el2 == m2, lane, big), axis=1, keepdims=True)
    e2 = jnp.exp(m2 - m1)
    den = 1.0 + e2
    w1 = (1.0 / den) * p_group
    w2 = (e2 / den) * p_group
    dense = jnp.where(lane == i1, w1, 0.0) + jnp.where(lane == i2, w2, 0.0)
    gates_ref[...] = jnp.where(lane == float(GSEL_LANE), gsel, dense)


def _merge(x, u, attn, ln_g, ln_b, wgate, bgate, poolw, pscale, wpp, wpa, wout, ln1g, ln1b,
           wrh, wrl, br, seq_len):
    T, D = x.shape
    tm = TM_MERGE
    tiles_per_seq = seq_len // tm
    grid = (T // tm,)
    tok = lambda i: (i, 0)
    c2 = lambda i: (0, 0)
    c3 = lambda i: (0, 0, 0)
    halo_blocks = tm // POOL_HALO
    kern = functools.partial(_merge_kernel, tiles_per_seq=tiles_per_seq)
    return pl.pallas_call(
        kern,
        grid=grid,
        in_specs=[
            pl.BlockSpec((tm, D), tok),
            pl.BlockSpec((tm, POOL_WIDTH), tok),
            pl.BlockSpec((POOL_HALO, POOL_WIDTH),
                         lambda i: (jnp.maximum(i * halo_blocks - 1, 0), 0)),
            pl.BlockSpec((tm, ATTN_WIDTH), tok),
            pl.BlockSpec((1, D), c2),
            pl.BlockSpec((1, D), c2),
            pl.BlockSpec((D, N_BRANCHES * D), c2),
            pl.BlockSpec((1, N_BRANCHES * D), c2),
            pl.BlockSpec((POOL_GROUPS, POOL_GROUP_DIM, POOL_GROUP_DIM), c3),
            pl.BlockSpec((1, POOL_WIDTH), c2),
            pl.BlockSpec((POOL_WIDTH, D), c2),
            pl.BlockSpec((ATTN_WIDTH, D), c2),
            pl.BlockSpec((D, D), c2),
            pl.BlockSpec((1, D), c2),
            pl.BlockSpec((1, D), c2),
            pl.BlockSpec((D, LANES), c2),
            pl.BlockSpec((D, LANES), c2),
            pl.BlockSpec((1, LANES), c2),
        ],
        out_specs=[
            pl.BlockSpec((tm, D), tok),
            pl.BlockSpec((tm, LANES), tok),
        ],
        out_shape=[
            jax.ShapeDtypeStruct((T, D), F32),
            jax.ShapeDtypeStruct((T, LANES), F32),
        ],
        compiler_params=pltpu.CompilerParams(
            dimension_semantics=("arbitrary",),
            vmem_limit_bytes=V7X_VMEM_LIMIT_BYTES),
        name="merge",
    )(x, u, u, attn, ln_g, ln_b, wgate, bgate, poolw, pscale, wpp, wpa, wout, ln1g, ln1b,
      wrh, wrl, br)


def _snake_group(tile, step):
    return jnp.where(tile % 2 == 0, step, N_GROUPS - 1 - step)


def _moe_kernel(h1_ref, g_ref, tri_ref, wg_ref, wu_ref, wd_ref, ln2g_ref, ln2b_ref, o_ref,
                xs_ref, ys_ref, dcol_ref, tab_ref):
    tm = TM_MOE
    step = pl.program_id(1)
    g = _snake_group(pl.program_id(0), step)
    lane = lax.broadcasted_iota(I32, (tm, LANES), 1).astype(F32)
    lane1 = lax.broadcasted_iota(I32, (1, LANES), 1).astype(F32)

    @pl.when(step == 0)
    def _():
        rec = g_ref[...]
        gsel = jnp.sum(jnp.where(lane == float(GSEL_LANE), rec, 0.0), axis=1, keepdims=True)
        onehot = jnp.where(lane == gsel, 1.0, 0.0)
        cum = _dot(tri_ref[...], onehot.astype(BF16))
        cnt = cum[tm - 1:tm, :]
        padded = jnp.floor((cnt + (MOE_CHUNK - 1)) * (1.0 / MOE_CHUNK)) * MOE_CHUNK
        start = jnp.zeros((1, LANES), F32)
        for gg in range(N_GROUPS - 1):
            p = jnp.sum(jnp.where(lane1 == float(gg), padded, 0.0), axis=1, keepdims=True)
            start = start + jnp.where(lane1 > float(gg), p, 0.0)
        tab_ref[0:1, :] = start
        tab_ref[1:2, :] = padded
        slot = jnp.sum(onehot * (start + cum - 1.0), axis=1, keepdims=True)
        slot_b = jnp.broadcast_to(slot, (tm, LANES))
        dcol_ref[...] = slot_b
        slot_row = slot_b.T[0:1, :]

        dense = jnp.where(lane < float(N_EXPERTS), rec, 0.0)
        own = jnp.where(gsel == 0.0, dense, 0.0)
        for gg in range(1, N_GROUPS):
            own = own + jnp.where(gsel == float(gg),
                                  pltpu.roll(dense, LANES - gg * EXPERTS_PER_GROUP, 1), 0.0)
        own = jnp.where(lane < float(EXPERTS_PER_GROUP), own, 0.0)
        own_hi = own.astype(BF16).astype(F32)
        side = (own_hi + pltpu.roll(own - own_hi, EXPERTS_PER_GROUP, 1)).astype(BF16)
        xa = jnp.concatenate([h1_ref[...].astype(BF16), side], axis=1)

        blk = 256
        for rb in range(MOE_SLOTS // blk):
            rows = rb * blk + lax.broadcasted_iota(I32, (blk, tm), 0).astype(F32)
            perm = jnp.where(rows == slot_row, 1.0, 0.0).astype(BF16)
            xs_ref[rb * blk:(rb + 1) * blk, :] = _dot(perm, xa).astype(BF16)
        ys_ref[...] = jnp.zeros(ys_ref.shape, BF16)

    gf = g.astype(F32)
    first = jnp.sum(jnp.where(lane1 == gf, tab_ref[0:1, :], 0.0)).astype(I32)
    n_chunks = jnp.sum(jnp.where(lane1 == gf, tab_ref[1:2, :], 0.0)).astype(I32) // MOE_CHUNK
    ff = EXPERTS_PER_GROUP * EXPERT_FF

    def ffn_chunk(c, carry):
        r0 = pl.multiple_of(first + c * MOE_CHUNK, MOE_CHUNK)
        xc = xs_ref[pl.ds(r0, MOE_CHUNK), :]
        xb = xc[:, :D_MODEL]
        gl = xc[:, D_MODEL:].astype(F32)
        gates = gl[:, :EXPERTS_PER_GROUP] + gl[:, EXPERTS_PER_GROUP:2 * EXPERTS_PER_GROUP]
        parts = []
        for j in range(EXPERTS_PER_GROUP):
            hid = jax.nn.silu(_dot(xb, wg_ref[j])) * _dot(xb, wu_ref[j])
            parts.append((hid * gates[:, j:j + 1]).astype(BF16))
        y = _dot(jnp.concatenate(parts, axis=1), wd_ref[...].reshape(ff, D_MODEL))
        ys_ref[pl.ds(r0, MOE_CHUNK), :] = y.astype(BF16)
        return carry

    lax.fori_loop(0, n_chunks, ffn_chunk, 0)

    @pl.when(step == N_GROUPS - 1)
    def _():
        slots = lax.broadcasted_iota(I32, (tm, MOE_SLOTS), 1).astype(F32)
        back = jnp.where(slots == dcol_ref[:, 0:1], 1.0, 0.0).astype(BF16)
        ffn = _dot(back, ys_ref[...])
        o_ref[...] = _layer_norm(DEEPNORM_ALPHA * h1_ref[...] + ffn, ln2g_ref[...], ln2b_ref[...])


def _moe(h1, route, wg, wu, wd, ln2g, ln2b):
    T, D = h1.shape
    tm = TM_MOE
    epg = EXPERTS_PER_GROUP
    grid = (T // tm, N_GROUPS)
    once = pl.Buffered(1)
    return pl.pallas_call(
        _moe_kernel,
        grid=grid,
        in_specs=[
            pl.BlockSpec((tm, D), lambda i, s: (i, 0), pipeline_mode=once),
            pl.BlockSpec((tm, LANES), lambda i, s: (i, 0), pipeline_mode=once),
            pl.BlockSpec((tm, tm), lambda i, s: (0, 0), pipeline_mode=once),
            pl.BlockSpec((epg, D, EXPERT_FF), lambda i, s: (_snake_group(i, s), 0, 0)),
            pl.BlockSpec((epg, D, EXPERT_FF), lambda i, s: (_snake_group(i, s), 0, 0)),
            pl.BlockSpec((epg, EXPERT_FF, D), lambda i, s: (_snake_group(i, s), 0, 0)),
            pl.BlockSpec((1, D), lambda i, s: (0, 0)),
            pl.BlockSpec((1, D), lambda i, s: (0, 0)),
        ],
        out_specs=pl.BlockSpec((tm, D), lambda i, s: (i, 0)),
        out_shape=jax.ShapeDtypeStruct((T, D), F32),
        scratch_shapes=[
            pltpu.VMEM((MOE_SLOTS, D + LANES), BF16),
            pltpu.VMEM((MOE_SLOTS, D), BF16),
            pltpu.VMEM((tm, LANES), F32),
            pltpu.VMEM((8, LANES), F32),
        ],
        compiler_params=pltpu.CompilerParams(
            dimension_semantics=("arbitrary", "arbitrary"),
            vmem_limit_bytes=V7X_VMEM_LIMIT_BYTES_MOE),
        name="moe",
    )(h1, route, jnp.tril(jnp.ones((tm, tm), BF16)), wg, wu, wd, ln2g, ln2b)


def kernel(x, ln_in_g, ln_in_b, w_in, b_gate, pool_w, pool_scale, w_proj_pool, w_proj_attn, w_out,
           ln1_g, ln1_b, w_group, b_group, w_router, b_router, w_gate, w_up, w_down, ln2_g, ln2_b):
    B, L, D = x.shape
    assert D == D_MODEL and w_in.shape[0] == DEPTH == 1
    assert L % TQ == 0 and L % TM_MERGE == 0 and L % TM_PROJ == 0
    assert L <= 4096
    T = B * L
    row = lambda v: v.reshape(1, -1).astype(F32)

    w = w_in[0]
    o = 0
    w_u = w[:, o:o + POOL_WIDTH]; o += POOL_WIDTH
    w_q = w[:, o:o + ATTN_WIDTH]; o += ATTN_WIDTH
    w_k = w[:, o:o + KV_DIM]; o += KV_DIM
    w_v = w[:, o:o + KV_DIM]; o += KV_DIM
    w_iq = w[:, o:o + IDX_HEADS * IDX_DIM]; o += IDX_HEADS * IDX_DIM
    w_ik = w[:, o:o + IDX_DIM]; o += IDX_DIM
    w_iw = w[:, o:o + IDX_HEADS]; o += IDX_HEADS
    w_g = w[:, o:]
    sm_scale = math.log2(math.e) / math.sqrt(HEAD_DIM)
    bf16_rows = 16
    w_t = jnp.concatenate([w_q * sm_scale, w_v, w_iq, w_iw,
                           jnp.zeros((D, bf16_rows - IDX_HEADS), F32)], axis=1).T.astype(BF16)

    u, k, ik, qw, vaug, iqt, iwt = _in_proj(
        x, row(ln_in_g), row(ln_in_b), w_u.astype(BF16), w_k.astype(BF16), w_ik.astype(BF16), w_t)
    attn = _dsa_attention(iqt, iwt, ik, qw, k, vaug)

    w_r = jnp.zeros((D, LANES), F32)
    w_r = w_r.at[:, :N_EXPERTS].set(w_router[0]).at[:, N_EXPERTS:N_EXPERTS + N_GROUPS].set(w_group[0])
    b_r = jnp.zeros((1, LANES), F32)
    b_r = b_r.at[0, :N_EXPERTS].set(b_router[0]).at[0, N_EXPERTS:N_EXPERTS + N_GROUPS].set(b_group[0])
    w_rh = w_r.astype(BF16)
    w_rl = (w_r - w_rh.astype(F32)).astype(BF16)

    h1, gates = _merge(
        x.reshape(T, D), u.reshape(T, POOL_WIDTH), attn.reshape(T, ATTN_WIDTH),
        row(ln_in_g), row(ln_in_b), w_g.astype(BF16), row(b_gate[0]),
        pool_w[0].astype(BF16), row(pool_scale[0]), w_proj_pool[0].astype(BF16),
        w_proj_attn[0].astype(BF16), w_out[0].astype(BF16), row(ln1_g[0]), row(ln1_b[0]),
        w_rh, w_rl, b_r, L)

    out = _moe(h1, gates, w_gate[0].astype(BF16), w_up[0].astype(BF16), w_down[0].astype(BF16),
               row(ln2_g[0]), row(ln2_b[0]))
    return out.reshape(B, L, D)
```

```python
import functools
import math

import jax
import jax.numpy as jnp
import numpy as np
from jax import lax
from jax.experimental import pallas as pl
from jax.experimental.pallas import tpu as pltpu

D_MODEL = 1024
POOL_WINDOWS = (2, 4, 8, 16)
POOL_GROUPS = 4
POOL_WIDTH = 512
POOL_GROUP_DIM = 128
N_HEADS = 8
HEAD_DIM = 64
ATTN_WIDTH = 512
KV_DIM = 64
IDX_HEADS = 8
IDX_DIM = 32
TOPK_MAX = 256
NEG_INF = float(np.float32(-1e30))
N_BRANCHES = 2
N_GROUPS = 4
EXPERTS_PER_GROUP = 8
N_EXPERTS = 32
EXPERT_FF = 256
LN_EPS = 1e-5
DEPTH = 1
DEEPNORM_ALPHA = (2.0 * DEPTH) ** 0.25

V7X_VMEM_LIMIT_BYTES = 56 * 1024 * 1024
V7X_VMEM_LIMIT_BYTES_MOE = 60 * 1024 * 1024
LANES = 128

F32 = jnp.float32
BF16 = jnp.bfloat16
I32 = jnp.int32

TM_PROJ = 512
TQ = 256
TK = 128
TM_MERGE = 512
POOL_HALO = 16
V_ONES_ROWS = 16
TM_MOE = 1024
MOE_CHUNK = 128
MOE_SLOTS = TM_MOE + N_GROUPS * MOE_CHUNK
MOE_SORT_ROWS = 256
GSEL_LANE = N_EXPERTS


def _layer_norm(x, g, b):
    mu = jnp.mean(x, axis=-1, keepdims=True)
    xc = x - mu
    var = jnp.mean(xc * xc, axis=-1, keepdims=True)
    return xc * lax.rsqrt(var + LN_EPS) * g + b


def _dot(a, b):
    return jnp.dot(a, b, preferred_element_type=F32)


def _dot_nt(a, b):
    return lax.dot_general(a, b, (((1,), (1,)), ((), ())), preferred_element_type=F32)


def _in_proj_kernel(x_ref, g_ref, b_ref, wu_ref, wk_ref, wik_ref, wt_ref,
                    u_ref, k_ref, ik_ref, qt_ref, vt_ref, iqt_ref, iwt_ref):
    h = _layer_norm(x_ref[...], g_ref[...], b_ref[...])
    hb = h.astype(BF16)
    u_ref[...] = _dot(hb, wu_ref[...])
    n_chunks = TM_PROJ // TK
    k_ref[...] = _dot(hb, wk_ref[...]).astype(BF16).reshape(n_chunks, TK, KV_DIM)
    ik_ref[...] = _dot(hb, wik_ref[...]).astype(BF16).reshape(n_chunks, TK, IDX_DIM)
    pt = _dot_nt(wt_ref[...], hb)
    r0 = 0
    for j in range(TM_PROJ // TQ):
        for h in range(N_HEADS):
            qt_ref[j, :, h * TQ:(h + 1) * TQ] = pt[r0 + h * HEAD_DIM:r0 + (h + 1) * HEAD_DIM,
                                                   j * TQ:(j + 1) * TQ].astype(BF16)
    r0 += ATTN_WIDTH
    for c in range(n_chunks):
        vt_ref[c, :KV_DIM, :] = pt[r0:r0 + KV_DIM, c * TK:(c + 1) * TK].astype(BF16)
        vt_ref[c, KV_DIM:, :] = jnp.ones((V_ONES_ROWS, TK), BF16)
    r0 += KV_DIM
    iqt_ref[...] = pt[r0:r0 + IDX_HEADS * IDX_DIM].astype(BF16)
    r0 += IDX_HEADS * IDX_DIM
    iwt_ref[...] = pt[r0:r0 + IDX_HEADS]


def _in_proj(x, ln_g, ln_b, wu, wk, wik, wt):
    B, L, D = x.shape
    tm = TM_PROJ
    grid = (B, L // tm)
    tok = lambda b, i: (b, i, 0)
    tokt = lambda b, i: (b, 0, i)
    chunked = lambda b, i: (b, i, 0, 0)
    const2 = lambda b, i: (0, 0)
    n_t = wt.shape[0]
    return pl.pallas_call(
        _in_proj_kernel,
        grid=grid,
        in_specs=[
            pl.BlockSpec((None, tm, D), tok),
            pl.BlockSpec((1, D), const2),
            pl.BlockSpec((1, D), const2),
            pl.BlockSpec((D, POOL_WIDTH), const2),
            pl.BlockSpec((D, KV_DIM), const2),
            pl.BlockSpec((D, IDX_DIM), const2),
            pl.BlockSpec((n_t, D), const2),
        ],
        out_specs=[
            pl.BlockSpec((None, tm, POOL_WIDTH), tok),
            pl.BlockSpec((None, tm // TK, TK, KV_DIM), chunked),
            pl.BlockSpec((None, tm // TK, TK, IDX_DIM), chunked),
            pl.BlockSpec((None, tm // TQ, HEAD_DIM, N_HEADS * TQ), chunked),
            pl.BlockSpec((None, tm // TK, KV_DIM + V_ONES_ROWS, TK), chunked),
            pl.BlockSpec((None, IDX_HEADS * IDX_DIM, tm), tokt),
            pl.BlockSpec((None, IDX_HEADS, tm), tokt),
        ],
        out_shape=[
            jax.ShapeDtypeStruct((B, L, POOL_WIDTH), F32),
            jax.ShapeDtypeStruct((B, L // TK, TK, KV_DIM), BF16),
            jax.ShapeDtypeStruct((B, L // TK, TK, IDX_DIM), BF16),
            jax.ShapeDtypeStruct((B, L // TQ, HEAD_DIM, N_HEADS * TQ), BF16),
            jax.ShapeDtypeStruct((B, L // TK, KV_DIM + V_ONES_ROWS, TK), BF16),
            jax.ShapeDtypeStruct((B, IDX_HEADS * IDX_DIM, L), BF16),
            jax.ShapeDtypeStruct((B, IDX_HEADS, L), F32),
        ],
        compiler_params=pltpu.CompilerParams(
            dimension_semantics=("arbitrary", "arbitrary"),
            vmem_limit_bytes=V7X_VMEM_LIMIT_BYTES),
        name="in_proj",
    )(x, ln_g, ln_b, wu, wk, wik, wt)


BRACKET_PASSES = 14
NO_TIE = 1e9


def _fori_pairs(n, body, init):
    def pair(j, carry):
        return body(2 * j + 1, body(2 * j, carry))
    carry = lax.fori_loop(0, n // 2, pair, init)
    return lax.fori_loop(2 * (n // 2), n, body, carry)


def _rows_to_sublanes(x, op):
    return op(x.reshape(x.shape[0] // 8, 8, TQ), axis=0)


SCAN_CHUNKS = 2
SCAN_ROWS = SCAN_CHUNKS * TK


def _dsa_kernel(iqt_ref, iwt_ref, ik_ref, qw_ref, k_ref, va_ref, o_ref,
                sc_ref, thr_ref, tie_ref, m_ref, acc_ref, *lg_refs, seq_len, n_sel):
    assert (TQ // TK) % 2 == 0
    qi = pl.program_id(1)
    q0 = qi * TQ
    nkc = (qi + 1) * (TQ // TK)
    n_beyond = seq_len - (qi + 1) * TQ
    k_sel = jnp.float32(n_sel)

    row_iota = lax.broadcasted_iota(I32, (TK, TQ), 0)
    t_idx = q0 + lax.broadcasted_iota(I32, (TK, TQ), 1)
    scan_rows = lax.broadcasted_iota(I32, (SCAN_ROWS, TQ), 0)
    scan_t = q0 + lax.broadcasted_iota(I32, (SCAN_ROWS, TQ), 1)
    n_scan = nkc // SCAN_CHUNKS

    def scores_at(b):
        return sc_ref[pl.ds(b * SCAN_CHUNKS, SCAN_CHUNKS)].reshape(SCAN_ROWS, TQ)

    def score_step(b, carry):
        smin, smax = carry
        ikc = ik_ref[pl.ds(b * SCAN_CHUNKS, SCAN_CHUNKS)].reshape(SCAN_ROWS, IDX_DIM)
        score = jnp.zeros((SCAN_ROWS, TQ), F32)
        for h in range(IDX_HEADS):
            lg = _dot(ikc, iqt_ref[h * IDX_DIM:(h + 1) * IDX_DIM, :])
            score = score + iwt_ref[h:h + 1, :] * jnp.maximum(lg, 0.0)
        smin = jnp.minimum(smin, jnp.min(score, axis=0, keepdims=True))
        smax = jnp.maximum(smax, jnp.max(score, axis=0, keepdims=True))
        score = jnp.where(b * SCAN_ROWS + scan_rows <= scan_t, score, NEG_INF)
        sc_ref[pl.ds(b * SCAN_CHUNKS, SCAN_CHUNKS)] = score.reshape(SCAN_CHUNKS, TK, TQ)
        return smin, smax

    big = jnp.float32(3e38)
    smin, smax = _fori_pairs(n_scan, score_step,
                             (jnp.full((1, TQ), big, F32), jnp.full((1, TQ), -big, F32)))

    nb_f = n_beyond.astype(F32)
    ninf = jnp.float32(-jnp.inf)

    def count_ge(cand):
        def body(b, acc):
            return acc + _rows_to_sublanes(jnp.where(scores_at(b) >= cand, 1.0, 0.0), jnp.sum)
        acc = _fori_pairs(n_scan, body, jnp.zeros((8, TQ), F32))
        return jnp.sum(acc, axis=0, keepdims=True) + jnp.where(cand <= NEG_INF, nb_f, 0.0)

    def max_below(h):
        def body(b, acc):
            s = scores_at(b)
            return jnp.maximum(acc, _rows_to_sublanes(jnp.where(s < h, s, ninf), jnp.max))
        acc = _fori_pairs(n_scan, body, jnp.full((8, TQ), ninf, F32))
        return jnp.max(acc, axis=0, keepdims=True)

    def count_ge_and_max_below(v):
        def body(b, carry):
            acc, mx = carry
            s = scores_at(b)
            acc = acc + _rows_to_sublanes(jnp.where(s >= v, 1.0, 0.0), jnp.sum)
            mx = jnp.maximum(mx, _rows_to_sublanes(jnp.where(s < v, s, ninf), jnp.max))
            return acc, mx
        acc, mx = _fori_pairs(n_scan, body,
                              (jnp.zeros((8, TQ), F32), jnp.full((8, TQ), ninf, F32)))
        cnt = jnp.sum(acc, axis=0, keepdims=True) + jnp.where(v <= NEG_INF, nb_f, 0.0)
        return cnt, jnp.max(mx, axis=0, keepdims=True)

    n_adm = (q0 + 1 + lax.broadcasted_iota(I32, (1, TQ), 1)).astype(F32)
    few = n_adm < k_sel
    lo0 = jnp.where(few, NEG_INF, smin)
    clo0 = jnp.where(few, k_sel, jnp.where(smin <= NEG_INF, jnp.float32(seq_len), n_adm))
    hi0 = smax + (jnp.abs(smax) * 1e-6 + 1e-30)
    chi0 = jnp.zeros((1, TQ), F32)

    def bracket_body(it, st):
        lo, hi, clo, chi, flo, fhi, side = st
        done = clo == k_sel
        frac = jnp.clip(flo / (flo - fhi), 1.0 / 512, 511.0 / 512)
        frac = jnp.where(clo - chi <= 2.0, 0.5, frac)
        cand = lo + (hi - lo) * frac
        zero_inside = jnp.logical_and(jnp.logical_and(lo < 0.0, hi > 0.0), it == 0)
        cand = jnp.where(done, lo, jnp.where(zero_inside, 0.0, cand))
        cnt = count_ge(cand)
        ge = cnt >= k_sel
        f = cnt - (k_sel - 0.5)
        new_side = jnp.where(ge, 1.0, -1.0)
        same = new_side == side
        flo_n = jnp.where(ge, f, jnp.where(same, flo * 0.5, flo))
        fhi_n = jnp.where(ge, jnp.where(same, fhi * 0.5, fhi), f)
        up_lo = jnp.logical_and(jnp.logical_not(done), ge)
        up_hi = jnp.logical_and(jnp.logical_not(done), jnp.logical_not(ge))
        lo = jnp.where(up_lo, cand, lo)
        clo = jnp.where(up_lo, cnt, clo)
        hi = jnp.where(up_hi, cand, hi)
        chi = jnp.where(up_hi, cnt, chi)
        flo = jnp.where(done, flo, flo_n)
        fhi = jnp.where(done, fhi, fhi_n)
        side = jnp.where(done, side, new_side)
        return lo, hi, clo, chi, flo, fhi, side

    lo, hi, clo, chi = lax.fori_loop(
        0, BRACKET_PASSES, bracket_body,
        (lo0, hi0, clo0, chi0, clo0 - (k_sel - 0.5), chi0 - (k_sel - 0.5),
         jnp.zeros((1, TQ), F32)))[:4]
    pending = jnp.sum(jnp.where(clo == k_sel, 0.0, 1.0))
    thr_ref[...] = lo
    tie_ref[...] = jnp.full((1, TQ), NO_TIE, F32)

    @pl.when(pending > 0.5)
    def _():
        fin0 = jnp.where(clo == k_sel, 1.0, 0.0)
        v0 = max_below(hi)

        def fin_cond(st):
            return jnp.logical_and(st[0] < seq_len + 2, st[1] > 0.5)

        def fin_body(st):
            j, _, fin, h, ch, v, kst, need = st
            cnt, v2 = count_ge_and_max_below(v)
            hit = jnp.logical_and(fin < 0.5, cnt >= k_sel)
            kst = jnp.where(hit, v, kst)
            need = jnp.where(jnp.logical_and(hit, cnt > k_sel), k_sel - ch, need)
            fin = jnp.where(hit, 1.0, fin)
            open_ = fin < 0.5
            h = jnp.where(open_, v, h)
            ch = jnp.where(open_, cnt, ch)
            v = jnp.where(open_, v2, v)
            return j + 1, jnp.sum(1.0 - fin), fin, h, ch, v, kst, need

        st2 = lax.while_loop(
            fin_cond, fin_body,
            (jnp.int32(0), pending, fin0, hi, chi, v0, lo, jnp.full((1, TQ), NO_TIE, F32)))
        thr_ref[...] = st2[6]
        tie_ref[...] = st2[7]

    kstar = thr_ref[...]
    need = tie_ref[...]
    fix = jnp.logical_or(need < NO_TIE, kstar <= NEG_INF)

    @pl.when(jnp.sum(jnp.where(fix, 1.0, 0.0)) > 0.5)
    def _():
        tri = jnp.where(lax.broadcasted_iota(I32, (TK, TK), 0) >= lax.broadcasted_iota(I32, (TK, TK), 1),
                        1.0, 0.0).astype(BF16)

        def rewrite(c, seen):
            s = sc_ref[c]
            tied = jnp.where(s == kstar, 1.0, 0.0)
            rank = seen + _dot(tri, tied.astype(BF16))
            tie_take = jnp.where(rank <= need, 2.0 * tied - 1.0, -1.0)
            take = jnp.where(s > kstar, 1.0, tie_take)
            sc_ref[c] = jnp.where(c * TK + row_iota <= t_idx, take, -1.0)
            return rank[TK - 1:TK, :]

        lax.fori_loop(0, nkc, rewrite, jnp.zeros((1, TQ), F32))
        thr_ref[...] = jnp.zeros((1, TQ), F32)

    m_ref[...] = jnp.full(m_ref.shape, NEG_INF, F32)
    acc_ref[...] = jnp.zeros(acc_ref.shape, F32)
    thr = thr_ref[...]

    def masked_logits(c):
        bias = jnp.where(sc_ref[c] >= thr, 0.0, NEG_INF)
        lg = _dot(k_ref[c], qw_ref[...])
        lg = jnp.concatenate(
            [lg[:, h * TQ:(h + 1) * TQ] + bias for h in range(N_HEADS)], axis=1)
        return lg, jnp.max(lg, axis=0, keepdims=True)

    n_pairs = nkc // 2

    def produce_pair(j, a_ref, b_ref):
        jc = jnp.minimum(j, n_pairs - 1)
        lg_a, max_a = masked_logits(2 * jc)
        lg_b, max_b = masked_logits(2 * jc + 1)
        a_ref[...] = lg_a
        b_ref[...] = lg_b
        return jnp.maximum(max_a, max_b)

    def consume_pair(j, a_ref, b_ref, pair_max):
        m_old = m_ref[...]
        m_new = jnp.maximum(m_old, pair_max)
        alpha = jnp.exp2(m_old - m_new)
        p = jnp.concatenate([jnp.exp2(a_ref[...] - m_new).astype(BF16),
                             jnp.exp2(b_ref[...] - m_new).astype(BF16)], axis=0)
        va = jnp.concatenate([va_ref[2 * j], va_ref[2 * j + 1]], axis=1)
        acc_ref[...] = alpha * acc_ref[...] + _dot(va, p)
        m_ref[...] = m_new

    def pipeline_step(cur, nxt):
        def body(j, pair_max):
            next_max = produce_pair(j + 1, *nxt)
            consume_pair(j, *cur, pair_max)
            return next_max
        return body

    even = pipeline_step(lg_refs[0:2], lg_refs[2:4])
    odd = pipeline_step(lg_refs[2:4], lg_refs[0:2])
    pair_max = produce_pair(0, *lg_refs[0:2])
    pair_max = lax.fori_loop(0, n_pairs // 2, lambda t, c: odd(2 * t + 1, even(2 * t, c)), pair_max)
    lax.fori_loop(2 * (n_pairs // 2), n_pairs, even, pair_max)

    a = acc_ref[...]
    o = a[:HEAD_DIM] / a[HEAD_DIM:HEAD_DIM + 1]
    o = jnp.concatenate([o[:, h * TQ:(h + 1) * TQ] for h in range(N_HEADS)], axis=0)
    o_ref[...] = o.T


def _dsa_attention(iqt, iwt, ik, qw, k, vaug):
    B, _, L = iqt.shape
    n_sel = min(TOPK_MAX, L // 4)
    nch = L // TK
    va_rows = vaug.shape[2]
    grid = (B, L // TQ)
    kern = functools.partial(_dsa_kernel, seq_len=L, n_sel=n_sel)
    return pl.pallas_call(
        kern,
        grid=grid,
        in_specs=[
            pl.BlockSpec((None, IDX_HEADS * IDX_DIM, TQ), lambda b, i: (b, 0, i)),
            pl.BlockSpec((None, IDX_HEADS, TQ), lambda b, i: (b, 0, i)),
            pl.BlockSpec((None, nch, TK, IDX_DIM), lambda b, i: (b, 0, 0, 0)),
            pl.BlockSpec((None, None, HEAD_DIM, N_HEADS * TQ), lambda b, i: (b, i, 0, 0)),
            pl.BlockSpec((None, nch, TK, HEAD_DIM), lambda b, i: (b, 0, 0, 0)),
            pl.BlockSpec((None, nch, va_rows, TK), lambda b, i: (b, 0, 0, 0)),
        ],
        out_specs=pl.BlockSpec((None, TQ, ATTN_WIDTH), lambda b, i: (b, i, 0)),
        out_shape=jax.ShapeDtypeStruct((B, L, ATTN_WIDTH), F32),
        scratch_shapes=[
            pltpu.VMEM((nch, TK, TQ), F32),
            pltpu.VMEM((1, TQ), F32),
            pltpu.VMEM((1, TQ), F32),
            pltpu.VMEM((1, N_HEADS * TQ), F32),
            pltpu.VMEM((va_rows, N_HEADS * TQ), F32),
        ] + [pltpu.VMEM((TK, N_HEADS * TQ), F32)] * 4,
        compiler_params=pltpu.CompilerParams(
            dimension_semantics=("arbitrary", "arbitrary"),
            vmem_limit_bytes=V7X_VMEM_LIMIT_BYTES),
        name="dsa_attn",
    )(iqt, iwt, ik, qw, k, vaug)


def _merge_kernel(x_ref, u_ref, uh_ref, a_ref, lng_ref, lnb_ref, wgate_ref, bgate_ref,
                  poolw_ref, pscale_ref, wpp_ref, wpa_ref, wout_ref, ln1g_ref, ln1b_ref,
                  wrh_ref, wrl_ref, br_ref, h1_ref, gates_ref, *, tiles_per_seq):
    tm = TM_MERGE
    i = pl.program_id(0)
    seq_start = (i % tiles_per_seq) == 0
    h = _layer_norm(x_ref[...], lng_ref[...], lnb_ref[...])
    hb = h.astype(BF16)

    halo = jnp.where(seq_start, 0.0, uh_ref[...])
    u = u_ref[...]
    ext = jnp.concatenate([halo, u], axis=0)
    pos = (i % tiles_per_seq) * tm + lax.broadcasted_iota(I32, (tm, 1), 0)
    mixed = []
    for g, w in enumerate(POOL_WINDOWS):
        s = ext[:, g * POOL_GROUP_DIM:(g + 1) * POOL_GROUP_DIM]
        span = 1
        while span < w:
            s = s + pltpu.roll(s, span, 0)
            span *= 2
        win = s[POOL_HALO:]
        cnt = jnp.minimum(pos + 1, w).astype(F32)
        ug = u[:, g * POOL_GROUP_DIM:(g + 1) * POOL_GROUP_DIM]
        delta = win / cnt - ug
        mixed.append(_dot(delta.astype(BF16), poolw_ref[g]))
    pool_out = jnp.concatenate(mixed, axis=1) * pscale_ref[...]

    gate_pre = _dot(hb, wgate_ref[...]) + bgate_ref[...]
    gates = jax.nn.sigmoid(gate_pre)
    bp = _dot(pool_out.astype(BF16), wpp_ref[...])
    ba = _dot(a_ref[...].astype(BF16), wpa_ref[...])
    merged = gates[:, :D_MODEL] * bp + gates[:, D_MODEL:] * ba
    mix = _dot(merged.astype(BF16), wout_ref[...])
    h1 = _layer_norm(DEEPNORM_ALPHA * h + mix, ln1g_ref[...], ln1b_ref[...])
    h1_ref[...] = h1

    hi = h1.astype(BF16)
    lo = (h1 - hi.astype(F32)).astype(BF16)
    lg = _dot(hi, wrh_ref[...]) + (_dot(lo, wrh_ref[...]) + _dot(hi, wrl_ref[...])) + br_ref[...]
    lane = lax.broadcasted_iota(I32, (tm, LANES), 1).astype(F32)
    big = jnp.float32(1 << 20)
    ninf = jnp.float32(-jnp.inf)
    is_g = jnp.logical_and(lane >= N_EXPERTS, lane < N_EXPERTS + N_GROUPS)
    gl = jnp.where(is_g, lg, ninf)
    gmax = jnp.max(gl, axis=1, keepdims=True)
    gsel = jnp.min(jnp.where(gl == gmax, lane, big), axis=1, keepdims=True) - N_EXPERTS
    sumexp = jnp.sum(jnp.where(is_g, jnp.exp(gl - gmax), 0.0), axis=1, keepdims=True)
    p_group = 1.0 / sumexp
    e_lo = gsel * EXPERTS_PER_GROUP
    in_grp = jnp.logical_and(lane >= e_lo, lane < e_lo + EXPERTS_PER_GROUP)
    el = jnp.where(in_grp, lg, ninf)
    m1 = jnp.max(el, axis=1, keepdims=True)
    i1 = jnp.min(jnp.where(el == m1, lane, big), axis=1, keepdims=True)
    el2 = jnp.where(lane == i1, ninf, el)
    m2 = jnp.max(el2, axis=1, keepdims=True)
    i2 = jnp.min(jnp.where(el2 == m2, lane, big), axis=1, keepdims=True)
    e2 = jnp.exp(m2 - m1)
    den = 1.0 + e2
    w1 = (1.0 / den) * p_group
    w2 = (e2 / den) * p_group
    dense = jnp.where(lane == i1, w1, 0.0) + jnp.where(lane == i2, w2, 0.0)
    gates_ref[...] = jnp.where(lane == float(GSEL_LANE), gsel, dense)


def _merge(x, u, attn, ln_g, ln_b, wgate, bgate, poolw, pscale, wpp, wpa, wout, ln1g, ln1b,
           wrh, wrl, br, seq_len):
    T, D = x.shape
    tm = TM_MERGE
    tiles_per_seq = seq_len // tm
    grid = (T // tm,)
    tok = lambda i: (i, 0)
    c2 = lambda i: (0, 0)
    c3 = lambda i: (0, 0, 0)
    halo_blocks = tm // POOL_HALO
    kern = functools.partial(_merge_kernel, tiles_per_seq=tiles_per_seq)
    return pl.pallas_call(
        kern,
        grid=grid,
        in_specs=[
            pl.BlockSpec((tm, D), tok),
            pl.BlockSpec((tm, POOL_WIDTH), tok),
            pl.BlockSpec((POOL_HALO, POOL_WIDTH),
                         lambda i: (jnp.maximum(i * halo_blocks - 1, 0), 0)),
            pl.BlockSpec((tm, ATTN_WIDTH), tok),
            pl.BlockSpec((1, D), c2),
            pl.BlockSpec((1, D), c2),
            pl.BlockSpec((D, N_BRANCHES * D), c2),
            pl.BlockSpec((1, N_BRANCHES * D), c2),
            pl.BlockSpec((POOL_GROUPS, POOL_GROUP_DIM, POOL_GROUP_DIM), c3),
            pl.BlockSpec((1, POOL_WIDTH), c2),
            pl.BlockSpec((POOL_WIDTH, D), c2),
            pl.BlockSpec((ATTN_WIDTH, D), c2),
            pl.BlockSpec((D, D), c2),
            pl.BlockSpec((1, D), c2),
            pl.BlockSpec((1, D), c2),
            pl.BlockSpec((D, LANES), c2),
            pl.BlockSpec((D, LANES), c2),
            pl.BlockSpec((1, LANES), c2),
        ],
        out_specs=[
            pl.BlockSpec((tm, D), tok),
            pl.BlockSpec((tm, LANES), tok),
        ],
        out_shape=[
            jax.ShapeDtypeStruct((T, D), F32),
            jax.ShapeDtypeStruct((T, LANES), F32),
        ],
        compiler_params=pltpu.CompilerParams(
            dimension_semantics=("arbitrary",),
            vmem_limit_bytes=V7X_VMEM_LIMIT_BYTES),
        name="merge",
    )(x, u, u, attn, ln_g, ln_b, wgate, bgate, poolw, pscale, wpp, wpa, wout, ln1g, ln1b,
      wrh, wrl, br)


def _snake_group(tile, step):
    return jnp.where(tile % 2 == 0, step, N_GROUPS - 1 - step)


def _moe_kernel(h1_ref, g_ref, tri_ref, wg_ref, wu_ref, wd_ref, ln2g_ref, ln2b_ref, o_ref,
                xs_ref, ys_ref, dcol_ref, tab_ref):
    tm = TM_MOE
    step = pl.program_id(1)
    g = _snake_group(pl.program_id(0), step)
    lane = lax.broadcasted_iota(I32, (tm, LANES), 1).astype(F32)
    lane1 = lax.broadcasted_iota(I32, (1, LANES), 1).astype(F32)

    @pl.when(step == 0)
    def _():
        rec = g_ref[...]
        gsel = jnp.sum(jnp.where(lane == float(GSEL_LANE), rec, 0.0), axis=1, keepdims=True)
        onehot = jnp.where(lane == gsel, 1.0, 0.0)
        cum = _dot(tri_ref[...], onehot.astype(BF16))
        cnt = cum[tm - 1:tm, :]
        padded = jnp.floor((cnt + (MOE_CHUNK - 1)) * (1.0 / MOE_CHUNK)) * MOE_CHUNK
        start = jnp.zeros((1, LANES), F32)
        for gg in range(N_GROUPS - 1):
            p = jnp.sum(jnp.where(lane1 == float(gg), padded, 0.0), axis=1, keepdims=True)
            start = start + jnp.where(lane1 > float(gg), p, 0.0)
        tab_ref[0:1, :] = start
        tab_ref[1:2, :] = padded
        slot = jnp.sum(onehot * (start + cum - 1.0), axis=1, keepdims=True)
        slot_b = jnp.broadcast_to(slot, (tm, LANES))
        dcol_ref[...] = slot_b
        slot_row = slot_b.T[0:1, :]

        dense = jnp.where(lane < float(N_EXPERTS), rec, 0.0)
        own = jnp.where(gsel == 0.0, dense, 0.0)
        for gg in range(1, N_GROUPS):
            own = own + jnp.where(gsel == float(gg),
                                  pltpu.roll(dense, LANES - gg * EXPERTS_PER_GROUP, 1), 0.0)
        own = jnp.where(lane < float(EXPERTS_PER_GROUP), own, 0.0)
        own_hi = own.astype(BF16).astype(F32)
        side = (own_hi + pltpu.roll(own - own_hi, EXPERTS_PER_GROUP, 1)).astype(BF16)
        xa = jnp.concatenate([h1_ref[...].astype(BF16), side], axis=1)

        blk = MOE_SORT_ROWS
        for rb in range(MOE_SLOTS // blk):
            rows = rb * blk + lax.broadcasted_iota(I32, (blk, tm), 0).astype(F32)
            perm = jnp.where(rows == slot_row, 1.0, 0.0).astype(BF16)
            xs_ref[rb * blk:(rb + 1) * blk, :] = _dot(perm, xa).astype(BF16)
        ys_ref[...] = jnp.zeros(ys_ref.shape, BF16)

    gf = g.astype(F32)
    first = jnp.sum(jnp.where(lane1 == gf, tab_ref[0:1, :], 0.0)).astype(I32)
    n_chunks = jnp.sum(jnp.where(lane1 == gf, tab_ref[1:2, :], 0.0)).astype(I32) // MOE_CHUNK
    ff = EXPERTS_PER_GROUP * EXPERT_FF

    def ffn_chunk(c, carry):
        r0 = pl.multiple_of(first + c * MOE_CHUNK, MOE_CHUNK)
        xc = xs_ref[pl.ds(r0, MOE_CHUNK), :]
        xb = xc[:, :D_MODEL]
        gl = xc[:, D_MODEL:].astype(F32)
        gates = gl[:, :EXPERTS_PER_GROUP] + gl[:, EXPERTS_PER_GROUP:2 * EXPERTS_PER_GROUP]
        parts = []
        for j in range(EXPERTS_PER_GROUP):
            hid = jax.nn.silu(_dot(xb, wg_ref[j])) * _dot(xb, wu_ref[j])
            parts.append((hid * gates[:, j:j + 1]).astype(BF16))
        y = _dot(jnp.concatenate(parts, axis=1), wd_ref[...].reshape(ff, D_MODEL))
        ys_ref[pl.ds(r0, MOE_CHUNK), :] = y.astype(BF16)
        return carry

    lax.fori_loop(0, n_chunks, ffn_chunk, 0)

    @pl.when(step == N_GROUPS - 1)
    def _():
        slots = lax.broadcasted_iota(I32, (tm, MOE_SLOTS), 1).astype(F32)
        back = jnp.where(slots == dcol_ref[:, 0:1], 1.0, 0.0).astype(BF16)
        ffn = _dot(back, ys_ref[...])
        o_ref[...] = _layer_norm(DEEPNORM_ALPHA * h1_ref[...] + ffn, ln2g_ref[...], ln2b_ref[...])


def _moe(h1, route, wg, wu, wd, ln2g, ln2b):
    T, D = h1.shape
    tm = TM_MOE
    epg = EXPERTS_PER_GROUP
    grid = (T // tm, N_GROUPS)
    once = pl.Buffered(1)
    return pl.pallas_call(
        _moe_kernel,
        grid=grid,
        in_specs=[
            pl.BlockSpec((tm, D), lambda i, s: (i, 0), pipeline_mode=once),
            pl.BlockSpec((tm, LANES), lambda i, s: (i, 0), pipeline_mode=once),
            pl.BlockSpec((tm, tm), lambda i, s: (0, 0), pipeline_mode=once),
            pl.BlockSpec((epg, D, EXPERT_FF), lambda i, s: (_snake_group(i, s), 0, 0)),
            pl.BlockSpec((epg, D, EXPERT_FF), lambda i, s: (_snake_group(i, s), 0, 0)),
            pl.BlockSpec((epg, EXPERT_FF, D), lambda i, s: (_snake_group(i, s), 0, 0)),
            pl.BlockSpec((1, D), lambda i, s: (0, 0)),
            pl.BlockSpec((1, D), lambda i, s: (0, 0)),
        ],
        out_specs=pl.BlockSpec((tm, D), lambda i, s: (i, 0)),
        out_shape=jax.ShapeDtypeStruct((T, D), F32),
        scratch_shapes=[
            pltpu.VMEM((MOE_SLOTS, D + LANES), BF16),
            pltpu.VMEM((MOE_SLOTS, D), BF16),
            pltpu.VMEM((tm, LANES), F32),
            pltpu.VMEM((8, LANES), F32),
        ],
        compiler_params=pltpu.CompilerParams(
            dimension_semantics=("arbitrary", "arbitrary"),
            vmem_limit_bytes=V7X_VMEM_LIMIT_BYTES_MOE),
        name="moe",
    )(h1, route, jnp.tril(jnp.ones((tm, tm), BF16)), wg, wu, wd, ln2g, ln2b)


def kernel(x, ln_in_g, ln_in_b, w_in, b_gate, pool_w, pool_scale, w_proj_pool, w_proj_attn, w_out,
           ln1_g, ln1_b, w_group, b_group, w_router, b_router, w_gate, w_up, w_down, ln2_g, ln2_b):
    B, L, D = x.shape
    assert D == D_MODEL and w_in.shape[0] == DEPTH == 1
    assert L % TQ == 0 and L % TM_MERGE == 0 and L % TM_PROJ == 0
    assert L <= 4096
    T = B * L
    row = lambda v: v.reshape(1, -1).astype(F32)

    w = w_in[0]
    o = 0
    w_u = w[:, o:o + POOL_WIDTH]; o += POOL_WIDTH
    w_q = w[:, o:o + ATTN_WIDTH]; o += ATTN_WIDTH
    w_k = w[:, o:o + KV_DIM]; o += KV_DIM
    w_v = w[:, o:o + KV_DIM]; o += KV_DIM
    w_iq = w[:, o:o + IDX_HEADS * IDX_DIM]; o += IDX_HEADS * IDX_DIM
    w_ik = w[:, o:o + IDX_DIM]; o += IDX_DIM
    w_iw = w[:, o:o + IDX_HEADS]; o += IDX_HEADS
    w_g = w[:, o:]
    sm_scale = math.log2(math.e) / math.sqrt(HEAD_DIM)
    bf16_rows = 16
    w_t = jnp.concatenate([w_q * sm_scale, w_v, w_iq, w_iw,
                           jnp.zeros((D, bf16_rows - IDX_HEADS), F32)], axis=1).T.astype(BF16)

    u, k, ik, qw, vaug, iqt, iwt = _in_proj(
        x, row(ln_in_g), row(ln_in_b), w_u.astype(BF16), w_k.astype(BF16), w_ik.astype(BF16), w_t)
    attn = _dsa_attention(iqt, iwt, ik, qw, k, vaug)

    w_r = jnp.zeros((D, LANES), F32)
    w_r = w_r.at[:, :N_EXPERTS].set(w_router[0]).at[:, N_EXPERTS:N_EXPERTS + N_GROUPS].set(w_group[0])
    b_r = jnp.zeros((1, LANES), F32)
    b_r = b_r.at[0, :N_EXPERTS].set(b_router[0]).at[0, N_EXPERTS:N_EXPERTS + N_GROUPS].set(b_group[0])
    w_rh = w_r.astype(BF16)
    w_rl = (w_r - w_rh.astype(F32)).astype(BF16)

    h1, gates = _merge(
        x.reshape(T, D), u.reshape(T, POOL_WIDTH), attn.reshape(T, ATTN_WIDTH),
        row(ln_in_g), row(ln_in_b), w_g.astype(BF16), row(b_gate[0]),
        pool_w[0].astype(BF16), row(pool_scale[0]), w_proj_pool[0].astype(BF16),
        w_proj_attn[0].astype(BF16), w_out[0].astype(BF16), row(ln1_g[0]), row(ln1_b[0]),
        w_rh, w_rl, b_r, L)

    out = _moe(h1, gates, w_gate[0].astype(BF16), w_up[0].astype(BF16), w_down[0].astype(BF16),
               row(ln2_g[0]), row(ln2_b[0]))
    return out.reshape(B, L, D)
```

```python
import functools
import math

import jax
import jax.numpy as jnp
import numpy as np
from jax import lax
from jax.experimental import pallas as pl
from jax.experimental.pallas import tpu as pltpu

D_MODEL = 1024
POOL_WINDOWS = (2, 4, 8, 16)
POOL_GROUPS = 4
POOL_WIDTH = 512
POOL_GROUP_DIM = 128
N_HEADS = 8
HEAD_DIM = 64
ATTN_WIDTH = 512
KV_DIM = 64
IDX_HEADS = 8
IDX_DIM = 32
TOPK_MAX = 256
NEG_INF = float(np.float32(-1e30))
N_BRANCHES = 2
N_GROUPS = 4
EXPERTS_PER_GROUP = 8
N_EXPERTS = 32
EXPERT_FF = 256
LN_EPS = 1e-5
DEPTH = 1
DEEPNORM_ALPHA = (2.0 * DEPTH) ** 0.25

V7X_VMEM_LIMIT_BYTES = 56 * 1024 * 1024
V7X_VMEM_LIMIT_BYTES_MOE = 60 * 1024 * 1024
LANES = 128

F32 = jnp.float32
BF16 = jnp.bfloat16
I32 = jnp.int32

TM_PROJ = 512
TQ = 256
TK = 128
TM_MERGE = 512
POOL_HALO = 16
V_ONES_ROWS = 16
TM_MOE = 1024
MOE_CHUNK = 128
MOE_SLOTS = TM_MOE + N_GROUPS * MOE_CHUNK
MOE_SORT_ROWS = 256
GSEL_LANE = N_EXPERTS


def _layer_norm(x, g, b):
    mu = jnp.mean(x, axis=-1, keepdims=True)
    xc = x - mu
    var = jnp.mean(xc * xc, axis=-1, keepdims=True)
    return xc * lax.rsqrt(var + LN_EPS) * g + b


def _dot(a, b):
    return jnp.dot(a, b, preferred_element_type=F32)


def _dot_nt(a, b):
    return lax.dot_general(a, b, (((1,), (1,)), ((), ())), preferred_element_type=F32)


def _in_proj_kernel(x_ref, g_ref, b_ref, wu_ref, wk_ref, wik_ref, wt_ref,
                    u_ref, k_ref, ik_ref, qt_ref, vt_ref, iqt_ref, iwt_ref):
    h = _layer_norm(x_ref[...], g_ref[...], b_ref[...])
    hb = h.astype(BF16)
    u_ref[...] = _dot(hb, wu_ref[...])
    n_chunks = TM_PROJ // TK
    k_ref[...] = _dot(hb, wk_ref[...]).astype(BF16).reshape(n_chunks, TK, KV_DIM)
    ik_ref[...] = _dot(hb, wik_ref[...]).astype(BF16).reshape(n_chunks, TK, IDX_DIM)
    pt = _dot_nt(wt_ref[...], hb)
    r0 = 0
    for j in range(TM_PROJ // TQ):
        for h in range(N_HEADS):
            qt_ref[j, :, h * TQ:(h + 1) * TQ] = pt[r0 + h * HEAD_DIM:r0 + (h + 1) * HEAD_DIM,
                                                   j * TQ:(j + 1) * TQ].astype(BF16)
    r0 += ATTN_WIDTH
    for c in range(n_chunks):
        vt_ref[c, :KV_DIM, :] = pt[r0:r0 + KV_DIM, c * TK:(c + 1) * TK].astype(BF16)
        vt_ref[c, KV_DIM:, :] = jnp.ones((V_ONES_ROWS, TK), BF16)
    r0 += KV_DIM
    iqt_ref[...] = pt[r0:r0 + IDX_HEADS * IDX_DIM].astype(BF16)
    r0 += IDX_HEADS * IDX_DIM
    iwt_ref[...] = pt[r0:r0 + IDX_HEADS]


def _in_proj(x, ln_g, ln_b, wu, wk, wik, wt):
    B, L, D = x.shape
    tm = TM_PROJ
    grid = (B, L // tm)
    tok = lambda b, i: (b, i, 0)
    tokt = lambda b, i: (b, 0, i)
    chunked = lambda b, i: (b, i, 0, 0)
    const2 = lambda b, i: (0, 0)
    n_t = wt.shape[0]
    return pl.pallas_call(
        _in_proj_kernel,
        grid=grid,
        in_specs=[
            pl.BlockSpec((None, tm, D), tok),
            pl.BlockSpec((1, D), const2),
            pl.BlockSpec((1, D), const2),
            pl.BlockSpec((D, POOL_WIDTH), const2),
            pl.BlockSpec((D, KV_DIM), const2),
            pl.BlockSpec((D, IDX_DIM), const2),
            pl.BlockSpec((n_t, D), const2),
        ],
        out_specs=[
            pl.BlockSpec((None, tm, POOL_WIDTH), tok),
            pl.BlockSpec((None, tm // TK, TK, KV_DIM), chunked),
            pl.BlockSpec((None, tm // TK, TK, IDX_DIM), chunked),
            pl.BlockSpec((None, tm // TQ, HEAD_DIM, N_HEADS * TQ), chunked),
            pl.BlockSpec((None, tm // TK, KV_DIM + V_ONES_ROWS, TK), chunked),
            pl.BlockSpec((None, IDX_HEADS * IDX_DIM, tm), tokt),
            pl.BlockSpec((None, IDX_HEADS, tm), tokt),
        ],
        out_shape=[
            jax.ShapeDtypeStruct((B, L, POOL_WIDTH), F32),
            jax.ShapeDtypeStruct((B, L // TK, TK, KV_DIM), BF16),
            jax.ShapeDtypeStruct((B, L // TK, TK, IDX_DIM), BF16),
            jax.ShapeDtypeStruct((B, L // TQ, HEAD_DIM, N_HEADS * TQ), BF16),
            jax.ShapeDtypeStruct((B, L // TK, KV_DIM + V_ONES_ROWS, TK), BF16),
            jax.ShapeDtypeStruct((B, IDX_HEADS * IDX_DIM, L), BF16),
            jax.ShapeDtypeStruct((B, IDX_HEADS, L), F32),
        ],
        compiler_params=pltpu.CompilerParams(
            dimension_semantics=("arbitrary", "arbitrary"),
            vmem_limit_bytes=V7X_VMEM_LIMIT_BYTES),
        name="in_proj",
    )(x, ln_g, ln_b, wu, wk, wik, wt)


BRACKET_PASSES = 14
NO_TIE = 1e9


def _fori_pairs(n, body, init):
    def pair(j, carry):
        return body(2 * j + 1, body(2 * j, carry))
    carry = lax.fori_loop(0, n // 2, pair, init)
    return lax.fori_loop(2 * (n // 2), n, body, carry)


def _rows_to_sublanes(x, op):
    return op(x.reshape(x.shape[0] // 8, 8, TQ), axis=0)


SCAN_CHUNKS = 2
SCAN_ROWS = SCAN_CHUNKS * TK


def _dsa_kernel(iqt_ref, iwt_ref, ik_ref, qw_ref, k_ref, va_ref, o_ref,
                sc_ref, thr_ref, tie_ref, m_ref, acc_ref, *lg_refs, seq_len, n_sel):
    assert (TQ // TK) % 2 == 0
    qi = pl.program_id(1)
    q0 = qi * TQ
    nkc = (qi + 1) * (TQ // TK)
    n_beyond = seq_len - (qi + 1) * TQ
    k_sel = jnp.float32(n_sel)

    scan_rows = lax.broadcasted_iota(I32, (SCAN_ROWS, TQ), 0)
    scan_t = q0 + lax.broadcasted_iota(I32, (SCAN_ROWS, TQ), 1)
    n_scan = nkc // SCAN_CHUNKS

    def scores_at(b):
        return sc_ref[pl.ds(b * SCAN_CHUNKS, SCAN_CHUNKS)].reshape(SCAN_ROWS, TQ)

    def score_step(b, carry):
        smin, smax = carry
        ikc = ik_ref[pl.ds(b * SCAN_CHUNKS, SCAN_CHUNKS)].reshape(SCAN_ROWS, IDX_DIM)
        score = jnp.zeros((SCAN_ROWS, TQ), F32)
        for h in range(IDX_HEADS):
            lg = _dot(ikc, iqt_ref[h * IDX_DIM:(h + 1) * IDX_DIM, :])
            score = score + iwt_ref[h:h + 1, :] * jnp.maximum(lg, 0.0)
        smin = jnp.minimum(smin, jnp.min(score, axis=0, keepdims=True))
        smax = jnp.maximum(smax, jnp.max(score, axis=0, keepdims=True))
        score = jnp.where(b * SCAN_ROWS + scan_rows <= scan_t, score, NEG_INF)
        sc_ref[pl.ds(b * SCAN_CHUNKS, SCAN_CHUNKS)] = score.reshape(SCAN_CHUNKS, TK, TQ)
        return smin, smax

    big = jnp.float32(3e38)
    smin, smax = _fori_pairs(n_scan, score_step,
                             (jnp.full((1, TQ), big, F32), jnp.full((1, TQ), -big, F32)))

    nb_f = n_beyond.astype(F32)
    ninf = jnp.float32(-jnp.inf)

    def count_ge(cand):
        def body(b, acc):
            return acc + _rows_to_sublanes(jnp.where(scores_at(b) >= cand, 1.0, 0.0), jnp.sum)
        acc = _fori_pairs(n_scan, body, jnp.zeros((8, TQ), F32))
        return jnp.sum(acc, axis=0, keepdims=True) + jnp.where(cand <= NEG_INF, nb_f, 0.0)

    def max_below(h):
        def body(b, acc):
            s = scores_at(b)
            return jnp.maximum(acc, _rows_to_sublanes(jnp.where(s < h, s, ninf), jnp.max))
        acc = _fori_pairs(n_scan, body, jnp.full((8, TQ), ninf, F32))
        return jnp.max(acc, axis=0, keepdims=True)

    def count_ge_and_max_below(v):
        def body(b, carry):
            acc, mx = carry
            s = scores_at(b)
            acc = acc + _rows_to_sublanes(jnp.where(s >= v, 1.0, 0.0), jnp.sum)
            mx = jnp.maximum(mx, _rows_to_sublanes(jnp.where(s < v, s, ninf), jnp.max))
            return acc, mx
        acc, mx = _fori_pairs(n_scan, body,
                              (jnp.zeros((8, TQ), F32), jnp.full((8, TQ), ninf, F32)))
        cnt = jnp.sum(acc, axis=0, keepdims=True) + jnp.where(v <= NEG_INF, nb_f, 0.0)
        return cnt, jnp.max(mx, axis=0, keepdims=True)

    n_adm = (q0 + 1 + lax.broadcasted_iota(I32, (1, TQ), 1)).astype(F32)
    few = n_adm < k_sel
    lo0 = jnp.where(few, NEG_INF, smin)
    clo0 = jnp.where(few, k_sel, jnp.where(smin <= NEG_INF, jnp.float32(seq_len), n_adm))
    hi0 = smax + (jnp.abs(smax) * 1e-6 + 1e-30)
    chi0 = jnp.zeros((1, TQ), F32)

    def bracket_body(it, st):
        lo, hi, clo, chi, flo, fhi, side = st
        done = clo == k_sel
        frac = jnp.clip(flo / (flo - fhi), 1.0 / 512, 511.0 / 512)
        frac = jnp.where(clo - chi <= 2.0, 0.5, frac)
        cand = lo + (hi - lo) * frac
        zero_inside = jnp.logical_and(jnp.logical_and(lo < 0.0, hi > 0.0), it == 0)
        cand = jnp.where(done, lo, jnp.where(zero_inside, 0.0, cand))
        cnt = count_ge(cand)
        ge = cnt >= k_sel
        f = cnt - (k_sel - 0.5)
        new_side = jnp.where(ge, 1.0, -1.0)
        same = new_side == side
        flo_n = jnp.where(ge, f, jnp.where(same, flo * 0.5, flo))
        fhi_n = jnp.where(ge, jnp.where(same, fhi * 0.5, fhi), f)
        up_lo = jnp.logical_and(jnp.logical_not(done), ge)
        up_hi = jnp.logical_and(jnp.logical_not(done), jnp.logical_not(ge))
        lo = jnp.where(up_lo, cand, lo)
        clo = jnp.where(up_lo, cnt, clo)
        hi = jnp.where(up_hi, cand, hi)
        chi = jnp.where(up_hi, cnt, chi)
        flo = jnp.where(done, flo, flo_n)
        fhi = jnp.where(done, fhi, fhi_n)
        side = jnp.where(done, side, new_side)
        return lo, hi, clo, chi, flo, fhi, side

    lo, hi, clo, chi = lax.fori_loop(
        0, BRACKET_PASSES, bracket_body,
        (lo0, hi0, clo0, chi0, clo0 - (k_sel - 0.5), chi0 - (k_sel - 0.5),
         jnp.zeros((1, TQ), F32)))[:4]
    pending = jnp.sum(jnp.where(clo == k_sel, 0.0, 1.0))
    thr_ref[...] = lo
    tie_ref[...] = jnp.full((1, TQ), NO_TIE, F32)

    @pl.when(pending > 0.5)
    def _():
        fin0 = jnp.where(clo == k_sel, 1.0, 0.0)
        v0 = max_below(hi)

        def fin_cond(st):
            return jnp.logical_and(st[0] < seq_len + 2, st[1] > 0.5)

        def fin_body(st):
            j, _, fin, h, ch, v, kst, need = st
            cnt, v2 = count_ge_and_max_below(v)
            hit = jnp.logical_and(fin < 0.5, cnt >= k_sel)
            kst = jnp.where(hit, v, kst)
            need = jnp.where(jnp.logical_and(hit, cnt > k_sel), k_sel - ch, need)
            fin = jnp.where(hit, 1.0, fin)
            open_ = fin < 0.5
            h = jnp.where(open_, v, h)
            ch = jnp.where(open_, cnt, ch)
            v = jnp.where(open_, v2, v)
            return j + 1, jnp.sum(1.0 - fin), fin, h, ch, v, kst, need

        st2 = lax.while_loop(
            fin_cond, fin_body,
            (jnp.int32(0), pending, fin0, hi, chi, v0, lo, jnp.full((1, TQ), NO_TIE, F32)))
        thr_ref[...] = st2[6]
        tie_ref[...] = st2[7]

    kstar = thr_ref[...]
    need = tie_ref[...]
    fix = jnp.logical_or(need < NO_TIE, kstar <= NEG_INF)

    @pl.when(jnp.sum(jnp.where(fix, 1.0, 0.0)) > 0.5)
    def _():
        tri = jnp.where(lax.broadcasted_iota(I32, (SCAN_ROWS, SCAN_ROWS), 0)
                        >= lax.broadcasted_iota(I32, (SCAN_ROWS, SCAN_ROWS), 1), 1.0, 0.0).astype(BF16)

        def rewrite(b, seen):
            s = scores_at(b)
            tied = jnp.where(s == kstar, 1.0, 0.0)
            rank = seen + _dot(tri, tied.astype(BF16))
            tie_take = jnp.where(rank <= need, 2.0 * tied - 1.0, -1.0)
            take = jnp.where(s > kstar, 1.0, tie_take)
            take = jnp.where(b * SCAN_ROWS + scan_rows <= scan_t, take, -1.0)
            sc_ref[pl.ds(b * SCAN_CHUNKS, SCAN_CHUNKS)] = take.reshape(SCAN_CHUNKS, TK, TQ)
            return seen + jnp.sum(tied, axis=0, keepdims=True)

        _fori_pairs(n_scan, rewrite, jnp.zeros((1, TQ), F32))
        thr_ref[...] = jnp.zeros((1, TQ), F32)

    m_ref[...] = jnp.full(m_ref.shape, NEG_INF, F32)
    acc_ref[...] = jnp.zeros(acc_ref.shape, F32)
    thr = thr_ref[...]

    def masked_logits(c):
        bias = jnp.where(sc_ref[c] >= thr, 0.0, NEG_INF)
        lg = _dot(k_ref[c], qw_ref[...])
        lg = jnp.concatenate(
            [lg[:, h * TQ:(h + 1) * TQ] + bias for h in range(N_HEADS)], axis=1)
        return lg, jnp.max(lg, axis=0, keepdims=True)

    n_pairs = nkc // 2

    def produce_pair(j, a_ref, b_ref):
        jc = jnp.minimum(j, n_pairs - 1)
        lg_a, max_a = masked_logits(2 * jc)
        lg_b, max_b = masked_logits(2 * jc + 1)
        a_ref[...] = lg_a
        b_ref[...] = lg_b
        return jnp.maximum(max_a, max_b)

    def consume_pair(j, a_ref, b_ref, pair_max):
        m_old = m_ref[...]
        m_new = jnp.maximum(m_old, pair_max)
        alpha = jnp.exp2(m_old - m_new)
        p = jnp.concatenate([jnp.exp2(a_ref[...] - m_new).astype(BF16),
                             jnp.exp2(b_ref[...] - m_new).astype(BF16)], axis=0)
        va = jnp.concatenate([va_ref[2 * j], va_ref[2 * j + 1]], axis=1)
        acc_ref[...] = alpha * acc_ref[...] + _dot(va, p)
        m_ref[...] = m_new

    def pipeline_step(cur, nxt):
        def body(j, pair_max):
            next_max = produce_pair(j + 1, *nxt)
            consume_pair(j, *cur, pair_max)
            return next_max
        return body

    even = pipeline_step(lg_refs[0:2], lg_refs[2:4])
    odd = pipeline_step(lg_refs[2:4], lg_refs[0:2])
    pair_max = produce_pair(0, *lg_refs[0:2])
    pair_max = lax.fori_loop(0, n_pairs // 2, lambda t, c: odd(2 * t + 1, even(2 * t, c)), pair_max)
    lax.fori_loop(2 * (n_pairs // 2), n_pairs, even, pair_max)

    a = acc_ref[...]
    o = a[:HEAD_DIM] / a[HEAD_DIM:HEAD_DIM + 1]
    o = jnp.concatenate([o[:, h * TQ:(h + 1) * TQ] for h in range(N_HEADS)], axis=0)
    o_ref[...] = o.T


def _dsa_attention(iqt, iwt, ik, qw, k, vaug):
    B, _, L = iqt.shape
    n_sel = min(TOPK_MAX, L // 4)
    nch = L // TK
    va_rows = vaug.shape[2]
    grid = (B, L // TQ)
    kern = functools.partial(_dsa_kernel, seq_len=L, n_sel=n_sel)
    return pl.pallas_call(
        kern,
        grid=grid,
        in_specs=[
            pl.BlockSpec((None, IDX_HEADS * IDX_DIM, TQ), lambda b, i: (b, 0, i)),
            pl.BlockSpec((None, IDX_HEADS, TQ), lambda b, i: (b, 0, i)),
            pl.BlockSpec((None, nch, TK, IDX_DIM), lambda b, i: (b, 0, 0, 0)),
            pl.BlockSpec((None, None, HEAD_DIM, N_HEADS * TQ), lambda b, i: (b, i, 0, 0)),
            pl.BlockSpec((None, nch, TK, HEAD_DIM), lambda b, i: (b, 0, 0, 0)),
            pl.BlockSpec((None, nch, va_rows, TK), lambda b, i: (b, 0, 0, 0)),
        ],
        out_specs=pl.BlockSpec((None, TQ, ATTN_WIDTH), lambda b, i: (b, i, 0)),
        out_shape=jax.ShapeDtypeStruct((B, L, ATTN_WIDTH), F32),
        scratch_shapes=[
            pltpu.VMEM((nch, TK, TQ), F32),
            pltpu.VMEM((1, TQ), F32),
            pltpu.VMEM((1, TQ), F32),
            pltpu.VMEM((1, N_HEADS * TQ), F32),
            pltpu.VMEM((va_rows, N_HEADS * TQ), F32),
        ] + [pltpu.VMEM((TK, N_HEADS * TQ), F32)] * 4,
        compiler_params=pltpu.CompilerParams(
            dimension_semantics=("arbitrary", "arbitrary"),
            vmem_limit_bytes=V7X_VMEM_LIMIT_BYTES),
        name="dsa_attn",
    )(iqt, iwt, ik, qw, k, vaug)


def _merge_kernel(x_ref, u_ref, uh_ref, a_ref, lng_ref, lnb_ref, wgate_ref, bgate_ref,
                  poolw_ref, pscale_ref, wpp_ref, wpa_ref, wout_ref, ln1g_ref, ln1b_ref,
                  wrh_ref, wrl_ref, br_ref, h1_ref, gates_ref, *, tiles_per_seq):
    tm = TM_MERGE
    i = pl.program_id(0)
    seq_start = (i % tiles_per_seq) == 0
    h = _layer_norm(x_ref[...], lng_ref[...], lnb_ref[...])
    hb = h.astype(BF16)

    halo = jnp.where(seq_start, 0.0, uh_ref[...])
    u = u_ref[...]
    ext = jnp.concatenate([halo, u], axis=0)
    pos = (i % tiles_per_seq) * tm + lax.broadcasted_iota(I32, (tm, 1), 0)
    mixed = []
    for g, w in enumerate(POOL_WINDOWS):
        s = ext[:, g * POOL_GROUP_DIM:(g + 1) * POOL_GROUP_DIM]
        span = 1
        while span < w:
            s = s + pltpu.roll(s, span, 0)
            span *= 2
        win = s[POOL_HALO:]
        cnt = jnp.minimum(pos + 1, w).astype(F32)
        ug = u[:, g * POOL_GROUP_DIM:(g + 1) * POOL_GROUP_DIM]
        delta = win / cnt - ug
        mixed.append(_dot(delta.astype(BF16), poolw_ref[g]))
    pool_out = jnp.concatenate(mixed, axis=1) * pscale_ref[...]

    gate_pre = _dot(hb, wgate_ref[...]) + bgate_ref[...]
    gates = jax.nn.sigmoid(gate_pre)
    bp = _dot(pool_out.astype(BF16), wpp_ref[...])
    ba = _dot(a_ref[...].astype(BF16), wpa_ref[...])
    merged = gates[:, :D_MODEL] * bp + gates[:, D_MODEL:] * ba
    mix = _dot(merged.astype(BF16), wout_ref[...])
    h1 = _layer_norm(DEEPNORM_ALPHA * h + mix, ln1g_ref[...], ln1b_ref[...])
    h1_ref[...] = h1

    hi = h1.astype(BF16)
    lo = (h1 - hi.astype(F32)).astype(BF16)
    lg = _dot(hi, wrh_ref[...]) + (_dot(lo, wrh_ref[...]) + _dot(hi, wrl_ref[...])) + br_ref[...]
    lane = lax.broadcasted_iota(I32, (tm, LANES), 1).astype(F32)
    big = jnp.float32(1 << 20)
    ninf = jnp.float32(-jnp.inf)
    is_g = jnp.logical_and(lane >= N_EXPERTS, lane < N_EXPERTS + N_GROUPS)
    gl = jnp.where(is_g, lg, ninf)
    gmax = jnp.max(gl, axis=1, keepdims=True)
    gsel = jnp.min(jnp.where(gl == gmax, lane, big), axis=1, keepdims=True) - N_EXPERTS
    sumexp = jnp.sum(jnp.where(is_g, jnp.exp(gl - gmax), 0.0), axis=1, keepdims=True)
    p_group = 1.0 / sumexp
    e_lo = gsel * EXPERTS_PER_GROUP
    in_grp = jnp.logical_and(lane >= e_lo, lane < e_lo + EXPERTS_PER_GROUP)
    el = jnp.where(in_grp, lg, ninf)
    m1 = jnp.max(el, axis=1, keepdims=True)
    i1 = jnp.min(jnp.where(el == m1, lane, big), axis=1, keepdims=True)
    el2 = jnp.where(lane == i1, ninf, el)
    m2 = jnp.max(el2, axis=1, keepdims=True)
    i2 = jnp.min(jnp.where(el2 == m2, lane, big), axis=1, keepdims=True)
    e2 = jnp.exp(m2 - m1)
    den = 1.0 + e2
    w1 = (1.0 / den) * p_group
    w2 = (e2 / den) * p_group
    dense = jnp.where(lane == i1, w1, 0.0) + jnp.where(lane == i2, w2, 0.0)
    gates_ref[...] = jnp.where(lane == float(GSEL_LANE), gsel, dense)


def _merge(x, u, attn, ln_g, ln_b, wgate, bgate, poolw, pscale, wpp, wpa, wout, ln1g, ln1b,
           wrh, wrl, br, seq_len):
    T, D = x.shape
    tm = TM_MERGE
    tiles_per_seq = seq_len // tm
    grid = (T // tm,)
    tok = lambda i: (i, 0)
    c2 = lambda i: (0, 0)
    c3 = lambda i: (0, 0, 0)
    halo_blocks = tm // POOL_HALO
    kern = functools.partial(_merge_kernel, tiles_per_seq=tiles_per_seq)
    return pl.pallas_call(
        kern,
        grid=grid,
        in_specs=[
            pl.BlockSpec((tm, D), tok),
            pl.BlockSpec((tm, POOL_WIDTH), tok),
            pl.BlockSpec((POOL_HALO, POOL_WIDTH),
                         lambda i: (jnp.maximum(i * halo_blocks - 1, 0), 0)),
            pl.BlockSpec((tm, ATTN_WIDTH), tok),
            pl.BlockSpec((1, D), c2),
            pl.BlockSpec((1, D), c2),
            pl.BlockSpec((D, N_BRANCHES * D), c2),
            pl.BlockSpec((1, N_BRANCHES * D), c2),
            pl.BlockSpec((POOL_GROUPS, POOL_GROUP_DIM, POOL_GROUP_DIM), c3),
            pl.BlockSpec((1, POOL_WIDTH), c2),
            pl.BlockSpec((POOL_WIDTH, D), c2),
            pl.BlockSpec((ATTN_WIDTH, D), c2),
            pl.BlockSpec((D, D), c2),
            pl.BlockSpec((1, D), c2),
            pl.BlockSpec((1, D), c2),
            pl.BlockSpec((D, LANES), c2),
            pl.BlockSpec((D, LANES), c2),
            pl.BlockSpec((1, LANES), c2),
        ],
        out_specs=[
            pl.BlockSpec((tm, D), tok),
            pl.BlockSpec((tm, LANES), tok),
        ],
        out_shape=[
            jax.ShapeDtypeStruct((T, D), F32),
            jax.ShapeDtypeStruct((T, LANES), F32),
        ],
        compiler_params=pltpu.CompilerParams(
            dimension_semantics=("arbitrary",),
            vmem_limit_bytes=V7X_VMEM_LIMIT_BYTES),
        name="merge",
    )(x, u, u, attn, ln_g, ln_b, wgate, bgate, poolw, pscale, wpp, wpa, wout, ln1g, ln1b,
      wrh, wrl, br)


def _snake_group(tile, step):
    return jnp.where(tile % 2 == 0, step, N_GROUPS - 1 - step)


def _moe_kernel(h1_ref, g_ref, tri_ref, wg_ref, wu_ref, wd_ref, ln2g_ref, ln2b_ref, o_ref,
                xs_ref, ys_ref, dcol_ref, tab_ref):
    tm = TM_MOE
    step = pl.program_id(1)
    g = _snake_group(pl.program_id(0), step)
    lane = lax.broadcasted_iota(I32, (tm, LANES), 1).astype(F32)
    lane1 = lax.broadcasted_iota(I32, (1, LANES), 1).astype(F32)

    @pl.when(step == 0)
    def _():
        rec = g_ref[...]
        gsel = jnp.sum(jnp.where(lane == float(GSEL_LANE), rec, 0.0), axis=1, keepdims=True)
        onehot = jnp.where(lane == gsel, 1.0, 0.0)
        cum = _dot(tri_ref[...], onehot.astype(BF16))
        cnt = cum[tm - 1:tm, :]
        padded = jnp.floor((cnt + (MOE_CHUNK - 1)) * (1.0 / MOE_CHUNK)) * MOE_CHUNK
        start = jnp.zeros((1, LANES), F32)
        for gg in range(N_GROUPS - 1):
            p = jnp.sum(jnp.where(lane1 == float(gg), padded, 0.0), axis=1, keepdims=True)
            start = start + jnp.where(lane1 > float(gg), p, 0.0)
        tab_ref[0:1, :] = start
        tab_ref[1:2, :] = padded
        slot = jnp.sum(onehot * (start + cum - 1.0), axis=1, keepdims=True)
        slot_b = jnp.broadcast_to(slot, (tm, LANES))
        dcol_ref[...] = slot_b
        slot_row = slot_b.T[0:1, :]

        dense = jnp.where(lane < float(N_EXPERTS), rec, 0.0)
        own = jnp.where(gsel == 0.0, dense, 0.0)
        for gg in range(1, N_GROUPS):
            own = own + jnp.where(gsel == float(gg),
                                  pltpu.roll(dense, LANES - gg * EXPERTS_PER_GROUP, 1), 0.0)
        own = jnp.where(lane < float(EXPERTS_PER_GROUP), own, 0.0)
        own_hi = own.astype(BF16).astype(F32)
        side = (own_hi + pltpu.roll(own - own_hi, EXPERTS_PER_GROUP, 1)).astype(BF16)
        xa = jnp.concatenate([h1_ref[...].astype(BF16), side], axis=1)

        blk = MOE_SORT_ROWS
        for rb in range(MOE_SLOTS // blk):
            rows = rb * blk + lax.broadcasted_iota(I32, (blk, tm), 0).astype(F32)
            perm = jnp.where(rows == slot_row, 1.0, 0.0).astype(BF16)
            xs_ref[rb * blk:(rb + 1) * blk, :] = _dot(perm, xa).astype(BF16)
        ys_ref[...] = jnp.zeros(ys_ref.shape, BF16)

    gf = g.astype(F32)
    first = jnp.sum(jnp.where(lane1 == gf, tab_ref[0:1, :], 0.0)).astype(I32)
    n_chunks = jnp.sum(jnp.where(lane1 == gf, tab_ref[1:2, :], 0.0)).astype(I32) // MOE_CHUNK
    ff = EXPERTS_PER_GROUP * EXPERT_FF

    def ffn_chunk(c, carry):
        r0 = pl.multiple_of(first + c * MOE_CHUNK, MOE_CHUNK)
        xc = xs_ref[pl.ds(r0, MOE_CHUNK), :]
        xb = xc[:, :D_MODEL]
        gl = xc[:, D_MODEL:].astype(F32)
        gates = gl[:, :EXPERTS_PER_GROUP] + gl[:, EXPERTS_PER_GROUP:2 * EXPERTS_PER_GROUP]
        parts = []
        for j in range(EXPERTS_PER_GROUP):
            hid = jax.nn.silu(_dot(xb, wg_ref[j])) * _dot(xb, wu_ref[j])
            parts.append((hid * gates[:, j:j + 1]).astype(BF16))
        y = _dot(jnp.concatenate(parts, axis=1), wd_ref[...].reshape(ff, D_MODEL))
        ys_ref[pl.ds(r0, MOE_CHUNK), :] = y.astype(BF16)
        return carry

    lax.fori_loop(0, n_chunks, ffn_chunk, 0)

    @pl.when(step == N_GROUPS - 1)
    def _():
        slots = lax.broadcasted_iota(I32, (tm, MOE_SLOTS), 1).astype(F32)
        back = jnp.where(slots == dcol_ref[:, 0:1], 1.0, 0.0).astype(BF16)
        ffn = _dot(back, ys_ref[...])
        o_ref[...] = _layer_norm(DEEPNORM_ALPHA * h1_ref[...] + ffn, ln2g_ref[...], ln2b_ref[...])


def _moe(h1, route, wg, wu, wd, ln2g, ln2b):
    T, D = h1.shape
    tm = TM_MOE
    epg = EXPERTS_PER_GROUP
    grid = (T // tm, N_GROUPS)
    once = pl.Buffered(1)
    return pl.pallas_call(
        _moe_kernel,
        grid=grid,
        in_specs=[
            pl.BlockSpec((tm, D), lambda i, s: (i, 0), pipeline_mode=once),
            pl.BlockSpec((tm, LANES), lambda i, s: (i, 0), pipeline_mode=once),
            pl.BlockSpec((tm, tm), lambda i, s: (0, 0), pipeline_mode=once),
            pl.BlockSpec((epg, D, EXPERT_FF), lambda i, s: (_snake_group(i, s), 0, 0)),
            pl.BlockSpec((epg, D, EXPERT_FF), lambda i, s: (_snake_group(i, s), 0, 0)),
            pl.BlockSpec((epg, EXPERT_FF, D), lambda i, s: (_snake_group(i, s), 0, 0)),
            pl.BlockSpec((1, D), lambda i, s: (0, 0)),
            pl.BlockSpec((1, D), lambda i, s: (0, 0)),
        ],
        out_specs=pl.BlockSpec((tm, D), lambda i, s: (i, 0)),
        out_shape=jax.ShapeDtypeStruct((T, D), F32),
        scratch_shapes=[
            pltpu.VMEM((MOE_SLOTS, D + LANES), BF16),
            pltpu.VMEM((MOE_SLOTS, D), BF16),
            pltpu.VMEM((tm, LANES), F32),
            pltpu.VMEM((8, LANES), F32),
        ],
        compiler_params=pltpu.CompilerParams(
            dimension_semantics=("arbitrary", "arbitrary"),
            vmem_limit_bytes=V7X_VMEM_LIMIT_BYTES_MOE),
        name="moe",
    )(h1, route, jnp.tril(jnp.ones((tm, tm), BF16)), wg, wu, wd, ln2g, ln2b)


def kernel(x, ln_in_g, ln_in_b, w_in, b_gate, pool_w, pool_scale, w_proj_pool, w_proj_attn, w_out,
           ln1_g, ln1_b, w_group, b_group, w_router, b_router, w_gate, w_up, w_down, ln2_g, ln2_b):
    B, L, D = x.shape
    assert D == D_MODEL and w_in.shape[0] == DEPTH == 1
    assert L % TQ == 0 and L % TM_MERGE == 0 and L % TM_PROJ == 0
    assert L <= 4096
    T = B * L
    row = lambda v: v.reshape(1, -1).astype(F32)

    w = w_in[0]
    o = 0
    w_u = w[:, o:o + POOL_WIDTH]; o += POOL_WIDTH
    w_q = w[:, o:o + ATTN_WIDTH]; o += ATTN_WIDTH
    w_k = w[:, o:o + KV_DIM]; o += KV_DIM
    w_v = w[:, o:o + KV_DIM]; o += KV_DIM
    w_iq = w[:, o:o + IDX_HEADS * IDX_DIM]; o += IDX_HEADS * IDX_DIM
    w_ik = w[:, o:o + IDX_DIM]; o += IDX_DIM
    w_iw = w[:, o:o + IDX_HEADS]; o += IDX_HEADS
    w_g = w[:, o:]
    sm_scale = math.log2(math.e) / math.sqrt(HEAD_DIM)
    bf16_rows = 16
    w_t = jnp.concatenate([w_q * sm_scale, w_v, w_iq, w_iw,
                           jnp.zeros((D, bf16_rows - IDX_HEADS), F32)], axis=1).T.astype(BF16)

    u, k, ik, qw, vaug, iqt, iwt = _in_proj(
        x, row(ln_in_g), row(ln_in_b), w_u.astype(BF16), w_k.astype(BF16), w_ik.astype(BF16), w_t)
    attn = _dsa_attention(iqt, iwt, ik, qw, k, vaug)

    w_r = jnp.zeros((D, LANES), F32)
    w_r = w_r.at[:, :N_EXPERTS].set(w_router[0]).at[:, N_EXPERTS:N_EXPERTS + N_GROUPS].set(w_group[0])
    b_r = jnp.zeros((1, LANES), F32)
    b_r = b_r.at[0, :N_EXPERTS].set(b_router[0]).at[0, N_EXPERTS:N_EXPERTS + N_GROUPS].set(b_group[0])
    w_rh = w_r.astype(BF16)
    w_rl = (w_r - w_rh.astype(F32)).astype(BF16)

    h1, gates = _merge(
        x.reshape(T, D), u.reshape(T, POOL_WIDTH), attn.reshape(T, ATTN_WIDTH),
        row(ln_in_g), row(ln_in_b), w_g.astype(BF16), row(b_gate[0]),
        pool_w[0].astype(BF16), row(pool_scale[0]), w_proj_pool[0].astype(BF16),
        w_proj_attn[0].astype(BF16), w_out[0].astype(BF16), row(ln1_g[0]), row(ln1_b[0]),
        w_rh, w_rl, b_r, L)

    out = _moe(h1, gates, w_gate[0].astype(BF16), w_up[0].astype(BF16), w_down[0].astype(BF16),
               row(ln2_g[0]), row(ln2_b[0]))
    return out.reshape(B, L, D)
```

```python
import functools
import math

import jax
import jax.numpy as jnp
import numpy as np
from jax import lax
from jax.experimental import pallas as pl
from jax.experimental.pallas import tpu as pltpu

D_MODEL = 1024
POOL_WINDOWS = (2, 4, 8, 16)
POOL_GROUPS = 4
POOL_WIDTH = 512
POOL_GROUP_DIM = 128
N_HEADS = 8
HEAD_DIM = 64
ATTN_WIDTH = 512
KV_DIM = 64
IDX_HEADS = 8
IDX_DIM = 32
TOPK_MAX = 256
NEG_INF = float(np.float32(-1e30))
N_BRANCHES = 2
N_GROUPS = 4
EXPERTS_PER_GROUP = 8
N_EXPERTS = 32
EXPERT_FF = 256
LN_EPS = 1e-5
DEPTH = 1
DEEPNORM_ALPHA = (2.0 * DEPTH) ** 0.25

V7X_VMEM_LIMIT_BYTES = 56 * 1024 * 1024
V7X_VMEM_LIMIT_BYTES_MOE = 60 * 1024 * 1024
LANES = 128

F32 = jnp.float32
BF16 = jnp.bfloat16
I32 = jnp.int32

TM_PROJ = 512
TQ = 256
TK = 128
TM_MERGE = 512
POOL_HALO = 16
V_ONES_ROWS = 16
TM_MOE = 1024
MOE_CHUNK = 128
MOE_SLOTS = TM_MOE + N_GROUPS * MOE_CHUNK
MOE_SORT_ROWS = 256
GSEL_LANE = N_EXPERTS


def _layer_norm(x, g, b):
    mu = jnp.mean(x, axis=-1, keepdims=True)
    xc = x - mu
    var = jnp.mean(xc * xc, axis=-1, keepdims=True)
    return xc * lax.rsqrt(var + LN_EPS) * g + b


def _dot(a, b):
    return jnp.dot(a, b, preferred_element_type=F32)


def _dot_nt(a, b):
    return lax.dot_general(a, b, (((1,), (1,)), ((), ())), preferred_element_type=F32)


def _in_proj_kernel(x_ref, g_ref, b_ref, wu_ref, wk_ref, wik_ref, wt_ref,
                    u_ref, k_ref, ik_ref, qt_ref, vt_ref, iqt_ref, iwt_ref):
    h = _layer_norm(x_ref[...], g_ref[...], b_ref[...])
    hb = h.astype(BF16)
    u_ref[...] = _dot(hb, wu_ref[...])
    n_chunks = TM_PROJ // TK
    k_ref[...] = _dot(hb, wk_ref[...]).astype(BF16).reshape(n_chunks, TK, KV_DIM)
    ik_ref[...] = _dot(hb, wik_ref[...]).astype(BF16).reshape(n_chunks, TK, IDX_DIM)
    pt = _dot_nt(wt_ref[...], hb)
    r0 = 0
    for j in range(TM_PROJ // TQ):
        for h in range(N_HEADS):
            qt_ref[j, :, h * TQ:(h + 1) * TQ] = pt[r0 + h * HEAD_DIM:r0 + (h + 1) * HEAD_DIM,
                                                   j * TQ:(j + 1) * TQ].astype(BF16)
    r0 += ATTN_WIDTH
    for c in range(n_chunks):
        vt_ref[c, :KV_DIM, :] = pt[r0:r0 + KV_DIM, c * TK:(c + 1) * TK].astype(BF16)
        vt_ref[c, KV_DIM:, :] = jnp.ones((V_ONES_ROWS, TK), BF16)
    r0 += KV_DIM
    iqt_ref[...] = pt[r0:r0 + IDX_HEADS * IDX_DIM].astype(BF16)
    r0 += IDX_HEADS * IDX_DIM
    iwt_ref[...] = pt[r0:r0 + IDX_HEADS]


def _in_proj(x, ln_g, ln_b, wu, wk, wik, wt):
    B, L, D = x.shape
    tm = TM_PROJ
    grid = (B, L // tm)
    tok = lambda b, i: (b, i, 0)
    tokt = lambda b, i: (b, 0, i)
    chunked = lambda b, i: (b, i, 0, 0)
    const2 = lambda b, i: (0, 0)
    n_t = wt.shape[0]
    return pl.pallas_call(
        _in_proj_kernel,
        grid=grid,
        in_specs=[
            pl.BlockSpec((None, tm, D), tok),
            pl.BlockSpec((1, D), const2),
            pl.BlockSpec((1, D), const2),
            pl.BlockSpec((D, POOL_WIDTH), const2),
            pl.BlockSpec((D, KV_DIM), const2),
            pl.BlockSpec((D, IDX_DIM), const2),
            pl.BlockSpec((n_t, D), const2),
        ],
        out_specs=[
            pl.BlockSpec((None, tm, POOL_WIDTH), tok),
            pl.BlockSpec((None, tm // TK, TK, KV_DIM), chunked),
            pl.BlockSpec((None, tm // TK, TK, IDX_DIM), chunked),
            pl.BlockSpec((None, tm // TQ, HEAD_DIM, N_HEADS * TQ), chunked),
            pl.BlockSpec((None, tm // TK, KV_DIM + V_ONES_ROWS, TK), chunked),
            pl.BlockSpec((None, IDX_HEADS * IDX_DIM, tm), tokt),
            pl.BlockSpec((None, IDX_HEADS, tm), tokt),
        ],
        out_shape=[
            jax.ShapeDtypeStruct((B, L, POOL_WIDTH), F32),
            jax.ShapeDtypeStruct((B, L // TK, TK, KV_DIM), BF16),
            jax.ShapeDtypeStruct((B, L // TK, TK, IDX_DIM), BF16),
            jax.ShapeDtypeStruct((B, L // TQ, HEAD_DIM, N_HEADS * TQ), BF16),
            jax.ShapeDtypeStruct((B, L // TK, KV_DIM + V_ONES_ROWS, TK), BF16),
            jax.ShapeDtypeStruct((B, IDX_HEADS * IDX_DIM, L), BF16),
            jax.ShapeDtypeStruct((B, IDX_HEADS, L), F32),
        ],
        compiler_params=pltpu.CompilerParams(
            dimension_semantics=("arbitrary", "arbitrary"),
            vmem_limit_bytes=V7X_VMEM_LIMIT_BYTES),
        name="in_proj",
    )(x, ln_g, ln_b, wu, wk, wik, wt)


BRACKET_PASSES = 14
NO_TIE = 1e9


def _fori_pairs(n, body, init):
    def pair(j, carry):
        return body(2 * j + 1, body(2 * j, carry))
    carry = lax.fori_loop(0, n // 2, pair, init)
    return lax.fori_loop(2 * (n // 2), n, body, carry)


def _rows_to_sublanes(x, op):
    return op(x.reshape(x.shape[0] // 8, 8, TQ), axis=0)


SCAN_CHUNKS = 2
SCAN_ROWS = SCAN_CHUNKS * TK


def _dsa_kernel(iqt_ref, iwt_ref, ik_ref, qw_ref, k_ref, va_ref, o_ref,
                sc_ref, thr_ref, tie_ref, m_ref, acc_ref, *lg_refs, seq_len, n_sel):
    assert (TQ // TK) % 2 == 0
    qi = pl.program_id(1)
    q0 = qi * TQ
    nkc = (qi + 1) * (TQ // TK)
    n_beyond = seq_len - (qi + 1) * TQ
    k_sel = jnp.float32(n_sel)

    scan_rows = lax.broadcasted_iota(I32, (SCAN_ROWS, TQ), 0)
    scan_t = q0 + lax.broadcasted_iota(I32, (SCAN_ROWS, TQ), 1)
    n_scan = nkc // SCAN_CHUNKS

    def scores_at(b):
        return sc_ref[pl.ds(b * SCAN_CHUNKS, SCAN_CHUNKS)].reshape(SCAN_ROWS, TQ)

    def score_step(b, carry):
        smin, smax = carry
        ikc = ik_ref[pl.ds(b * SCAN_CHUNKS, SCAN_CHUNKS)].reshape(SCAN_ROWS, IDX_DIM)
        score = jnp.zeros((SCAN_ROWS, TQ), F32)
        for h in range(IDX_HEADS):
            lg = _dot(ikc, iqt_ref[h * IDX_DIM:(h + 1) * IDX_DIM, :])
            score = score + iwt_ref[h:h + 1, :] * jnp.maximum(lg, 0.0)
        smin = jnp.minimum(smin, jnp.min(score, axis=0, keepdims=True))
        smax = jnp.maximum(smax, jnp.max(score, axis=0, keepdims=True))
        score = jnp.where(b * SCAN_ROWS + scan_rows <= scan_t, score, NEG_INF)
        sc_ref[pl.ds(b * SCAN_CHUNKS, SCAN_CHUNKS)] = score.reshape(SCAN_CHUNKS, TK, TQ)
        return smin, smax

    big = jnp.float32(3e38)
    smin, smax = _fori_pairs(n_scan, score_step,
                             (jnp.full((1, TQ), big, F32), jnp.full((1, TQ), -big, F32)))

    nb_f = n_beyond.astype(F32)
    ninf = jnp.float32(-jnp.inf)

    def count_ge(cand):
        def body(b, acc):
            return acc + _rows_to_sublanes(jnp.where(scores_at(b) >= cand, 1.0, 0.0), jnp.sum)
        acc = _fori_pairs(n_scan, body, jnp.zeros((8, TQ), F32))
        return jnp.sum(acc, axis=0, keepdims=True) + jnp.where(cand <= NEG_INF, nb_f, 0.0)

    def max_below(h):
        def body(b, acc):
            s = scores_at(b)
            return jnp.maximum(acc, _rows_to_sublanes(jnp.where(s < h, s, ninf), jnp.max))
        acc = _fori_pairs(n_scan, body, jnp.full((8, TQ), ninf, F32))
        return jnp.max(acc, axis=0, keepdims=True)

    def count_ge_and_max_below(v):
        def body(b, carry):
            acc, mx = carry
            s = scores_at(b)
            acc = acc + _rows_to_sublanes(jnp.where(s >= v, 1.0, 0.0), jnp.sum)
            mx = jnp.maximum(mx, _rows_to_sublanes(jnp.where(s < v, s, ninf), jnp.max))
            return acc, mx
        acc, mx = _fori_pairs(n_scan, body,
                              (jnp.zeros((8, TQ), F32), jnp.full((8, TQ), ninf, F32)))
        cnt = jnp.sum(acc, axis=0, keepdims=True) + jnp.where(v <= NEG_INF, nb_f, 0.0)
        return cnt, jnp.max(mx, axis=0, keepdims=True)

    n_adm = (q0 + 1 + lax.broadcasted_iota(I32, (1, TQ), 1)).astype(F32)
    few = n_adm < k_sel
    lo0 = jnp.where(few, NEG_INF, smin)
    clo0 = jnp.where(few, k_sel, jnp.where(smin <= NEG_INF, jnp.float32(seq_len), n_adm))
    hi0 = smax + (jnp.abs(smax) * 1e-6 + 1e-30)
    chi0 = jnp.zeros((1, TQ), F32)

    def bracket_body(it, st):
        lo, hi, clo, chi, flo, fhi, side = st
        done = clo == k_sel
        frac = jnp.clip(flo / (flo - fhi), 1.0 / 512, 511.0 / 512)
        frac = jnp.where(clo - chi <= 2.0, 0.5, frac)
        cand = lo + (hi - lo) * frac
        zero_inside = jnp.logical_and(jnp.logical_and(lo < 0.0, hi > 0.0), it == 0)
        cand = jnp.where(done, lo, jnp.where(zero_inside, 0.0, cand))
        cnt = count_ge(cand)
        ge = cnt >= k_sel
        f = cnt - (k_sel - 0.5)
        new_side = jnp.where(ge, 1.0, -1.0)
        same = new_side == side
        flo_n = jnp.where(ge, f, jnp.where(same, flo * 0.5, flo))
        fhi_n = jnp.where(ge, jnp.where(same, fhi * 0.5, fhi), f)
        up_lo = jnp.logical_and(jnp.logical_not(done), ge)
        up_hi = jnp.logical_and(jnp.logical_not(done), jnp.logical_not(ge))
        lo = jnp.where(up_lo, cand, lo)
        clo = jnp.where(up_lo, cnt, clo)
        hi = jnp.where(up_hi, cand, hi)
        chi = jnp.where(up_hi, cnt, chi)
        flo = jnp.where(done, flo, flo_n)
        fhi = jnp.where(done, fhi, fhi_n)
        side = jnp.where(done, side, new_side)
        return lo, hi, clo, chi, flo, fhi, side

    lo, hi, clo, chi = lax.fori_loop(
        0, BRACKET_PASSES, bracket_body,
        (lo0, hi0, clo0, chi0, clo0 - (k_sel - 0.5), chi0 - (k_sel - 0.5),
         jnp.zeros((1, TQ), F32)))[:4]
    pending = jnp.sum(jnp.where(clo == k_sel, 0.0, 1.0))
    thr_ref[...] = lo
    tie_ref[...] = jnp.full((1, TQ), NO_TIE, F32)

    @pl.when(pending > 0.5)
    def _():
        fin0 = jnp.where(clo == k_sel, 1.0, 0.0)
        v0 = max_below(hi)

        def fin_cond(st):
            return jnp.logical_and(st[0] < seq_len + 2, st[1] > 0.5)

        def fin_body(st):
            j, _, fin, h, ch, v, kst, need = st
            cnt, v2 = count_ge_and_max_below(v)
            hit = jnp.logical_and(fin < 0.5, cnt >= k_sel)
            kst = jnp.where(hit, v, kst)
            need = jnp.where(jnp.logical_and(hit, cnt > k_sel), k_sel - ch, need)
            fin = jnp.where(hit, 1.0, fin)
            open_ = fin < 0.5
            h = jnp.where(open_, v, h)
            ch = jnp.where(open_, cnt, ch)
            v = jnp.where(open_, v2, v)
            return j + 1, jnp.sum(1.0 - fin), fin, h, ch, v, kst, need

        st2 = lax.while_loop(
            fin_cond, fin_body,
            (jnp.int32(0), pending, fin0, hi, chi, v0, lo, jnp.full((1, TQ), NO_TIE, F32)))
        thr_ref[...] = st2[6]
        tie_ref[...] = st2[7]

    kstar = thr_ref[...]
    need = tie_ref[...]
    fix = jnp.logical_or(need < NO_TIE, kstar <= NEG_INF)

    @pl.when(jnp.sum(jnp.where(fix, 1.0, 0.0)) > 0.5)
    def _():
        tri = jnp.where(lax.broadcasted_iota(I32, (SCAN_ROWS, SCAN_ROWS), 0)
                        >= lax.broadcasted_iota(I32, (SCAN_ROWS, SCAN_ROWS), 1), 1.0, 0.0).astype(BF16)

        def rewrite(b, seen):
            s = scores_at(b)
            tied = jnp.where(s == kstar, 1.0, 0.0)
            rank = seen + _dot(tri, tied.astype(BF16))
            tie_take = jnp.where(rank <= need, 2.0 * tied - 1.0, -1.0)
            take = jnp.where(s > kstar, 1.0, tie_take)
            take = jnp.where(b * SCAN_ROWS + scan_rows <= scan_t, take, -1.0)
            sc_ref[pl.ds(b * SCAN_CHUNKS, SCAN_CHUNKS)] = take.reshape(SCAN_CHUNKS, TK, TQ)
            return seen + jnp.sum(tied, axis=0, keepdims=True)

        _fori_pairs(n_scan, rewrite, jnp.zeros((1, TQ), F32))
        thr_ref[...] = jnp.zeros((1, TQ), F32)

    m_ref[...] = jnp.full(m_ref.shape, NEG_INF, F32)
    acc_ref[...] = jnp.zeros(acc_ref.shape, F32)
    thr = thr_ref[...]

    def masked_logits(c):
        bias = jnp.where(sc_ref[c] >= thr, 0.0, NEG_INF)
        lg = _dot(k_ref[c], qw_ref[...])
        lg = jnp.concatenate(
            [lg[:, h * TQ:(h + 1) * TQ] + bias for h in range(N_HEADS)], axis=1)
        return lg, jnp.max(lg, axis=0, keepdims=True)

    n_pairs = nkc // 2

    def produce_pair(j, a_ref, b_ref):
        jc = jnp.minimum(j, n_pairs - 1)
        lg_a, max_a = masked_logits(2 * jc)
        lg_b, max_b = masked_logits(2 * jc + 1)
        a_ref[...] = lg_a
        b_ref[...] = lg_b
        return jnp.maximum(max_a, max_b)

    def consume_pair(j, a_ref, b_ref, pair_max):
        m_old = m_ref[...]
        m_new = jnp.maximum(m_old, pair_max)
        alpha = jnp.exp2(m_old - m_new)
        p = jnp.concatenate([jnp.exp2(a_ref[...] - m_new).astype(BF16),
                             jnp.exp2(b_ref[...] - m_new).astype(BF16)], axis=0)
        va = jnp.concatenate([va_ref[2 * j], va_ref[2 * j + 1]], axis=1)
        acc_ref[...] = alpha * acc_ref[...] + _dot(va, p)
        m_ref[...] = m_new

    def pipeline_step(cur, nxt):
        def body(j, pair_max):
            next_max = produce_pair(j + 1, *nxt)
            consume_pair(j, *cur, pair_max)
            return next_max
        return body

    even = pipeline_step(lg_refs[0:2], lg_refs[2:4])
    odd = pipeline_step(lg_refs[2:4], lg_refs[0:2])
    pair_max = produce_pair(0, *lg_refs[0:2])
    pair_max = lax.fori_loop(0, n_pairs // 2, lambda t, c: odd(2 * t + 1, even(2 * t, c)), pair_max)
    lax.fori_loop(2 * (n_pairs // 2), n_pairs, even, pair_max)

    a = acc_ref[...]
    o = a[:HEAD_DIM] / a[HEAD_DIM:HEAD_DIM + 1]
    o = jnp.concatenate([o[:, h * TQ:(h + 1) * TQ] for h in range(N_HEADS)], axis=0)
    o_ref[...] = o.T


def _dsa_attention(iqt, iwt, ik, qw, k, vaug):
    B, _, L = iqt.shape
    n_sel = min(TOPK_MAX, L // 4)
    nch = L // TK
    va_rows = vaug.shape[2]
    grid = (B, L // TQ)
    kern = functools.partial(_dsa_kernel, seq_len=L, n_sel=n_sel)
    return pl.pallas_call(
        kern,
        grid=grid,
        in_specs=[
            pl.BlockSpec((None, IDX_HEADS * IDX_DIM, TQ), lambda b, i: (b, 0, i)),
            pl.BlockSpec((None, IDX_HEADS, TQ), lambda b, i: (b, 0, i)),
            pl.BlockSpec((None, nch, TK, IDX_DIM), lambda b, i: (b, 0, 0, 0)),
            pl.BlockSpec((None, None, HEAD_DIM, N_HEADS * TQ), lambda b, i: (b, i, 0, 0)),
            pl.BlockSpec((None, nch, TK, HEAD_DIM), lambda b, i: (b, 0, 0, 0)),
            pl.BlockSpec((None, nch, va_rows, TK), lambda b, i: (b, 0, 0, 0)),
        ],
        out_specs=pl.BlockSpec((None, TQ, ATTN_WIDTH), lambda b, i: (b, i, 0)),
        out_shape=jax.ShapeDtypeStruct((B, L, ATTN_WIDTH), F32),
        scratch_shapes=[
            pltpu.VMEM((nch, TK, TQ), F32),
            pltpu.VMEM((1, TQ), F32),
            pltpu.VMEM((1, TQ), F32),
            pltpu.VMEM((1, N_HEADS * TQ), F32),
            pltpu.VMEM((va_rows, N_HEADS * TQ), F32),
        ] + [pltpu.VMEM((TK, N_HEADS * TQ), F32)] * 4,
        compiler_params=pltpu.CompilerParams(
            dimension_semantics=("arbitrary", "arbitrary"),
            vmem_limit_bytes=V7X_VMEM_LIMIT_BYTES),
        name="dsa_attn",
    )(iqt, iwt, ik, qw, k, vaug)


def _merge_kernel(x_ref, u_ref, uh_ref, a_ref, lng_ref, lnb_ref, wgate_ref, bgate_ref,
                  poolw_ref, pscale_ref, wpp_ref, wpa_ref, wout_ref, ln1g_ref, ln1b_ref,
                  wrh_ref, wrl_ref, br_ref, h1_ref, gates_ref, *, tiles_per_seq):
    tm = TM_MERGE
    i = pl.program_id(0)
    seq_start = (i % tiles_per_seq) == 0
    h = _layer_norm(x_ref[...], lng_ref[...], lnb_ref[...])
    hb = h.astype(BF16)

    halo = jnp.where(seq_start, 0.0, uh_ref[...])
    u = u_ref[...]
    ext = jnp.concatenate([halo, u], axis=0)
    pos = (i % tiles_per_seq) * tm + lax.broadcasted_iota(I32, (tm, 1), 0)
    mixed = []
    for g, w in enumerate(POOL_WINDOWS):
        s = ext[:, g * POOL_GROUP_DIM:(g + 1) * POOL_GROUP_DIM]
        span = 1
        while span < w:
            s = s + pltpu.roll(s, span, 0)
            span *= 2
        win = s[POOL_HALO:]
        cnt = jnp.minimum(pos + 1, w).astype(F32)
        ug = u[:, g * POOL_GROUP_DIM:(g + 1) * POOL_GROUP_DIM]
        delta = win / cnt - ug
        mixed.append(_dot(delta.astype(BF16), poolw_ref[g]))
    pool_out = jnp.concatenate(mixed, axis=1) * pscale_ref[...]

    gate_pre = _dot(hb, wgate_ref[...]) + bgate_ref[...]
    gates = jax.nn.sigmoid(gate_pre)
    bp = _dot(pool_out.astype(BF16), wpp_ref[...])
    ba = _dot(a_ref[...].astype(BF16), wpa_ref[...])
    merged = gates[:, :D_MODEL] * bp + gates[:, D_MODEL:] * ba
    mix = _dot(merged.astype(BF16), wout_ref[...])
    h1 = _layer_norm(DEEPNORM_ALPHA * h + mix, ln1g_ref[...], ln1b_ref[...])
    h1_ref[...] = h1

    hi = h1.astype(BF16)
    lo = (h1 - hi.astype(F32)).astype(BF16)
    lg = _dot(hi, wrh_ref[...]) + (_dot(lo, wrh_ref[...]) + _dot(hi, wrl_ref[...])) + br_ref[...]
    lane = lax.broadcasted_iota(I32, (tm, LANES), 1).astype(F32)
    big = jnp.float32(1 << 20)
    ninf = jnp.float32(-jnp.inf)
    is_g = jnp.logical_and(lane >= N_EXPERTS, lane < N_EXPERTS + N_GROUPS)
    gl = jnp.where(is_g, lg, ninf)
    gmax = jnp.max(gl, axis=1, keepdims=True)
    gsel = jnp.min(jnp.where(gl == gmax, lane, big), axis=1, keepdims=True) - N_EXPERTS
    sumexp = jnp.sum(jnp.where(is_g, jnp.exp(gl - gmax), 0.0), axis=1, keepdims=True)
    p_group = 1.0 / sumexp
    e_lo = gsel * EXPERTS_PER_GROUP
    in_grp = jnp.logical_and(lane >= e_lo, lane < e_lo + EXPERTS_PER_GROUP)
    el = jnp.where(in_grp, lg, ninf)
    m1 = jnp.max(el, axis=1, keepdims=True)
    i1 = jnp.min(jnp.where(el == m1, lane, big), axis=1, keepdims=True)
    el2 = jnp.where(lane == i1, ninf, el)
    m2 = jnp.max(el2, axis=1, keepdims=True)
    i2 = jnp.min(jnp.where(el2 == m2, lane, big), axis=1, keepdims=True)
    e2 = jnp.exp(m2 - m1)
    den = 1.0 + e2
    w1 = (1.0 / den) * p_group
    w2 = (e2 / den) * p_group
    dense = jnp.where(lane == i1, w1, 0.0) + jnp.where(lane == i2, w2, 0.0)
    gates_ref[...] = jnp.where(lane == float(GSEL_LANE), gsel, dense)


def _merge(x, u, attn, ln_g, ln_b, wgate, bgate, poolw, pscale, wpp, wpa, wout, ln1g, ln1b,
           wrh, wrl, br, seq_len):
    T, D = x.shape
    tm = TM_MERGE
    tiles_per_seq = seq_len // tm
    grid = (T // tm,)
    tok = lambda i: (i, 0)
    c2 = lambda i: (0, 0)
    c3 = lambda i: (0, 0, 0)
    halo_blocks = tm // POOL_HALO
    kern = functools.partial(_merge_kernel, tiles_per_seq=tiles_per_seq)
    return pl.pallas_call(
        kern,
        grid=grid,
        in_specs=[
            pl.BlockSpec((tm, D), tok),
            pl.BlockSpec((tm, POOL_WIDTH), tok),
            pl.BlockSpec((POOL_HALO, POOL_WIDTH),
                         lambda i: (jnp.maximum(i * halo_blocks - 1, 0), 0)),
            pl.BlockSpec((tm, ATTN_WIDTH), tok),
            pl.BlockSpec((1, D), c2),
            pl.BlockSpec((1, D), c2),
            pl.BlockSpec((D, N_BRANCHES * D), c2),
            pl.BlockSpec((1, N_BRANCHES * D), c2),
            pl.BlockSpec((POOL_GROUPS, POOL_GROUP_DIM, POOL_GROUP_DIM), c3),
            pl.BlockSpec((1, POOL_WIDTH), c2),
            pl.BlockSpec((POOL_WIDTH, D), c2),
            pl.BlockSpec((ATTN_WIDTH, D), c2),
            pl.BlockSpec((D, D), c2),
            pl.BlockSpec((1, D), c2),
            pl.BlockSpec((1, D), c2),
            pl.BlockSpec((D, LANES), c2),
            pl.BlockSpec((D, LANES), c2),
            pl.BlockSpec((1, LANES), c2),
        ],
        out_specs=[
            pl.BlockSpec((tm, D), tok),
            pl.BlockSpec((tm, LANES), tok),
        ],
        out_shape=[
            jax.ShapeDtypeStruct((T, D), F32),
            jax.ShapeDtypeStruct((T, LANES), F32),
        ],
        compiler_params=pltpu.CompilerParams(
            dimension_semantics=("arbitrary",),
            vmem_limit_bytes=V7X_VMEM_LIMIT_BYTES),
        name="merge",
    )(x, u, u, attn, ln_g, ln_b, wgate, bgate, poolw, pscale, wpp, wpa, wout, ln1g, ln1b,
      wrh, wrl, br)


def _snake_group(tile, step):
    return jnp.where(tile % 2 == 0, step, N_GROUPS - 1 - step)


def _moe_kernel(h1_ref, g_ref, tri_ref, wg_ref, wu_ref, wd_ref, ln2g_ref, ln2b_ref, o_ref,
                xs_ref, ys_ref, dcol_ref, tab_ref):
    tm = TM_MOE
    step = pl.program_id(1)
    g = _snake_group(pl.program_id(0), step)
    lane = lax.broadcasted_iota(I32, (tm, LANES), 1).astype(F32)
    lane1 = lax.broadcasted_iota(I32, (1, LANES), 1).astype(F32)

    @pl.when(step == 0)
    def _():
        rec = g_ref[...]
        gsel = jnp.sum(jnp.where(lane == float(GSEL_LANE), rec, 0.0), axis=1, keepdims=True)
        onehot = jnp.where(lane == gsel, 1.0, 0.0)
        cum = _dot(tri_ref[...], onehot.astype(BF16))
        cnt = cum[tm - 1:tm, :]
        padded = jnp.floor((cnt + (MOE_CHUNK - 1)) * (1.0 / MOE_CHUNK)) * MOE_CHUNK
        start = jnp.zeros((1, LANES), F32)
        for gg in range(N_GROUPS - 1):
            p = jnp.sum(jnp.where(lane1 == float(gg), padded, 0.0), axis=1, keepdims=True)
            start = start + jnp.where(lane1 > float(gg), p, 0.0)
        tab_ref[0:1, :] = start
        tab_ref[1:2, :] = padded
        slot = jnp.sum(onehot * (start + cum - 1.0), axis=1, keepdims=True)
        slot_b = jnp.broadcast_to(slot, (tm, LANES))
        dcol_ref[...] = slot_b
        slot_row = slot_b.T[0:1, :]

        dense = jnp.where(lane < float(N_EXPERTS), rec, 0.0)
        own = jnp.where(gsel == 0.0, dense, 0.0)
        for gg in range(1, N_GROUPS):
            own = own + jnp.where(gsel == float(gg),
                                  pltpu.roll(dense, LANES - gg * EXPERTS_PER_GROUP, 1), 0.0)
        own = jnp.where(lane < float(EXPERTS_PER_GROUP), own, 0.0)
        own_hi = own.astype(BF16).astype(F32)
        side = (own_hi + pltpu.roll(own - own_hi, EXPERTS_PER_GROUP, 1)).astype(BF16)
        xa = jnp.concatenate([h1_ref[...].astype(BF16), side], axis=1)

        blk = MOE_SORT_ROWS
        used = jnp.sum(jnp.where(lane1 < float(N_GROUPS), padded, 0.0))

        def sort_block(rb):
            rows = rb * blk + lax.broadcasted_iota(I32, (blk, tm), 0).astype(F32)
            perm = jnp.where(rows == slot_row, 1.0, 0.0).astype(BF16)
            xs_ref[rb * blk:(rb + 1) * blk, :] = _dot(perm, xa).astype(BF16)

        for rb in range(MOE_SLOTS // blk):
            if (rb + 1) * blk <= tm:
                sort_block(rb)
            else:
                pl.when(used > float(rb * blk))(functools.partial(sort_block, rb))
        ys_ref[...] = jnp.zeros(ys_ref.shape, BF16)

    gf = g.astype(F32)
    first = jnp.sum(jnp.where(lane1 == gf, tab_ref[0:1, :], 0.0)).astype(I32)
    n_chunks = jnp.sum(jnp.where(lane1 == gf, tab_ref[1:2, :], 0.0)).astype(I32) // MOE_CHUNK
    ff = EXPERTS_PER_GROUP * EXPERT_FF

    def ffn_chunk(c, carry):
        r0 = pl.multiple_of(first + c * MOE_CHUNK, MOE_CHUNK)
        xc = xs_ref[pl.ds(r0, MOE_CHUNK), :]
        xb = xc[:, :D_MODEL]
        gl = xc[:, D_MODEL:].astype(F32)
        gates = gl[:, :EXPERTS_PER_GROUP] + gl[:, EXPERTS_PER_GROUP:2 * EXPERTS_PER_GROUP]
        parts = []
        for j in range(EXPERTS_PER_GROUP):
            hid = jax.nn.silu(_dot(xb, wg_ref[j])) * _dot(xb, wu_ref[j])
            parts.append((hid * gates[:, j:j + 1]).astype(BF16))
        y = _dot(jnp.concatenate(parts, axis=1), wd_ref[...].reshape(ff, D_MODEL))
        ys_ref[pl.ds(r0, MOE_CHUNK), :] = y.astype(BF16)
        return carry

    lax.fori_loop(0, n_chunks, ffn_chunk, 0)

    @pl.when(step == N_GROUPS - 1)
    def _():
        slots = lax.broadcasted_iota(I32, (tm, MOE_SLOTS), 1).astype(F32)
        back = jnp.where(slots == dcol_ref[:, 0:1], 1.0, 0.0).astype(BF16)
        ffn = _dot(back, ys_ref[...])
        o_ref[...] = _layer_norm(DEEPNORM_ALPHA * h1_ref[...] + ffn, ln2g_ref[...], ln2b_ref[...])


def _moe(h1, route, wg, wu, wd, ln2g, ln2b):
    T, D = h1.shape
    tm = TM_MOE
    epg = EXPERTS_PER_GROUP
    grid = (T // tm, N_GROUPS)
    once = pl.Buffered(1)
    return pl.pallas_call(
        _moe_kernel,
        grid=grid,
        in_specs=[
            pl.BlockSpec((tm, D), lambda i, s: (i, 0), pipeline_mode=once),
            pl.BlockSpec((tm, LANES), lambda i, s: (i, 0), pipeline_mode=once),
            pl.BlockSpec((tm, tm), lambda i, s: (0, 0), pipeline_mode=once),
            pl.BlockSpec((epg, D, EXPERT_FF), lambda i, s: (_snake_group(i, s), 0, 0)),
            pl.BlockSpec((epg, D, EXPERT_FF), lambda i, s: (_snake_group(i, s), 0, 0)),
            pl.BlockSpec((epg, EXPERT_FF, D), lambda i, s: (_snake_group(i, s), 0, 0)),
            pl.BlockSpec((1, D), lambda i, s: (0, 0)),
            pl.BlockSpec((1, D), lambda i, s: (0, 0)),
        ],
        out_specs=pl.BlockSpec((tm, D), lambda i, s: (i, 0)),
        out_shape=jax.ShapeDtypeStruct((T, D), F32),
        scratch_shapes=[
            pltpu.VMEM((MOE_SLOTS, D + LANES), BF16),
            pltpu.VMEM((MOE_SLOTS, D), BF16),
            pltpu.VMEM((tm, LANES), F32),
            pltpu.VMEM((8, LANES), F32),
        ],
        compiler_params=pltpu.CompilerParams(
            dimension_semantics=("arbitrary", "arbitrary"),
            vmem_limit_bytes=V7X_VMEM_LIMIT_BYTES_MOE),
        name="moe",
    )(h1, route, jnp.tril(jnp.ones((tm, tm), BF16)), wg, wu, wd, ln2g, ln2b)


def kernel(x, ln_in_g, ln_in_b, w_in, b_gate, pool_w, pool_scale, w_proj_pool, w_proj_attn, w_out,
           ln1_g, ln1_b, w_group, b_group, w_router, b_router, w_gate, w_up, w_down, ln2_g, ln2_b):
    B, L, D = x.shape
    assert D == D_MODEL and w_in.shape[0] == DEPTH == 1
    assert L % TQ == 0 and L % TM_MERGE == 0 and L % TM_PROJ == 0
    assert L <= 4096
    T = B * L
    row = lambda v: v.reshape(1, -1).astype(F32)

    w = w_in[0]
    o = 0
    w_u = w[:, o:o + POOL_WIDTH]; o += POOL_WIDTH
    w_q = w[:, o:o + ATTN_WIDTH]; o += ATTN_WIDTH
    w_k = w[:, o:o + KV_DIM]; o += KV_DIM
    w_v = w[:, o:o + KV_DIM]; o += KV_DIM
    w_iq = w[:, o:o + IDX_HEADS * IDX_DIM]; o += IDX_HEADS * IDX_DIM
    w_ik = w[:, o:o + IDX_DIM]; o += IDX_DIM
    w_iw = w[:, o:o + IDX_HEADS]; o += IDX_HEADS
    w_g = w[:, o:]
    sm_scale = math.log2(math.e) / math.sqrt(HEAD_DIM)
    bf16_rows = 16
    w_t = jnp.concatenate([w_q * sm_scale, w_v, w_iq, w_iw,
                           jnp.zeros((D, bf16_rows - IDX_HEADS), F32)], axis=1).T.astype(BF16)

    u, k, ik, qw, vaug, iqt, iwt = _in_proj(
        x, row(ln_in_g), row(ln_in_b), w_u.astype(BF16), w_k.astype(BF16), w_ik.astype(BF16), w_t)
    attn = _dsa_attention(iqt, iwt, ik, qw, k, vaug)

    w_r = jnp.zeros((D, LANES), F32)
    w_r = w_r.at[:, :N_EXPERTS].set(w_router[0]).at[:, N_EXPERTS:N_EXPERTS + N_GROUPS].set(w_group[0])
    b_r = jnp.zeros((1, LANES), F32)
    b_r = b_r.at[0, :N_EXPERTS].set(b_router[0]).at[0, N_EXPERTS:N_EXPERTS + N_GROUPS].set(b_group[0])
    w_rh = w_r.astype(BF16)
    w_rl = (w_r - w_rh.astype(F32)).astype(BF16)

    h1, gates = _merge(
        x.reshape(T, D), u.reshape(T, POOL_WIDTH), attn.reshape(T, ATTN_WIDTH),
        row(ln_in_g), row(ln_in_b), w_g.astype(BF16), row(b_gate[0]),
        pool_w[0].astype(BF16), row(pool_scale[0]), w_proj_pool[0].astype(BF16),
        w_proj_attn[0].astype(BF16), w_out[0].astype(BF16), row(ln1_g[0]), row(ln1_b[0]),
        w_rh, w_rl, b_r, L)

    out = _moe(h1, gates, w_gate[0].astype(BF16), w_up[0].astype(BF16), w_down[0].astype(BF16),
               row(ln2_g[0]), row(ln2_b[0]))
    return out.reshape(B, L, D)
```
